```python
import math, functools
import jax, jax.numpy as jnp
from jax import lax
import numpy as np

D_MODEL = 4096
BATCH = 32
SEQ = 256
DEPTH = 4
DEC_BATCH = 2
DEC_SEQ = 1024
PAST_LEN = 512

GRID_W = 64
MIX_WIDTH = D_MODEL
D_HYENA = MIX_WIDTH // 2
D_ATTN = MIX_WIDTH - D_HYENA
HEAD_DIM = 128
N_HEADS = D_ATTN // HEAD_DIM
N_KV_HEADS = 4
KV_GROUP = N_HEADS // N_KV_HEADS
KV_DIM = N_KV_HEADS * HEAD_DIM
WINDOW = 128
BLOCK = 128
ATTN_SCALE = 1.0 / math.sqrt(HEAD_DIM)
ROPE_BASE = 10000.0
FILTER_EMB = 33
FILTER_BANDS = (FILTER_EMB - 1) // 2
FILTER_ORDER = 64
DECAY_TARGET = 1e-2
FAST_DECAY_PCT = 0.3
SLOW_DECAY_PCT = 1.5
MAX_DECAY = math.log(DECAY_TARGET) / FAST_DECAY_PCT
MIN_DECAY = math.log(DECAY_TARGET) / SLOW_DECAY_PCT
SHORT_CONV = 3
D_FF = 11008
FFN_CONV = 3
RMS_EPS = 1e-6
NEG_INF = -1e30
O_Q = 3 * D_HYENA
O_K = O_Q + D_ATTN
O_V = O_K + KV_DIM
D_IN = O_V + KV_DIM

kernel_name = "hyena_swa_hybrid_dit_step"


def rmsnorm(x, g):
    xf = x.astype(jnp.float32)
    y = xf * lax.rsqrt(jnp.mean(xf * xf, axis=-1, keepdims=True) + RMS_EPS)
    return (y * g.astype(jnp.float32)).astype(x.dtype)


def dwconv3(x, w, b):
    L = x.shape[1]
    xp = jnp.pad(x, ((0, 0), (1, 1), (0, 0)))
    return xp[:, :L] * w[:, 0] + xp[:, 1:L + 1] * w[:, 1] + xp[:, 2:] * w[:, 2] + b


def hyena_filters(L, w1, b1, w2, b2, w3, freq):
    f32 = jnp.float32
    t = jnp.linspace(0.0, 1.0, L, dtype=f32)[:, None]
    w = 2.0 * math.pi * jnp.arange(L, dtype=f32)[:, None] / L
    f = jnp.linspace(1e-4, FILTER_BANDS - 1, FILTER_BANDS, dtype=f32)[None, :]
    z = jnp.concatenate([t, jnp.cos(f * w), -jnp.sin(f * w)], axis=-1)
    fr = freq.astype(f32)
    h = jnp.sin(fr * (z @ w1.astype(f32) + b1.astype(f32)))
    h = jnp.sin(fr * (h @ w2.astype(f32) + b2.astype(f32)))
    h = h @ w3.astype(f32)
    deltas = jnp.linspace(MIN_DECAY, MAX_DECAY, D_HYENA, dtype=f32)
    decay = jnp.exp(-t * jnp.abs(deltas)[None, :])
    return h[:, :D_HYENA] * decay, h[:, D_HYENA:] * decay


def bidir_long_conv(u, h_fwd, h_bwd):
    L, C = u.shape[1], u.shape[2]
    kern = jnp.concatenate([h_fwd, jnp.zeros((1, C), jnp.float32), h_bwd[1:][::-1]], axis=0)
    U = jnp.fft.rfft(u.astype(jnp.float32), n=2 * L, axis=1)
    K = jnp.fft.rfft(kern, n=2 * L, axis=0)
    y = jnp.fft.irfft(U * K[None], n=2 * L, axis=1)[:, :L]
    return y.astype(u.dtype)


def hyena_mixer(u3, lp):
    L = u3.shape[1]
    x0 = u3[..., :D_HYENA]
    x1 = u3[..., D_HYENA:2 * D_HYENA]
    v = u3[..., 2 * D_HYENA:]
    h_fwd, h_bwd = hyena_filters(L, lp["filt_w1"], lp["filt_b1"], lp["filt_w2"], lp["filt_b2"],
                                 lp["filt_w3"], lp["filt_freq"])
    u = x1 * v
    y = bidir_long_conv(u, h_fwd, h_bwd) + u * lp["hy_bias"]
    return x0 * y


def axial_rope(x, row_pos, col_pos):
    f32 = jnp.float32
    half = HEAD_DIM // 2
    quarter = half // 2
    inv = ROPE_BASE ** (-jnp.arange(quarter, dtype=f32) / quarter)

    def rotate(xa, pos):
        ang = pos.astype(f32)[:, None] * inv[None, :]
        cos = jnp.cos(ang)[None, :, None, :]
        sin = jnp.sin(ang)[None, :, None, :]
        xa = xa.astype(f32)
        x1, x2 = xa[..., :quarter], xa[..., quarter:]
        return jnp.concatenate([x1 * cos - x2 * sin, x1 * sin + x2 * cos], axis=-1)

    out = jnp.concatenate([rotate(x[..., :half], row_pos), rotate(x[..., half:], col_pos)], axis=-1)
    return out.astype(x.dtype)


def sink_softmax(s, sink):
    sk = jnp.broadcast_to(sink.astype(jnp.float32).reshape(N_KV_HEADS, KV_GROUP, 1, 1),
                          s.shape[:-1] + (1,))
    p = jax.nn.softmax(jnp.concatenate([s, sk], axis=-1), axis=-1)
    return p[..., :-1]


def attend_context(q, k, v, sink):
    B, L = q.shape[0], q.shape[1]
    nb = L // BLOCK
    qb = jnp.moveaxis(q.reshape(B, nb, BLOCK, N_KV_HEADS, KV_GROUP, HEAD_DIM), 1, 0)

    def one_block(q_blk):
        s = jnp.einsum("bqkgd,bskd->bkgqs", q_blk, k, preferred_element_type=jnp.float32) * ATTN_SCALE
        p = sink_softmax(s, sink)
        return jnp.einsum("bkgqs,bskd->bqkgd", p.astype(v.dtype), v)

    out = lax.map(one_block, qb)
    return jnp.moveaxis(out, 0, 1).reshape(B, L, D_ATTN)


def attend_latent(q, k, v, k_ctx, v_ctx, sink, row_pos, col_pos):
    q = axial_rope(q, row_pos, col_pos)
    k = axial_rope(k, row_pos, col_pos)
    B, L = q.shape[0], q.shape[1]
    nb = L // BLOCK
    span = BLOCK + 2 * WINDOW
    kp = jnp.pad(k, ((0, 0), (WINDOW, WINDOW), (0, 0), (0, 0)))
    vp = jnp.pad(v, ((0, 0), (WINDOW, WINDOW), (0, 0), (0, 0)))
    offs_q = jnp.arange(BLOCK)
    offs_k = jnp.arange(span) - WINDOW

    def one_block(i):
        start = i * BLOCK
        q_blk = lax.dynamic_slice_in_dim(q, start, BLOCK, axis=1).reshape(
            B, BLOCK, N_KV_HEADS, KV_GROUP, HEAD_DIM)
        k_win = lax.dynamic_slice_in_dim(kp, start, span, axis=1)
        v_win = lax.dynamic_slice_in_dim(vp, start, span, axis=1)
        k_pos = start + offs_k
        valid = ((jnp.abs(offs_q[:, None] - offs_k[None, :]) <= WINDOW)
                 & (k_pos >= 0)[None, :] & (k_pos < L)[None, :])
        s_lat = jnp.einsum("bqkgd,bskd->bkgqs", q_blk, k_win, preferred_element_type=jnp.float32) * ATTN_SCALE
        s_lat = jnp.where(valid, s_lat, NEG_INF)
        s_ctx = jnp.einsum("bqkgd,bskd->bkgqs", q_blk, k_ctx, preferred_element_type=jnp.float32) * ATTN_SCALE
        p = sink_softmax(jnp.concatenate([s_lat, s_ctx], axis=-1), sink).astype(v.dtype)
        return (jnp.einsum("bkgqs,bskd->bqkgd", p[..., :span], v_win)
                + jnp.einsum("bkgqs,bskd->bqkgd", p[..., span:], v_ctx))

    out = lax.map(one_block, jnp.arange(nb))
    return jnp.moveaxis(out, 0, 1).reshape(B, L, D_ATTN)


def conv_ffn(h, lp):
    u = dwconv3(h @ lp["w_up"], lp["ffn_conv_w"], lp["ffn_conv_b"])
    return (jax.nn.silu(u[..., :D_FF]) * u[..., D_FF:]) @ lp["w_down"]


def layer_forward(x, mod, lp, attend):
    B, L, _ = x.shape
    shift1, scale1, gate1, shift2, scale2, gate2 = jnp.split(mod[:, None, :], 6, axis=-1)
    h = rmsnorm(x, lp["norm_mix"]) * (1 + scale1) + shift1
    proj = h @ lp["w_in"]
    u3 = dwconv3(proj[..., :O_Q], lp["hy_conv_w"], lp["hy_conv_b"])
    q = proj[..., O_Q:O_K].reshape(B, L, N_HEADS, HEAD_DIM)
    k = proj[..., O_K:O_V].reshape(B, L, N_KV_HEADS, HEAD_DIM)
    v = proj[..., O_V:].reshape(B, L, N_KV_HEADS, HEAD_DIM)
    y_a = hyena_mixer(u3, lp)
    y_b = attend(q, k, v)
    y = jnp.concatenate([rmsnorm(y_a, lp["gn_hyena"]), rmsnorm(y_b, lp["gn_attn"])], axis=-1) @ lp["w_out"]
    x = x + gate1 * y
    h = rmsnorm(x, lp["norm_ffn"]) * (1 + scale2) + shift2
    x = x + gate2 * conv_ffn(h, lp)
    return x, k, v


def setup_inputs(seed: int = 0) -> dict:
    key = jax.random.key(seed)
    ks = iter(jax.random.split(key, 40))
    f32 = jnp.float32

    def nrm(shape, scale):
        return jax.random.normal(next(ks), shape, f32) * scale

    def gain(shape):
        return 1.0 + 0.02 * jax.random.normal(next(ks), shape, f32)

    D = D_MODEL
    return {
        "x_prompt": nrm((BATCH, SEQ, D), 1.0),
        "x_sample": nrm((DEC_BATCH, DEC_SEQ, D), 1.0),
        "cache_k": nrm((DEC_BATCH, DEPTH, PAST_LEN, N_KV_HEADS, HEAD_DIM), 1.0),
        "cache_v": nrm((DEC_BATCH, DEPTH, PAST_LEN, N_KV_HEADS, HEAD_DIM), 1.0),
        "c": nrm((DEC_BATCH, D), 1.0),
        "c_ctx": nrm((D,), 1.0),
        "w_ada": nrm((DEPTH, D, 6 * D), 0.5 * D ** -0.5),
        "b_ada": nrm((DEPTH, 6 * D), 0.02),
        "norm_mix": gain((DEPTH, D)),
        "w_in": nrm((DEPTH, D, D_IN), D ** -0.5),
        "hy_conv_w": nrm((DEPTH, 3 * D_HYENA, SHORT_CONV), SHORT_CONV ** -0.5),
        "hy_conv_b": nrm((DEPTH, 3 * D_HYENA), 0.02),
        "filt_w1": nrm((DEPTH, FILTER_EMB, FILTER_ORDER), FILTER_EMB ** -0.5),
        "filt_b1": nrm((DEPTH, FILTER_ORDER), 0.02),
        "filt_w2": nrm((DEPTH, FILTER_ORDER, FILTER_ORDER), FILTER_ORDER ** -0.5),
        "filt_b2": nrm((DEPTH, FILTER_ORDER), 0.02),
        "filt_w3": nrm((DEPTH, FILTER_ORDER, 2 * D_HYENA), FILTER_ORDER ** -0.5),
        "filt_freq": gain((DEPTH, FILTER_ORDER)),
        "hy_bias": nrm((DEPTH, D_HYENA), 1.0),
        "attn_sink": nrm((DEPTH, N_HEADS), 0.5),
        "gn_hyena": gain((DEPTH, D_HYENA)),
        "gn_attn": gain((DEPTH, D_ATTN)),
        "w_out": nrm((DEPTH, MIX_WIDTH, D), MIX_WIDTH ** -0.5),
        "norm_ffn": gain((DEPTH, D)),
        "w_up": nrm((DEPTH, D, 2 * D_FF), D ** -0.5),
        "ffn_conv_w": nrm((DEPTH, 2 * D_FF, FFN_CONV), FFN_CONV ** -0.5),
        "ffn_conv_b": nrm((DEPTH, 2 * D_FF), 0.02),
        "w_down": nrm((DEPTH, D_FF, D), D_FF ** -0.5),
        "final_norm": gain((D,)),
    }


def reference(x_prompt, x_sample, cache_k, cache_v, c, c_ctx, w_ada, b_ada, norm_mix, w_in,
              hy_conv_w, hy_conv_b, filt_w1, filt_b1, filt_w2, filt_b2, filt_w3, filt_freq, hy_bias,
              attn_sink, gn_hyena, gn_attn, w_out, norm_ffn, w_up, ffn_conv_w, ffn_conv_b, w_down,
              final_norm):
    rows = x_sample.shape[1] // GRID_W
    rr, cc = jnp.meshgrid(jnp.arange(rows), jnp.arange(GRID_W), indexing="ij")
    row_pos = rr.reshape(-1)
    col_pos = cc.reshape(-1)
    xp = x_prompt
    xs = x_sample
    new_k = []
    new_v = []
    for i in range(DEPTH):
        lp = {
            "norm_mix": norm_mix[i], "w_in": w_in[i], "hy_conv_w": hy_conv_w[i], "hy_conv_b": hy_conv_b[i],
            "filt_w1": filt_w1[i], "filt_b1": filt_b1[i], "filt_w2": filt_w2[i], "filt_b2": filt_b2[i],
            "filt_w3": filt_w3[i], "filt_freq": filt_freq[i], "hy_bias": hy_bias[i],
            "gn_hyena": gn_hyena[i], "gn_attn": gn_attn[i], "w_out": w_out[i], "norm_ffn": norm_ffn[i],
            "w_up": w_up[i], "ffn_conv_w": ffn_conv_w[i], "ffn_conv_b": ffn_conv_b[i], "w_down": w_down[i],
        }
        mod_ctx = jax.nn.silu(c_ctx)[None, :] @ w_ada[i] + b_ada[i]
        mod_lat = jax.nn.silu(c) @ w_ada[i] + b_ada[i]
        xp, k_ctx, v_ctx = layer_forward(xp, mod_ctx, lp,
                                         functools.partial(attend_context, sink=attn_sink[i]))
        new_k.append(k_ctx)
        new_v.append(v_ctx)
        xs, _, _ = layer_forward(xs, mod_lat, lp,
                                 functools.partial(attend_latent, k_ctx=cache_k[:, i], v_ctx=cache_v[:, i],
                                                   sink=attn_sink[i], row_pos=row_pos, col_pos=col_pos))
    y_prompt = rmsnorm(xp, final_norm)
    y_sample = rmsnorm(xs, final_norm)
    new_cache_k = jnp.stack(new_k, axis=1)
    new_cache_v = jnp.stack(new_v, axis=1)
    return (y_prompt, y_sample, new_cache_k, new_cache_v)
```

```python
import functools
import math

import jax
import jax.numpy as jnp
from jax import lax
from jax.experimental import pallas as pl
from jax.experimental.pallas import tpu as pltpu

F32 = jnp.float32
BF16 = jnp.bfloat16
HIGHEST = lax.Precision.HIGHEST

D_MODEL = 4096
BATCH = 32
SEQ = 256
DEPTH = 4
DEC_BATCH = 2
DEC_SEQ = 1024
PAST_LEN = 512
GRID_W = 64
D_HYENA = D_MODEL // 2
D_ATTN = D_MODEL - D_HYENA
HEAD_DIM = 128
N_HEADS = D_ATTN // HEAD_DIM
N_KV_HEADS = 4
KV_GROUP = N_HEADS // N_KV_HEADS
KV_DIM = N_KV_HEADS * HEAD_DIM
WINDOW = 128
BLOCK = 128
ATTN_SCALE = 1.0 / math.sqrt(HEAD_DIM)
ROPE_BASE = 10000.0
FILTER_EMB = 33
FILTER_BANDS = (FILTER_EMB - 1) // 2
FILTER_ORDER = 64
DECAY_TARGET = 1e-2
MAX_DECAY = math.log(DECAY_TARGET) / 0.3
MIN_DECAY = math.log(DECAY_TARGET) / 1.5
D_FF = 11008
RMS_EPS = 1e-6
NEG_INF = -1e30
O_Q = 3 * D_HYENA
O_K = O_Q + D_ATTN
O_V = O_K + KV_DIM
D_IN = O_V + KV_DIM

N_CTX = BATCH * SEQ
N_LAT = DEC_BATCH * DEC_SEQ
N_TOK = N_CTX + N_LAT
MOD_ROWS = 8
N_MOD = 6 * D_MODEL

V7X_VMEM_BYTES = 64 * 1024 * 1024
V7X_VMEM_USABLE = 58 * 1024 * 1024
MIB = 1024 * 1024


def _vmem_limit(estimate_bytes):
    return int(min(V7X_VMEM_USABLE, estimate_bytes + estimate_bytes // 4 + 2 * MIB))


def _mod_row(row_start):
    return jnp.where(row_start < N_CTX, 0, 1 + (row_start - N_CTX) // DEC_SEQ)


def _seq_len(row_start):
    return jnp.where(row_start < N_CTX, SEQ, DEC_SEQ)


def _conv3_rows(u, w, b, has_prev, has_next):
    rows = u.shape[0]
    up = jnp.where(has_prev, pltpu.roll(u, 1, 0), 0.0)
    un = jnp.where(has_next, pltpu.roll(u, rows - 1, 0), 0.0)
    return up * w[0:1, :] + u * w[1:2, :] + un * w[2:3, :] + b


MOD_TN = 512


def _mod_kernel(c_ref, w_ref, b_ref, o_ref):
    c = c_ref[...]
    s = (c * jax.nn.sigmoid(c)).astype(BF16)
    o_ref[0] = jnp.dot(s, w_ref[0].astype(BF16), preferred_element_type=F32) + b_ref[0]


def _modulation(cond, w_ada, b_ada):
    est = 2 * D_MODEL * MOD_TN * 4 + D_MODEL * MOD_TN * 2 + 4 * MOD_ROWS * D_MODEL * 4
    return pl.pallas_call(
        _mod_kernel,
        name="adaln_mod",
        grid=(DEPTH, N_MOD // MOD_TN),
        in_specs=[
            pl.BlockSpec((MOD_ROWS, D_MODEL), lambda l, n: (0, 0)),
            pl.BlockSpec((1, D_MODEL, MOD_TN), lambda l, n: (l, 0, n)),
            pl.BlockSpec((1, 1, MOD_TN), lambda l, n: (l, 0, n)),
        ],
        out_specs=pl.BlockSpec((1, MOD_ROWS, MOD_TN), lambda l, n: (l, 0, n)),
        out_shape=jax.ShapeDtypeStruct((DEPTH, MOD_ROWS, N_MOD), F32),
        compiler_params=pltpu.CompilerParams(
            dimension_semantics=("arbitrary", "arbitrary"), vmem_limit_bytes=_vmem_limit(est)),
    )(cond, w_ada, b_ada.reshape(DEPTH, 1, N_MOD))


NORM_TR = 256


def _prenorm_kernel(x_ref, g_ref, shift_ref, scale_ref, o_ref):
    r = _mod_row(pl.program_id(0) * NORM_TR)
    x = x_ref[...]
    y = x * lax.rsqrt(jnp.mean(x * x, axis=-1, keepdims=True) + RMS_EPS) * g_ref[...]
    o_ref[...] = (y * (1.0 + scale_ref[pl.ds(r, 1), :]) + shift_ref[pl.ds(r, 1), :]).astype(o_ref.dtype)


def _prenorm(x, g, mod, shift_chunk, scale_chunk):
    est = 2 * NORM_TR * D_MODEL * (4 + 2) + 3 * NORM_TR * D_MODEL * 4
    return pl.pallas_call(
        _prenorm_kernel,
        name="prenorm",
        grid=(N_TOK // NORM_TR,),
        in_specs=[
            pl.BlockSpec((NORM_TR, D_MODEL), lambda i: (i, 0)),
            pl.BlockSpec((1, D_MODEL), lambda i: (0, 0)),
            pl.BlockSpec((MOD_ROWS, D_MODEL), lambda i: (0, shift_chunk)),
            pl.BlockSpec((MOD_ROWS, D_MODEL), lambda i: (0, scale_chunk)),
        ],
        out_specs=pl.BlockSpec((NORM_TR, D_MODEL), lambda i: (i, 0)),
        out_shape=jax.ShapeDtypeStruct((N_TOK, D_MODEL), BF16),
        compiler_params=pltpu.CompilerParams(
            dimension_semantics=("arbitrary",), vmem_limit_bytes=_vmem_limit(est)),
    )(x, g.reshape(1, D_MODEL), mod, mod)


def _final_norm_kernel(x_ref, g_ref, o_ref):
    x = x_ref[...]
    o_ref[...] = x * lax.rsqrt(jnp.mean(x * x, axis=-1, keepdims=True) + RMS_EPS) * g_ref[...]


def _final_norm(x, g, row_block0, rows):
    est = 4 * NORM_TR * D_MODEL * 4 + 2 * NORM_TR * D_MODEL * 4
    return pl.pallas_call(
        _final_norm_kernel,
        name="final_norm",
        grid=(rows // NORM_TR,),
        in_specs=[
            pl.BlockSpec((NORM_TR, D_MODEL), lambda i: (row_block0 + i, 0)),
            pl.BlockSpec((1, D_MODEL), lambda i: (0, 0)),
        ],
        out_specs=pl.BlockSpec((NORM_TR, D_MODEL), lambda i: (i, 0)),
        out_shape=jax.ShapeDtypeStruct((rows, D_MODEL), F32),
        compiler_params=pltpu.CompilerParams(
            dimension_semantics=("arbitrary",), vmem_limit_bytes=_vmem_limit(est)),
    )(x, g.reshape(1, D_MODEL))


def _matmul_kernel(x_ref, w_ref, o_ref):
    o_ref[...] = jnp.dot(x_ref[...], w_ref[...], preferred_element_type=F32)


def _matmul(x, w, tm, tn):
    m, k = x.shape
    n = w.shape[1]
    est = 2 * (tm * k * 2 + k * tn * 2 + tm * tn * 4) + tm * tn * 4
    return pl.pallas_call(
        _matmul_kernel,
        name="in_proj",
        grid=(n // tn, m // tm),
        in_specs=[
            pl.BlockSpec((tm, k), lambda j, i: (i, 0)),
            pl.BlockSpec((k, tn), lambda j, i: (0, j)),
        ],
        out_specs=pl.BlockSpec((tm, tn), lambda j, i: (i, j)),
        out_shape=jax.ShapeDtypeStruct((m, n), F32),
        compiler_params=pltpu.CompilerParams(
            dimension_semantics=("arbitrary", "arbitrary"), vmem_limit_bytes=_vmem_limit(est)),
    )(x, w)


def _matmul_residual_kernel(*refs, n_lhs, tm):
    lhs = refs[:n_lhs]
    w_ref, res_ref, gate_ref, o_ref = refs[n_lhs:]
    r = _mod_row(pl.program_id(1) * tm)
    y = None
    k0 = 0
    for x_ref in lhs:
        kk = x_ref.shape[1]
        part = jnp.dot(x_ref[...], w_ref[k0:k0 + kk, :], preferred_element_type=F32)
        y = part if y is None else y + part
        k0 += kk
    o_ref[...] = res_ref[...] + gate_ref[pl.ds(r, 1), :] * y


def _matmul_residual(lhs, w, res, mod, gate_chunk, tm, tn, name):
    m = res.shape[0]
    k, n = w.shape
    est = 2 * (tm * k * 2 + k * tn * 2 + 2 * tm * tn * 4 + MOD_ROWS * tn * 4) + 2 * tm * tn * 4
    gate_block0 = gate_chunk * (D_MODEL // tn)
    return pl.pallas_call(
        functools.partial(_matmul_residual_kernel, n_lhs=len(lhs), tm=tm),
        name=name,
        grid=(n // tn, m // tm),
        in_specs=[pl.BlockSpec((tm, x.shape[1]), lambda j, i: (i, 0)) for x in lhs] + [
            pl.BlockSpec((k, tn), lambda j, i: (0, j)),
            pl.BlockSpec((tm, tn), lambda j, i: (i, j)),
            pl.BlockSpec((MOD_ROWS, tn), lambda j, i: (0, gate_block0 + j)),
        ],
        out_specs=pl.BlockSpec((tm, tn), lambda j, i: (i, j)),
        out_shape=jax.ShapeDtypeStruct((m, n), F32),
        compiler_params=pltpu.CompilerParams(
            dimension_semantics=("arbitrary", "arbitrary"), vmem_limit_bytes=_vmem_limit(est)),
    )(*lhs, w, res, mod)


FFN_TM = 1024
FFN_TN = 256


def _ffn_up_kernel(h_ref, wg_ref, wv_ref, cwg_ref, cwv_ref, cbg_ref, cbv_ref, o_ref):
    row0 = pl.program_id(1) * FFN_TM
    seq = _seq_len(row0)
    pos = lax.broadcasted_iota(jnp.int32, (FFN_TM, 1), 0) & (seq - 1)
    has_prev = pos != 0
    has_next = pos != seq - 1
    h = h_ref[...]
    g = _conv3_rows(jnp.dot(h, wg_ref[...], preferred_element_type=F32), cwg_ref[...], cbg_ref[...],
                    has_prev, has_next)
    v = _conv3_rows(jnp.dot(h, wv_ref[...], preferred_element_type=F32), cwv_ref[...], cbv_ref[...],
                    has_prev, has_next)
    o_ref[...] = (g * jax.nn.sigmoid(g) * v).astype(o_ref.dtype)


def _ffn_up(h, w_up, conv_w_t, conv_b):
    nb = D_FF // FFN_TN
    est = (2 * (FFN_TM * D_MODEL * 2 + 2 * D_MODEL * FFN_TN * 2 + FFN_TM * FFN_TN * 2)
           + 8 * FFN_TM * FFN_TN * 4)
    return pl.pallas_call(
        _ffn_up_kernel,
        name="ffn_up",
        grid=(nb, N_TOK // FFN_TM),
        in_specs=[
            pl.BlockSpec((FFN_TM, D_MODEL), lambda j, i: (i, 0)),
            pl.BlockSpec((D_MODEL, FFN_TN), lambda j, i: (0, j)),
            pl.BlockSpec((D_MODEL, FFN_TN), lambda j, i: (0, nb + j)),
            pl.BlockSpec((3, FFN_TN), lambda j, i: (0, j)),
            pl.BlockSpec((3, FFN_TN), lambda j, i: (0, nb + j)),
            pl.BlockSpec((1, FFN_TN), lambda j, i: (0, j)),
            pl.BlockSpec((1, FFN_TN), lambda j, i: (0, nb + j)),
        ],
        out_specs=pl.BlockSpec((FFN_TM, FFN_TN), lambda j, i: (i, j)),
        out_shape=jax.ShapeDtypeStruct((N_TOK, D_FF), BF16),
        compiler_params=pltpu.CompilerParams(
            dimension_semantics=("arbitrary", "arbitrary"), vmem_limit_bytes=_vmem_limit(est)),
    )(h, w_up, w_up, conv_w_t, conv_w_t, conv_b, conv_b)


FILT_PAD = 64


def _split_bf16(a):
    hi = a.astype(BF16)
    return hi, (a - hi.astype(F32)).astype(BF16)


def _dot_split(t_hi, t_lo, b):
    b_hi, b_lo = _split_bf16(b)
    return (jnp.dot(t_hi, b_hi, preferred_element_type=F32)
            + jnp.dot(t_hi, b_lo, preferred_element_type=F32)
            + jnp.dot(t_lo, b_hi, preferred_element_type=F32))


def _dft_tables(L):
    f = jnp.arange(L, dtype=jnp.int32)[:, None]
    s = jnp.arange(L, dtype=jnp.int32)[None, :]
    ang = ((f * s) % (2 * L)).astype(F32) * (math.pi / L)
    cm = jnp.cos(ang)
    nyq = jnp.where(s % 2 == 0, 1.0, -1.0).astype(F32)
    sm = jnp.where(f == 0, nyq, -jnp.sin(ang))
    return tuple(jnp.stack(_split_bf16(t)) for t in (cm, sm, sm.T))


def _filter_features(L):
    t = jnp.linspace(0.0, 1.0, L, dtype=F32)[:, None]
    w = 2.0 * math.pi * jnp.arange(L, dtype=F32)[:, None] / L
    f = jnp.linspace(1e-4, FILTER_BANDS - 1, FILTER_BANDS, dtype=F32)[None, :]
    z = jnp.concatenate([t, jnp.cos(f * w), -jnp.sin(f * w)], axis=-1)
    return jnp.pad(z, ((0, 0), (0, FILT_PAD - FILTER_EMB)))


def _filter_kernel(z_ref, w1_ref, b1_ref, w2_ref, b2_ref, w3f_ref, w3b_ref, freq_ref, delta_ref,
                   cm_ref, sm_ref, ka_ref, kb_ref, *, L):
    fr = freq_ref[...]
    z = z_ref[...]
    h = jnp.sin(fr * (jnp.dot(z, w1_ref[...], precision=HIGHEST, preferred_element_type=F32) + b1_ref[...]))
    h = jnp.sin(fr * (jnp.dot(h, w2_ref[...], precision=HIGHEST, preferred_element_type=F32) + b2_ref[...]))
    decay = jnp.exp(-z[:, 0:1] * jnp.abs(delta_ref[...]))
    h_fwd = jnp.dot(h, w3f_ref[...], precision=HIGHEST, preferred_element_type=F32) * decay
    h_bwd = jnp.dot(h, w3b_ref[...], precision=HIGHEST, preferred_element_type=F32) * decay
    pos = lax.broadcasted_iota(jnp.int32, (L, 1), 0)
    first = pos == 0
    h_bwd = jnp.where(first, 0.0, h_bwd)
    even = h_fwd + h_bwd
    odd = h_fwd - h_bwd
    sign = jnp.where((pos & 1) == 0, 1.0, -1.0)
    nyq = jnp.sum(sign * even, axis=0, keepdims=True)
    ka_ref[...] = _dot_split(cm_ref[0], cm_ref[1], even)
    kb = _dot_split(sm_ref[0], sm_ref[1], odd)
    kb_ref[...] = jnp.where(first, nyq, kb)


FILT_TC = 256


def _filter_spectrum(L, z, w1p, b1, w2, b2, w3, freq, deltas, cm, sm):
    nb = D_HYENA // FILT_TC
    est = 2 * (2 * 2 * L * L * 2) + 2 * 4 * L * FILT_TC * 4 + 10 * L * FILT_TC * 4 + 4 * L * FILT_PAD * 4
    full = lambda shape: pl.BlockSpec(shape, lambda c: (0,) * len(shape))
    return pl.pallas_call(
        functools.partial(_filter_kernel, L=L),
        name=f"hyena_filter_{L}",
        grid=(nb,),
        in_specs=[
            full((L, FILT_PAD)), full((FILT_PAD, FILTER_ORDER)), full((1, FILTER_ORDER)),
            full((FILTER_ORDER, FILTER_ORDER)), full((1, FILTER_ORDER)),
            pl.BlockSpec((FILTER_ORDER, FILT_TC), lambda c: (0, c)),
            pl.BlockSpec((FILTER_ORDER, FILT_TC), lambda c: (0, nb + c)),
            full((1, FILTER_ORDER)),
            pl.BlockSpec((1, FILT_TC), lambda c: (0, c)),
            full((2, L, L)), full((2, L, L)),
        ],
        out_specs=[pl.BlockSpec((L, FILT_TC), lambda c: (0, c)),
                   pl.BlockSpec((L, FILT_TC), lambda c: (0, c))],
        out_shape=[jax.ShapeDtypeStruct((L, D_HYENA), F32)] * 2,
        compiler_params=pltpu.CompilerParams(
            dimension_semantics=("arbitrary",), vmem_limit_bytes=_vmem_limit(est)),
    )(z, w1p, b1, w2, b2, w3, w3, freq, deltas, cm, sm)


def _hyena_kernel(x0_ref, x1_ref, v_ref, cw0_ref, cw1_ref, cw2_ref, cb0_ref, cb1_ref, cb2_ref, hb_ref,
                  ka_ref, kb_ref, cm_ref, sm_ref, smt_ref, gn_ref, o_ref, acc_ref, *, L, tc, nct):
    ct = pl.program_id(1)
    pos = lax.broadcasted_iota(jnp.int32, (L, 1), 0)
    first = pos == 0
    has_prev = pos != 0
    has_next = pos != L - 1
    x0 = _conv3_rows(x0_ref[...], cw0_ref[...], cb0_ref[...], has_prev, has_next)
    x1 = _conv3_rows(x1_ref[...], cw1_ref[...], cb1_ref[...], has_prev, has_next)
    v = _conv3_rows(v_ref[...], cw2_ref[...], cb2_ref[...], has_prev, has_next)
    u = x1 * v
    top = _dot_split(cm_ref[0], cm_ref[1], u)
    bot = _dot_split(sm_ref[0], sm_ref[1], u)
    ka = ka_ref[...]
    kb = kb_ref[...]
    y_top = top * ka - jnp.where(first, 0.0, bot * kb)
    y_bot = jnp.where(first, bot * kb, top * kb + bot * ka)
    wgt = jnp.where(first, 1.0 / (2 * L), 2.0 / (2 * L))
    y = _dot_split(cm_ref[0], cm_ref[1], y_top * wgt) + _dot_split(smt_ref[0], smt_ref[1], y_bot * wgt)
    acc_ref[ct] = x0 * (y + u * hb_ref[...])

    @pl.when(ct == nct - 1)
    def _():
        ss = jnp.zeros((L, 1), F32)
        for j in range(nct):
            a = acc_ref[j]
            ss = ss + jnp.sum(a * a, axis=-1, keepdims=True)
        inv = lax.rsqrt(ss * (1.0 / D_HYENA) + RMS_EPS)
        for j in range(nct):
            o_ref[:, j * tc:(j + 1) * tc] = (acc_ref[j] * inv * gn_ref[:, j * tc:(j + 1) * tc]).astype(o_ref.dtype)


def _hyena(proj, conv_w_t, conv_b, hy_bias, ka, kb, cm, sm, smt, gn, *, L, tc, nseq, row_block0):
    nct = D_HYENA // tc
    cpb = D_HYENA // tc
    est = (3 * 2 * L * L * 2 + 2 * (3 * L * tc * 4 + 2 * L * tc * 4) + nct * L * tc * 4
           + 2 * L * D_HYENA * 2 + 14 * L * tc * 4)
    once = pl.Buffered(1)
    grp = lambda g: pl.BlockSpec((L, tc), lambda s, c: (row_block0 + s, g * cpb + c))
    tap = lambda g: pl.BlockSpec((3, tc), lambda s, c: (0, g * cpb + c))
    bias = lambda g: pl.BlockSpec((1, tc), lambda s, c: (0, g * cpb + c))
    col = lambda rows: pl.BlockSpec((rows, tc), lambda s, c: (0, c))
    table = pl.BlockSpec((2, L, L), lambda s, c: (0, 0, 0), pipeline_mode=once)
    return pl.pallas_call(
        functools.partial(_hyena_kernel, L=L, tc=tc, nct=nct),
        name=f"hyena_conv_{L}",
        grid=(nseq, nct),
        in_specs=[grp(0), grp(1), grp(2), tap(0), tap(1), tap(2), bias(0), bias(1), bias(2),
                  col(1), col(L), col(L), table, table, table,
                  pl.BlockSpec((1, D_HYENA), lambda s, c: (0, 0))],
        out_specs=pl.BlockSpec((L, D_HYENA), lambda s, c: (s, 0)),
        out_shape=jax.ShapeDtypeStruct((nseq * L, D_HYENA), BF16),
        scratch_shapes=[pltpu.VMEM((nct, L, tc), F32)],
        compiler_params=pltpu.CompilerParams(
            dimension_semantics=("arbitrary", "arbitrary"), vmem_limit_bytes=_vmem_limit(est)),
    )(proj, proj, proj, conv_w_t, conv_w_t, conv_w_t, conv_b, conv_b, conv_b,
      hy_bias, ka, kb, cm, sm, smt, gn)


def _dot_nt(a, b):
    return lax.dot_general(a, b, (((1,), (1,)), ((), ())), preferred_element_type=F32)


def _head_norm_store(acc_ref, gn_ref, o_ref):
    y = acc_ref[...]
    inv = lax.rsqrt(jnp.mean(y * y, axis=-1, keepdims=True) + RMS_EPS)
    o_ref[...] = (y * inv * gn_ref[...]).astype(o_ref.dtype)


def _attn_ctx_kernel(sink_ref, q_ref, k_ref, v_ref, gn_ref, o_ref, acc_ref):
    for kvh in range(N_KV_HEADS):
        ksl = slice(kvh * HEAD_DIM, (kvh + 1) * HEAD_DIM)
        kh = k_ref[:, ksl].astype(BF16)
        vh = v_ref[:, ksl].astype(BF16)
        for g in range(KV_GROUP):
            hd = kvh * KV_GROUP + g
            hsl = slice(hd * HEAD_DIM, (hd + 1) * HEAD_DIM)
            s = _dot_nt(q_ref[:, hsl].astype(BF16), kh) * ATTN_SCALE
            sk = sink_ref[hd]
            m = jnp.maximum(jnp.max(s, axis=-1, keepdims=True), sk)
            e = jnp.exp(s - m)
            den = jnp.sum(e, axis=-1, keepdims=True) + jnp.exp(sk - m)
            acc_ref[:, hsl] = jnp.dot((e / den).astype(BF16), vh, preferred_element_type=F32)
    _head_norm_store(acc_ref, gn_ref, o_ref)


def _attn_ctx(proj, sink, gn):
    est = 2 * (SEQ * D_ATTN * 4 + 2 * SEQ * KV_DIM * 4 + SEQ * D_ATTN * 2) + 3 * SEQ * D_ATTN * 4
    return pl.pallas_call(
        _attn_ctx_kernel,
        name="attn_ctx",
        grid=(BATCH,),
        in_specs=[
            pl.BlockSpec(memory_space=pltpu.SMEM),
            pl.BlockSpec((SEQ, D_ATTN), lambda b: (b, O_Q // D_ATTN)),
            pl.BlockSpec((SEQ, KV_DIM), lambda b: (b, O_K // KV_DIM)),
            pl.BlockSpec((SEQ, KV_DIM), lambda b: (b, O_V // KV_DIM)),
            pl.BlockSpec((1, D_ATTN), lambda b: (0, 0)),
        ],
        out_specs=pl.BlockSpec((SEQ, D_ATTN), lambda b: (b, 0)),
        out_shape=jax.ShapeDtypeStruct((N_CTX, D_ATTN), BF16),
        scratch_shapes=[pltpu.VMEM((SEQ, D_ATTN), F32)],
        compiler_params=pltpu.CompilerParams(
            dimension_semantics=("arbitrary",), vmem_limit_bytes=_vmem_limit(est)),
    )(sink, proj, proj, proj, gn)


def _rope_tables():
    half = HEAD_DIM // 2
    quarter = half // 2
    pos = jnp.arange(DEC_SEQ, dtype=jnp.int32)
    row_pos = (pos // GRID_W).astype(F32)
    col_pos = (pos % GRID_W).astype(F32)
    inv = ROPE_BASE ** (-jnp.arange(quarter, dtype=F32) / quarter)
    lane = jnp.arange(HEAD_DIM, dtype=jnp.int32)
    p = jnp.where((lane < half)[None, :], row_pos[:, None], col_pos[:, None])
    ang = p * inv[lane % quarter][None, :]
    low = ((lane % half) < quarter)[None, :]
    cos = jnp.cos(ang)
    sin = jnp.sin(ang)
    return cos, jnp.where(low, -sin, 0.0), jnp.where(low, 0.0, sin)


ROPE_SHIFT = HEAD_DIM // 4
LAT_SPAN = BLOCK + 2 * WINDOW


def _rope(x, cos, sin_lo, sin_hi):
    return (x * cos + pltpu.roll(x, HEAD_DIM - ROPE_SHIFT, 1) * sin_lo
            + pltpu.roll(x, ROPE_SHIFT, 1) * sin_hi)


def _attn_lat_kernel(sink_ref, q_ref, k_ref, v_ref, ck_ref, cv_ref, cos_ref, slo_ref, shi_ref, gn_ref,
                     o_ref, acc_ref):
    i = pl.program_id(1)
    q0 = pl.multiple_of(i * BLOCK, BLOCK)
    k0 = pl.multiple_of(jnp.clip((i - 1) * BLOCK, 0, DEC_SEQ - LAT_SPAN), BLOCK)
    q_tab = (cos_ref[pl.ds(q0, BLOCK), :], slo_ref[pl.ds(q0, BLOCK), :], shi_ref[pl.ds(q0, BLOCK), :])
    k_tab = (cos_ref[pl.ds(k0, LAT_SPAN), :], slo_ref[pl.ds(k0, LAT_SPAN), :], shi_ref[pl.ds(k0, LAT_SPAN), :])
    q_pos = q0 + lax.broadcasted_iota(jnp.int32, (BLOCK, 1), 0)
    k_pos = k0 + lax.broadcasted_iota(jnp.int32, (1, LAT_SPAN), 1)
    valid = jnp.abs(q_pos - k_pos) <= WINDOW
    for kvh in range(N_KV_HEADS):
        ksl = slice(kvh * HEAD_DIM, (kvh + 1) * HEAD_DIM)
        kw = _rope(k_ref[pl.ds(k0, LAT_SPAN), ksl], *k_tab).astype(BF16)
        vw = v_ref[pl.ds(k0, LAT_SPAN), ksl].astype(BF16)
        kc = ck_ref[0, :, ksl].astype(BF16)
        vc = cv_ref[0, :, ksl].astype(BF16)
        for g in range(KV_GROUP):
            hd = kvh * KV_GROUP + g
            hsl = slice(hd * HEAD_DIM, (hd + 1) * HEAD_DIM)
            qh = _rope(q_ref[:, hsl], *q_tab).astype(BF16)
            s_lat = jnp.where(valid, _dot_nt(qh, kw) * ATTN_SCALE, NEG_INF)
            s_ctx = _dot_nt(qh, kc) * ATTN_SCALE
            sk = sink_ref[hd]
            m = jnp.maximum(jnp.maximum(jnp.max(s_lat, axis=-1, keepdims=True),
                                        jnp.max(s_ctx, axis=-1, keepdims=True)), sk)
            e_lat = jnp.exp(s_lat - m)
            e_ctx = jnp.exp(s_ctx - m)
            den = (jnp.sum(e_lat, axis=-1, keepdims=True) + jnp.sum(e_ctx, axis=-1, keepdims=True)
                   + jnp.exp(sk - m))
            acc_ref[:, hsl] = (jnp.dot((e_lat / den).astype(BF16), vw, preferred_element_type=F32)
                               + jnp.dot((e_ctx / den).astype(BF16), vc, preferred_element_type=F32))
    _head_norm_store(acc_ref, gn_ref, o_ref)


def _attn_lat(proj, cache_k, cache_v, sink, rope, gn):
    nqb = DEC_SEQ // BLOCK
    est = (2 * (BLOCK * D_ATTN * 4 + 2 * DEC_SEQ * KV_DIM * 4 + 2 * PAST_LEN * KV_DIM * 4
                + 3 * DEC_SEQ * HEAD_DIM * 4 + BLOCK * D_ATTN * 2) + 8 * BLOCK * D_ATTN * 4)
    lat_row_block0 = N_CTX // BLOCK
    lat_seq_block0 = N_CTX // DEC_SEQ
    tab = pl.BlockSpec((DEC_SEQ, HEAD_DIM), lambda b, i: (0, 0))
    return pl.pallas_call(
        _attn_lat_kernel,
        name="attn_lat",
        grid=(DEC_BATCH, nqb),
        in_specs=[
            pl.BlockSpec(memory_space=pltpu.SMEM),
            pl.BlockSpec((BLOCK, D_ATTN), lambda b, i: (lat_row_block0 + b * nqb + i, O_Q // D_ATTN)),
            pl.BlockSpec((DEC_SEQ, KV_DIM), lambda b, i: (lat_seq_block0 + b, O_K // KV_DIM)),
            pl.BlockSpec((DEC_SEQ, KV_DIM), lambda b, i: (lat_seq_block0 + b, O_V // KV_DIM)),
            pl.BlockSpec((1, PAST_LEN, KV_DIM), lambda b, i: (b, 0, 0)),
            pl.BlockSpec((1, PAST_LEN, KV_DIM), lambda b, i: (b, 0, 0)),
            tab, tab, tab,
            pl.BlockSpec((1, D_ATTN), lambda b, i: (0, 0)),
        ],
        out_specs=pl.BlockSpec((BLOCK, D_ATTN), lambda b, i: (b * nqb + i, 0)),
        out_shape=jax.ShapeDtypeStruct((N_LAT, D_ATTN), BF16),
        scratch_shapes=[pltpu.VMEM((BLOCK, D_ATTN), F32)],
        compiler_params=pltpu.CompilerParams(
            dimension_semantics=("arbitrary", "arbitrary"), vmem_limit_bytes=_vmem_limit(est)),
    )(sink, proj, proj, proj, cache_k, cache_v, *rope, gn)


def kernel(x_prompt, x_sample, cache_k, cache_v, c, c_ctx, w_ada, b_ada, norm_mix, w_in, hy_conv_w, hy_conv_b, filt_w1, filt_b1, filt_w2, filt_b2, filt_w3, filt_freq, hy_bias, attn_sink, gn_hyena, gn_attn, w_out, norm_ffn, w_up, ffn_conv_w, ffn_conv_b, w_down, final_norm):
    x = jnp.concatenate([x_prompt.reshape(N_CTX, D_MODEL), x_sample.reshape(N_LAT, D_MODEL)], axis=0)
    cond = jnp.concatenate([c_ctx[None, :], c, jnp.zeros((MOD_ROWS - 1 - DEC_BATCH, D_MODEL), F32)], axis=0)
    mod_all = _modulation(cond, w_ada, b_ada)

    rope = _rope_tables()
    tables = {L: _dft_tables(L) for L in (SEQ, DEC_SEQ)}
    feats = {L: _filter_features(L) for L in (SEQ, DEC_SEQ)}
    deltas = jnp.linspace(MIN_DECAY, MAX_DECAY, D_HYENA, dtype=F32)[None, :]

    new_k = []
    new_v = []
    for i in range(DEPTH):
        mod = mod_all[i]
        h = _prenorm(x, norm_mix[i], mod, 0, 1)
        proj = _matmul(h, w_in[i].astype(BF16), 1024, 1024)

        hy_w_t = hy_conv_w[i].T
        hy_b = hy_conv_b[i][None, :]
        w1p = jnp.pad(filt_w1[i], ((0, FILT_PAD - FILTER_EMB), (0, 0)))
        ya = []
        for L, tc, nseq, rb0 in ((SEQ, 1024, BATCH, 0), (DEC_SEQ, 256, DEC_BATCH, N_CTX // DEC_SEQ)):
            cm, sm, smt = tables[L]
            ka, kb = _filter_spectrum(L, feats[L], w1p, filt_b1[i][None, :], filt_w2[i], filt_b2[i][None, :],
                                      filt_w3[i], filt_freq[i][None, :], deltas, cm, sm)
            ya.append(_hyena(proj, hy_w_t, hy_b, hy_bias[i][None, :], ka, kb, cm, sm, smt,
                             gn_hyena[i][None, :], L=L, tc=tc, nseq=nseq, row_block0=rb0))
        gn_a = gn_attn[i][None, :]
        yb_ctx = _attn_ctx(proj, attn_sink[i], gn_a)
        yb_lat = _attn_lat(proj, cache_k[:, i].reshape(DEC_BATCH, PAST_LEN, KV_DIM),
                           cache_v[:, i].reshape(DEC_BATCH, PAST_LEN, KV_DIM), attn_sink[i], rope, gn_a)
        y_a = jnp.concatenate(ya, axis=0)
        y_b = jnp.concatenate([yb_ctx, yb_lat], axis=0)
        x = _matmul_residual([y_a, y_b], w_out[i].astype(BF16), x, mod, 2, 1024, 512, "out_proj")

        h = _prenorm(x, norm_ffn[i], mod, 3, 4)
        act = _ffn_up(h, w_up[i].astype(BF16), ffn_conv_w[i].T, ffn_conv_b[i][None, :])
        x = _matmul_residual([act], w_down[i].astype(BF16), x, mod, 5, 512, 512, "ffn_down")

        new_k.append(proj[:N_CTX, O_K:O_V].reshape(BATCH, SEQ, N_KV_HEADS, HEAD_DIM))
        new_v.append(proj[:N_CTX, O_V:].reshape(BATCH, SEQ, N_KV_HEADS, HEAD_DIM))

    y_prompt = _final_norm(x, final_norm, 0, N_CTX).reshape(BATCH, SEQ, D_MODEL)
    y_sample = _final_norm(x, final_norm, N_CTX // NORM_TR, N_LAT).reshape(DEC_BATCH, DEC_SEQ, D_MODEL)
    return (y_prompt, y_sample, jnp.stack(new_k, axis=1), jnp.stack(new_v, axis=1))
```

```python
import functools
import math

import jax
import jax.numpy as jnp
from jax import lax
from jax.experimental import pallas as pl
from jax.experimental.pallas import tpu as pltpu

F32 = jnp.float32
BF16 = jnp.bfloat16
HIGHEST = lax.Precision.HIGHEST

D_MODEL = 4096
BATCH = 32
SEQ = 256
DEPTH = 4
DEC_BATCH = 2
DEC_SEQ = 1024
PAST_LEN = 512
GRID_W = 64
D_HYENA = D_MODEL // 2
D_ATTN = D_MODEL - D_HYENA
HEAD_DIM = 128
N_HEADS = D_ATTN // HEAD_DIM
N_KV_HEADS = 4
KV_GROUP = N_HEADS // N_KV_HEADS
KV_DIM = N_KV_HEADS * HEAD_DIM
WINDOW = 128
BLOCK = 128
ATTN_SCALE = 1.0 / math.sqrt(HEAD_DIM)
ROPE_BASE = 10000.0
FILTER_EMB = 33
FILTER_BANDS = (FILTER_EMB - 1) // 2
FILTER_ORDER = 64
DECAY_TARGET = 1e-2
MAX_DECAY = math.log(DECAY_TARGET) / 0.3
MIN_DECAY = math.log(DECAY_TARGET) / 1.5
D_FF = 11008
RMS_EPS = 1e-6
NEG_INF = -1e30
O_Q = 3 * D_HYENA
O_K = O_Q + D_ATTN
O_V = O_K + KV_DIM
D_IN = O_V + KV_DIM

N_CTX = BATCH * SEQ
N_LAT = DEC_BATCH * DEC_SEQ
N_TOK = N_CTX + N_LAT
MOD_ROWS = 8
N_MOD = 6 * D_MODEL

V7X_VMEM_BYTES = 64 * 1024 * 1024
V7X_VMEM_USABLE = 58 * 1024 * 1024
MIB = 1024 * 1024


def _vmem_limit(estimate_bytes):
    return int(min(V7X_VMEM_USABLE, estimate_bytes + estimate_bytes // 4 + 2 * MIB))


def _mod_row(row_start):
    return jnp.where(row_start < N_CTX, 0, 1 + (row_start - N_CTX) // DEC_SEQ)


def _seq_len(row_start):
    return jnp.where(row_start < N_CTX, SEQ, DEC_SEQ)


def _conv3_rows(u, w, b, has_prev, has_next):
    rows = u.shape[0]
    up = jnp.where(has_prev, pltpu.roll(u, 1, 0), 0.0)
    un = jnp.where(has_next, pltpu.roll(u, rows - 1, 0), 0.0)
    return up * w[0:1, :] + u * w[1:2, :] + un * w[2:3, :] + b


MOD_TN = 512


def _mod_kernel(c_ref, w_ref, b_ref, o_ref):
    c = c_ref[...]
    s = (c * jax.nn.sigmoid(c)).astype(BF16)
    o_ref[0] = jnp.dot(s, w_ref[0].astype(BF16), preferred_element_type=F32) + b_ref[0]


def _modulation(cond, w_ada, b_ada):
    est = 2 * D_MODEL * MOD_TN * 4 + D_MODEL * MOD_TN * 2 + 4 * MOD_ROWS * D_MODEL * 4
    return pl.pallas_call(
        _mod_kernel,
        name="adaln_mod",
        grid=(DEPTH, N_MOD // MOD_TN),
        in_specs=[
            pl.BlockSpec((MOD_ROWS, D_MODEL), lambda l, n: (0, 0)),
            pl.BlockSpec((1, D_MODEL, MOD_TN), lambda l, n: (l, 0, n)),
            pl.BlockSpec((1, 1, MOD_TN), lambda l, n: (l, 0, n)),
        ],
        out_specs=pl.BlockSpec((1, MOD_ROWS, MOD_TN), lambda l, n: (l, 0, n)),
        out_shape=jax.ShapeDtypeStruct((DEPTH, MOD_ROWS, N_MOD), F32),
        compiler_params=pltpu.CompilerParams(
            dimension_semantics=("arbitrary", "arbitrary"), vmem_limit_bytes=_vmem_limit(est)),
    )(cond, w_ada, b_ada.reshape(DEPTH, 1, N_MOD))


NORM_TR = 256


def _prenorm_kernel(x_ref, g_ref, shift_ref, scale_ref, o_ref):
    r = _mod_row(pl.program_id(0) * NORM_TR)
    x = x_ref[...]
    y = x * lax.rsqrt(jnp.mean(x * x, axis=-1, keepdims=True) + RMS_EPS) * g_ref[...]
    o_ref[...] = (y * (1.0 + scale_ref[pl.ds(r, 1), :]) + shift_ref[pl.ds(r, 1), :]).astype(o_ref.dtype)


def _prenorm(x, g, mod, shift_chunk, scale_chunk):
    est = 2 * NORM_TR * D_MODEL * (4 + 2) + 3 * NORM_TR * D_MODEL * 4
    return pl.pallas_call(
        _prenorm_kernel,
        name="prenorm",
        grid=(N_TOK // NORM_TR,),
        in_specs=[
            pl.BlockSpec((NORM_TR, D_MODEL), lambda i: (i, 0)),
            pl.BlockSpec((1, D_MODEL), lambda i: (0, 0)),
            pl.BlockSpec((MOD_ROWS, D_MODEL), lambda i: (0, shift_chunk)),
            pl.BlockSpec((MOD_ROWS, D_MODEL), lambda i: (0, scale_chunk)),
        ],
        out_specs=pl.BlockSpec((NORM_TR, D_MODEL), lambda i: (i, 0)),
        out_shape=jax.ShapeDtypeStruct((N_TOK, D_MODEL), BF16),
        compiler_params=pltpu.CompilerParams(
            dimension_semantics=("arbitrary",), vmem_limit_bytes=_vmem_limit(est)),
    )(x, g.reshape(1, D_MODEL), mod, mod)


def _final_norm_kernel(x_ref, g_ref, o_ref):
    x = x_ref[...]
    o_ref[...] = x * lax.rsqrt(jnp.mean(x * x, axis=-1, keepdims=True) + RMS_EPS) * g_ref[...]


def _final_norm(x, g, row_block0, rows):
    est = 4 * NORM_TR * D_MODEL * 4 + 2 * NORM_TR * D_MODEL * 4
    return pl.pallas_call(
        _final_norm_kernel,
        name="final_norm",
        grid=(rows // NORM_TR,),
        in_specs=[
            pl.BlockSpec((NORM_TR, D_MODEL), lambda i: (row_block0 + i, 0)),
            pl.BlockSpec((1, D_MODEL), lambda i: (0, 0)),
        ],
        out_specs=pl.BlockSpec((NORM_TR, D_MODEL), lambda i: (i, 0)),
        out_shape=jax.ShapeDtypeStruct((rows, D_MODEL), F32),
        compiler_params=pltpu.CompilerParams(
            dimension_semantics=("arbitrary",), vmem_limit_bytes=_vmem_limit(est)),
    )(x, g.reshape(1, D_MODEL))


def _weight_tile(w_ref, wb_ref):
    if wb_ref is None:
        return w_ref

    @pl.when(pl.program_id(1) == 0)
    def _():
        wb_ref[...] = w_ref[...].astype(BF16)
    return wb_ref


def _matmul_kernel(x_ref, w_ref, o_ref, wb_ref):
    w = _weight_tile(w_ref, wb_ref)
    o_ref[...] = jnp.dot(x_ref[...], w[...], preferred_element_type=F32)


def _matmul(x, w, tm, tn):
    m, k = x.shape
    n = w.shape[1]
    est = 2 * (tm * k * 2 + k * tn * 4 + tm * tn * 4) + k * tn * 2 + tm * tn * 4
    return pl.pallas_call(
        _matmul_kernel,
        name="in_proj",
        grid=(n // tn, m // tm),
        in_specs=[
            pl.BlockSpec((tm, k), lambda j, i: (i, 0)),
            pl.BlockSpec((k, tn), lambda j, i: (0, j)),
        ],
        out_specs=pl.BlockSpec((tm, tn), lambda j, i: (i, j)),
        out_shape=jax.ShapeDtypeStruct((m, n), F32),
        scratch_shapes=[pltpu.VMEM((k, tn), BF16)],
        compiler_params=pltpu.CompilerParams(
            dimension_semantics=("arbitrary", "arbitrary"), vmem_limit_bytes=_vmem_limit(est)),
    )(x, w)


def _matmul_residual_kernel(*refs, n_lhs, tm, cast_weight):
    lhs = refs[:n_lhs]
    w_ref, res_ref, gate_ref, o_ref = refs[n_lhs:n_lhs + 4]
    w = _weight_tile(w_ref, refs[n_lhs + 4] if cast_weight else None)
    r = _mod_row(pl.program_id(1) * tm)
    y = None
    k0 = 0
    for x_ref in lhs:
        kk = x_ref.shape[1]
        part = jnp.dot(x_ref[...], w[k0:k0 + kk, :], preferred_element_type=F32)
        y = part if y is None else y + part
        k0 += kk
    o_ref[...] = res_ref[...] + gate_ref[pl.ds(r, 1), :] * y


def _matmul_residual(lhs, w, res, mod, gate_chunk, tm, tn, name):
    m = res.shape[0]
    k, n = w.shape
    cast_weight = w.dtype != BF16
    w_bytes = 2 * k * tn * 4 + k * tn * 2 if cast_weight else 2 * k * tn * 2
    est = 2 * (tm * k * 2 + 2 * tm * tn * 4 + MOD_ROWS * tn * 4) + w_bytes + 2 * tm * tn * 4
    gate_block0 = gate_chunk * (D_MODEL // tn)
    return pl.pallas_call(
        functools.partial(_matmul_residual_kernel, n_lhs=len(lhs), tm=tm, cast_weight=cast_weight),
        name=name,
        grid=(n // tn, m // tm),
        in_specs=[pl.BlockSpec((tm, x.shape[1]), lambda j, i: (i, 0)) for x in lhs] + [
            pl.BlockSpec((k, tn), lambda j, i: (0, j)),
            pl.BlockSpec((tm, tn), lambda j, i: (i, j)),
            pl.BlockSpec((MOD_ROWS, tn), lambda j, i: (0, gate_block0 + j)),
        ],
        out_specs=pl.BlockSpec((tm, tn), lambda j, i: (i, j)),
        out_shape=jax.ShapeDtypeStruct((m, n), F32),
        scratch_shapes=[pltpu.VMEM((k, tn), BF16)] if cast_weight else [],
        compiler_params=pltpu.CompilerParams(
            dimension_semantics=("arbitrary", "arbitrary"), vmem_limit_bytes=_vmem_limit(est)),
    )(*lhs, w, res, mod)


FFN_TM = 1024
FFN_TN = 256


def _ffn_up_kernel(h_ref, wg_ref, wv_ref, cwg_ref, cwv_ref, cbg_ref, cbv_ref, o_ref, wb_ref):
    @pl.when(pl.program_id(1) == 0)
    def _():
        wb_ref[:, :FFN_TN] = wg_ref[...].astype(BF16)
        wb_ref[:, FFN_TN:] = wv_ref[...].astype(BF16)

    u = jnp.dot(h_ref[...], wb_ref[...], preferred_element_type=F32)
    row0 = pl.program_id(1) * FFN_TM
    seq = _seq_len(row0)
    pos = lax.broadcasted_iota(jnp.int32, (FFN_TM, 1), 0) & (seq - 1)
    has_prev = pos != 0
    has_next = pos != seq - 1
    g = _conv3_rows(u[:, :FFN_TN], cwg_ref[...], cbg_ref[...], has_prev, has_next)
    v = _conv3_rows(u[:, FFN_TN:], cwv_ref[...], cbv_ref[...], has_prev, has_next)
    o_ref[...] = (g * jax.nn.sigmoid(g) * v).astype(o_ref.dtype)


def _ffn_up(h, w_up, conv_w_t, conv_b):
    nb = D_FF // FFN_TN
    est = (2 * (FFN_TM * D_MODEL * 2 + 2 * D_MODEL * FFN_TN * 4 + FFN_TM * FFN_TN * 2)
           + 2 * D_MODEL * FFN_TN * 2 + 10 * FFN_TM * FFN_TN * 4)
    return pl.pallas_call(
        _ffn_up_kernel,
        name="ffn_up",
        grid=(nb, N_TOK // FFN_TM),
        in_specs=[
            pl.BlockSpec((FFN_TM, D_MODEL), lambda j, i: (i, 0)),
            pl.BlockSpec((D_MODEL, FFN_TN), lambda j, i: (0, j)),
            pl.BlockSpec((D_MODEL, FFN_TN), lambda j, i: (0, nb + j)),
            pl.BlockSpec((3, FFN_TN), lambda j, i: (0, j)),
            pl.BlockSpec((3, FFN_TN), lambda j, i: (0, nb + j)),
            pl.BlockSpec((1, FFN_TN), lambda j, i: (0, j)),
            pl.BlockSpec((1, FFN_TN), lambda j, i: (0, nb + j)),
        ],
        out_specs=pl.BlockSpec((FFN_TM, FFN_TN), lambda j, i: (i, j)),
        out_shape=jax.ShapeDtypeStruct((N_TOK, D_FF), BF16),
        scratch_shapes=[pltpu.VMEM((D_MODEL, 2 * FFN_TN), BF16)],
        compiler_params=pltpu.CompilerParams(
            dimension_semantics=("arbitrary", "arbitrary"), vmem_limit_bytes=_vmem_limit(est)),
    )(h, w_up, w_up, conv_w_t, conv_w_t, conv_b, conv_b)


FILT_PAD = 64


def _split_bf16(a):
    hi = a.astype(BF16)
    return hi, (a - hi.astype(F32)).astype(BF16)


def _dot_split(t_hi, t_lo, b):
    b_hi, b_lo = _split_bf16(b)
    return (jnp.dot(t_hi, b_hi, preferred_element_type=F32)
            + jnp.dot(t_hi, b_lo, preferred_element_type=F32)
            + jnp.dot(t_lo, b_hi, preferred_element_type=F32))


def _dft_tables(L):
    f = jnp.arange(L, dtype=jnp.int32)[:, None]
    s = jnp.arange(L, dtype=jnp.int32)[None, :]
    ang = ((f * s) % (2 * L)).astype(F32) * (math.pi / L)
    cm = jnp.cos(ang)
    nyq = jnp.where(s % 2 == 0, 1.0, -1.0).astype(F32)
    sm = jnp.where(f == 0, nyq, -jnp.sin(ang))
    return tuple(jnp.stack(_split_bf16(t)) for t in (cm, sm, sm.T))


def _filter_features(L):
    t = jnp.linspace(0.0, 1.0, L, dtype=F32)[:, None]
    w = 2.0 * math.pi * jnp.arange(L, dtype=F32)[:, None] / L
    f = jnp.linspace(1e-4, FILTER_BANDS - 1, FILTER_BANDS, dtype=F32)[None, :]
    z = jnp.concatenate([t, jnp.cos(f * w), -jnp.sin(f * w)], axis=-1)
    return jnp.pad(z, ((0, 0), (0, FILT_PAD - FILTER_EMB)))


def _filter_kernel(z_ref, w1_ref, b1_ref, w2_ref, b2_ref, w3f_ref, w3b_ref, freq_ref, delta_ref,
                   cm_ref, sm_ref, ka_ref, kb_ref, *, L):
    fr = freq_ref[...]
    z = z_ref[...]
    h = jnp.sin(fr * (jnp.dot(z, w1_ref[...], precision=HIGHEST, preferred_element_type=F32) + b1_ref[...]))
    h = jnp.sin(fr * (jnp.dot(h, w2_ref[...], precision=HIGHEST, preferred_element_type=F32) + b2_ref[...]))
    decay = jnp.exp(-z[:, 0:1] * jnp.abs(delta_ref[...]))
    h_fwd = jnp.dot(h, w3f_ref[...], precision=HIGHEST, preferred_element_type=F32) * decay
    h_bwd = jnp.dot(h, w3b_ref[...], precision=HIGHEST, preferred_element_type=F32) * decay
    pos = lax.broadcasted_iota(jnp.int32, (L, 1), 0)
    first = pos == 0
    h_bwd = jnp.where(first, 0.0, h_bwd)
    even = h_fwd + h_bwd
    odd = h_fwd - h_bwd
    sign = jnp.where((pos & 1) == 0, 1.0, -1.0)
    nyq = jnp.sum(sign * even, axis=0, keepdims=True)
    ka_ref[...] = _dot_split(cm_ref[0], cm_ref[1], even)
    kb = _dot_split(sm_ref[0], sm_ref[1], odd)
    kb_ref[...] = jnp.where(first, nyq, kb)


FILT_TC = 256


def _filter_spectrum(L, z, w1p, b1, w2, b2, w3, freq, deltas, cm, sm):
    nb = D_HYENA // FILT_TC
    est = 2 * (2 * 2 * L * L * 2) + 2 * 4 * L * FILT_TC * 4 + 10 * L * FILT_TC * 4 + 4 * L * FILT_PAD * 4
    full = lambda shape: pl.BlockSpec(shape, lambda c: (0,) * len(shape))
    return pl.pallas_call(
        functools.partial(_filter_kernel, L=L),
        name=f"hyena_filter_{L}",
        grid=(nb,),
        in_specs=[
            full((L, FILT_PAD)), full((FILT_PAD, FILTER_ORDER)), full((1, FILTER_ORDER)),
            full((FILTER_ORDER, FILTER_ORDER)), full((1, FILTER_ORDER)),
            pl.BlockSpec((FILTER_ORDER, FILT_TC), lambda c: (0, c)),
            pl.BlockSpec((FILTER_ORDER, FILT_TC), lambda c: (0, nb + c)),
            full((1, FILTER_ORDER)),
            pl.BlockSpec((1, FILT_TC), lambda c: (0, c)),
            full((2, L, L)), full((2, L, L)),
        ],
        out_specs=[pl.BlockSpec((L, FILT_TC), lambda c: (0, c)),
                   pl.BlockSpec((L, FILT_TC), lambda c: (0, c))],
        out_shape=[jax.ShapeDtypeStruct((L, D_HYENA), F32)] * 2,
        compiler_params=pltpu.CompilerParams(
            dimension_semantics=("arbitrary",), vmem_limit_bytes=_vmem_limit(est)),
    )(z, w1p, b1, w2, b2, w3, w3, freq, deltas, cm, sm)


def _hyena_kernel(x0_ref, x1_ref, v_ref, cw0_ref, cw1_ref, cw2_ref, cb0_ref, cb1_ref, cb2_ref, hb_ref,
                  ka_ref, kb_ref, cm_ref, sm_ref, smt_ref, gn_ref, o_ref, acc_ref, *, L, tc, nct):
    ct = pl.program_id(1)
    pos = lax.broadcasted_iota(jnp.int32, (L, 1), 0)
    first = pos == 0
    has_prev = pos != 0
    has_next = pos != L - 1
    x0 = _conv3_rows(x0_ref[...], cw0_ref[...], cb0_ref[...], has_prev, has_next)
    x1 = _conv3_rows(x1_ref[...], cw1_ref[...], cb1_ref[...], has_prev, has_next)
    v = _conv3_rows(v_ref[...], cw2_ref[...], cb2_ref[...], has_prev, has_next)
    u = x1 * v
    top = _dot_split(cm_ref[0], cm_ref[1], u)
    bot = _dot_split(sm_ref[0], sm_ref[1], u)
    ka = ka_ref[...]
    kb = kb_ref[...]
    y_top = top * ka - jnp.where(first, 0.0, bot * kb)
    y_bot = jnp.where(first, bot * kb, top * kb + bot * ka)
    wgt = jnp.where(first, 1.0 / (2 * L), 2.0 / (2 * L))
    y = _dot_split(cm_ref[0], cm_ref[1], y_top * wgt) + _dot_split(smt_ref[0], smt_ref[1], y_bot * wgt)
    acc_ref[ct] = x0 * (y + u * hb_ref[...])

    @pl.when(ct == nct - 1)
    def _():
        ss = jnp.zeros((L, 1), F32)
        for j in range(nct):
            a = acc_ref[j]
            ss = ss + jnp.sum(a * a, axis=-1, keepdims=True)
        inv = lax.rsqrt(ss * (1.0 / D_HYENA) + RMS_EPS)
        for j in range(nct):
            o_ref[:, j * tc:(j + 1) * tc] = (acc_ref[j] * inv * gn_ref[:, j * tc:(j + 1) * tc]).astype(o_ref.dtype)


def _into_rows(kernel_fn, dest):
    if dest is None:
        return kernel_fn, [], [], {}

    def body(dest_ref, *refs):
        del dest_ref
        kernel_fn(*refs)
    return body, [pl.BlockSpec(memory_space=pl.ANY)], [dest], {0: 0}


def _hyena(proj, conv_w_t, conv_b, hy_bias, ka, kb, cm, sm, smt, gn, *, L, tc, nseq, row_block0, dest=None):
    nct = D_HYENA // tc
    cpb = D_HYENA // tc
    est = (3 * 2 * L * L * 2 + 2 * (3 * L * tc * 4 + 2 * L * tc * 4) + nct * L * tc * 4
           + 2 * L * D_HYENA * 2 + 14 * L * tc * 4)
    once = pl.Buffered(1)
    grp = lambda g: pl.BlockSpec((L, tc), lambda s, c: (row_block0 + s, g * cpb + c))
    tap = lambda g: pl.BlockSpec((3, tc), lambda s, c: (0, g * cpb + c))
    bias = lambda g: pl.BlockSpec((1, tc), lambda s, c: (0, g * cpb + c))
    col = lambda rows: pl.BlockSpec((rows, tc), lambda s, c: (0, c))
    table = pl.BlockSpec((2, L, L), lambda s, c: (0, 0, 0), pipeline_mode=once)
    body, dest_specs, dest_args, aliases = _into_rows(functools.partial(_hyena_kernel, L=L, tc=tc, nct=nct), dest)
    return pl.pallas_call(
        body,
        name=f"hyena_conv_{L}",
        grid=(nseq, nct),
        in_specs=dest_specs + [grp(0), grp(1), grp(2), tap(0), tap(1), tap(2), bias(0), bias(1), bias(2),
                               col(1), col(L), col(L), table, table, table,
                               pl.BlockSpec((1, D_HYENA), lambda s, c: (0, 0))],
        out_specs=pl.BlockSpec((L, D_HYENA), lambda s, c: (row_block0 + s, 0)),
        out_shape=jax.ShapeDtypeStruct((N_TOK, D_HYENA), BF16),
        input_output_aliases=aliases,
        scratch_shapes=[pltpu.VMEM((nct, L, tc), F32)],
        compiler_params=pltpu.CompilerParams(
            dimension_semantics=("arbitrary", "arbitrary"), vmem_limit_bytes=_vmem_limit(est)),
    )(*dest_args, proj, proj, proj, conv_w_t, conv_w_t, conv_w_t, conv_b, conv_b, conv_b,
      hy_bias, ka, kb, cm, sm, smt, gn)


def _dot_nt(a, b):
    return lax.dot_general(a, b, (((1,), (1,)), ((), ())), preferred_element_type=F32)


def _head_norm_store(acc_ref, gn_ref, o_ref):
    y = acc_ref[...]
    inv = lax.rsqrt(jnp.mean(y * y, axis=-1, keepdims=True) + RMS_EPS)
    o_ref[...] = (y * inv * gn_ref[...]).astype(o_ref.dtype)


def _attn_ctx_kernel(sink_ref, q_ref, k_ref, v_ref, gn_ref, o_ref, ko_ref, vo_ref, acc_ref):
    ko_ref[...] = k_ref[...]
    vo_ref[...] = v_ref[...]
    for kvh in range(N_KV_HEADS):
        ksl = slice(kvh * HEAD_DIM, (kvh + 1) * HEAD_DIM)
        kh = k_ref[:, ksl].astype(BF16)
        vh = v_ref[:, ksl].astype(BF16)
        for g in range(KV_GROUP):
            hd = kvh * KV_GROUP + g
            hsl = slice(hd * HEAD_DIM, (hd + 1) * HEAD_DIM)
            s = _dot_nt(q_ref[:, hsl].astype(BF16), kh) * ATTN_SCALE
            sk = sink_ref[hd]
            m = jnp.maximum(jnp.max(s, axis=-1, keepdims=True), sk)
            e = jnp.exp(s - m)
            den = jnp.sum(e, axis=-1, keepdims=True) + jnp.exp(sk - m)
            acc_ref[:, hsl] = jnp.dot((e / den).astype(BF16), vh, preferred_element_type=F32)
    _head_norm_store(acc_ref, gn_ref, o_ref)


def _attn_ctx(proj, sink, gn):
    est = 2 * (SEQ * D_ATTN * 4 + 4 * SEQ * KV_DIM * 4 + SEQ * D_ATTN * 2) + 3 * SEQ * D_ATTN * 4
    return pl.pallas_call(
        _attn_ctx_kernel,
        name="attn_ctx",
        grid=(BATCH,),
        in_specs=[
            pl.BlockSpec(memory_space=pltpu.SMEM),
            pl.BlockSpec((SEQ, D_ATTN), lambda b: (b, O_Q // D_ATTN)),
            pl.BlockSpec((SEQ, KV_DIM), lambda b: (b, O_K // KV_DIM)),
            pl.BlockSpec((SEQ, KV_DIM), lambda b: (b, O_V // KV_DIM)),
            pl.BlockSpec((1, D_ATTN), lambda b: (0, 0)),
        ],
        out_specs=[pl.BlockSpec((SEQ, D_ATTN), lambda b: (b, 0)),
                   pl.BlockSpec((SEQ, KV_DIM), lambda b: (b, 0)),
                   pl.BlockSpec((SEQ, KV_DIM), lambda b: (b, 0))],
        out_shape=[jax.ShapeDtypeStruct((N_TOK, D_ATTN), BF16),
                   jax.ShapeDtypeStruct((N_CTX, KV_DIM), F32),
                   jax.ShapeDtypeStruct((N_CTX, KV_DIM), F32)],
        scratch_shapes=[pltpu.VMEM((SEQ, D_ATTN), F32)],
        compiler_params=pltpu.CompilerParams(
            dimension_semantics=("arbitrary",), vmem_limit_bytes=_vmem_limit(est)),
    )(sink, proj, proj, proj, gn)


def _rope_tables():
    half = HEAD_DIM // 2
    quarter = half // 2
    pos = jnp.arange(DEC_SEQ, dtype=jnp.int32)
    row_pos = (pos // GRID_W).astype(F32)
    col_pos = (pos % GRID_W).astype(F32)
    inv = ROPE_BASE ** (-jnp.arange(quarter, dtype=F32) / quarter)
    lane = jnp.arange(HEAD_DIM, dtype=jnp.int32)
    p = jnp.where((lane < half)[None, :], row_pos[:, None], col_pos[:, None])
    ang = p * inv[lane % quarter][None, :]
    low = ((lane % half) < quarter)[None, :]
    cos = jnp.cos(ang)
    sin = jnp.sin(ang)
    return cos, jnp.where(low, -sin, 0.0), jnp.where(low, 0.0, sin)


ROPE_SHIFT = HEAD_DIM // 4
LAT_SPAN = BLOCK + 2 * WINDOW


def _rope(x, cos, sin_lo, sin_hi):
    return (x * cos + pltpu.roll(x, HEAD_DIM - ROPE_SHIFT, 1) * sin_lo
            + pltpu.roll(x, ROPE_SHIFT, 1) * sin_hi)


def _attn_lat_kernel(sink_ref, q_ref, k_ref, v_ref, ck_ref, cv_ref, cos_ref, slo_ref, shi_ref, gn_ref,
                     o_ref, acc_ref):
    i = pl.program_id(1)
    q0 = pl.multiple_of(i * BLOCK, BLOCK)
    k0 = pl.multiple_of(jnp.clip((i - 1) * BLOCK, 0, DEC_SEQ - LAT_SPAN), BLOCK)
    q_tab = (cos_ref[pl.ds(q0, BLOCK), :], slo_ref[pl.ds(q0, BLOCK), :], shi_ref[pl.ds(q0, BLOCK), :])
    k_tab = (cos_ref[pl.ds(k0, LAT_SPAN), :], slo_ref[pl.ds(k0, LAT_SPAN), :], shi_ref[pl.ds(k0, LAT_SPAN), :])
    q_pos = q0 + lax.broadcasted_iota(jnp.int32, (BLOCK, 1), 0)
    k_pos = k0 + lax.broadcasted_iota(jnp.int32, (1, LAT_SPAN), 1)
    valid = jnp.abs(q_pos - k_pos) <= WINDOW
    for kvh in range(N_KV_HEADS):
        ksl = slice(kvh * HEAD_DIM, (kvh + 1) * HEAD_DIM)
        kw = _rope(k_ref[pl.ds(k0, LAT_SPAN), ksl], *k_tab).astype(BF16)
        vw = v_ref[pl.ds(k0, LAT_SPAN), ksl].astype(BF16)
        kc = ck_ref[0, :, ksl].astype(BF16)
        vc = cv_ref[0, :, ksl].astype(BF16)
        for g in range(KV_GROUP):
            hd = kvh * KV_GROUP + g
            hsl = slice(hd * HEAD_DIM, (hd + 1) * HEAD_DIM)
            qh = _rope(q_ref[:, hsl], *q_tab).astype(BF16)
            s_lat = jnp.where(valid, _dot_nt(qh, kw) * ATTN_SCALE, NEG_INF)
            s_ctx = _dot_nt(qh, kc) * ATTN_SCALE
            sk = sink_ref[hd]
            m = jnp.maximum(jnp.maximum(jnp.max(s_lat, axis=-1, keepdims=True),
                                        jnp.max(s_ctx, axis=-1, keepdims=True)), sk)
            e_lat = jnp.exp(s_lat - m)
            e_ctx = jnp.exp(s_ctx - m)
            den = (jnp.sum(e_lat, axis=-1, keepdims=True) + jnp.sum(e_ctx, axis=-1, keepdims=True)
                   + jnp.exp(sk - m))
            acc_ref[:, hsl] = (jnp.dot((e_lat / den).astype(BF16), vw, preferred_element_type=F32)
                               + jnp.dot((e_ctx / den).astype(BF16), vc, preferred_element_type=F32))
    _head_norm_store(acc_ref, gn_ref, o_ref)


def _attn_lat(proj, cache_k, cache_v, sink, rope, gn, dest):
    nqb = DEC_SEQ // BLOCK
    est = (2 * (BLOCK * D_ATTN * 4 + 2 * DEC_SEQ * KV_DIM * 4 + 2 * PAST_LEN * KV_DIM * 4
                + 3 * DEC_SEQ * HEAD_DIM * 4 + BLOCK * D_ATTN * 2) + 8 * BLOCK * D_ATTN * 4)
    lat_row_block0 = N_CTX // BLOCK
    lat_seq_block0 = N_CTX // DEC_SEQ
    tab = pl.BlockSpec((DEC_SEQ, HEAD_DIM), lambda b, i: (0, 0))
    body, dest_specs, dest_args, aliases = _into_rows(_attn_lat_kernel, dest)
    return pl.pallas_call(
        body,
        name="attn_lat",
        grid=(DEC_BATCH, nqb),
        input_output_aliases=aliases,
        in_specs=dest_specs + [
            pl.BlockSpec(memory_space=pltpu.SMEM),
            pl.BlockSpec((BLOCK, D_ATTN), lambda b, i: (lat_row_block0 + b * nqb + i, O_Q // D_ATTN)),
            pl.BlockSpec((DEC_SEQ, KV_DIM), lambda b, i: (lat_seq_block0 + b, O_K // KV_DIM)),
            pl.BlockSpec((DEC_SEQ, KV_DIM), lambda b, i: (lat_seq_block0 + b, O_V // KV_DIM)),
            pl.BlockSpec((1, PAST_LEN, KV_DIM), lambda b, i: (b, 0, 0)),
            pl.BlockSpec((1, PAST_LEN, KV_DIM), lambda b, i: (b, 0, 0)),
            tab, tab, tab,
            pl.BlockSpec((1, D_ATTN), lambda b, i: (0, 0)),
        ],
        out_specs=pl.BlockSpec((BLOCK, D_ATTN), lambda b, i: (lat_row_block0 + b * nqb + i, 0)),
        out_shape=jax.ShapeDtypeStruct((N_TOK, D_ATTN), BF16),
        scratch_shapes=[pltpu.VMEM((BLOCK, D_ATTN), F32)],
        compiler_params=pltpu.CompilerParams(
            dimension_semantics=("arbitrary", "arbitrary"), vmem_limit_bytes=_vmem_limit(est)),
    )(*dest_args, sink, proj, proj, proj, cache_k, cache_v, *rope, gn)


def kernel(x_prompt, x_sample, cache_k, cache_v, c, c_ctx, w_ada, b_ada, norm_mix, w_in, hy_conv_w, hy_conv_b, filt_w1, filt_b1, filt_w2, filt_b2, filt_w3, filt_freq, hy_bias, attn_sink, gn_hyena, gn_attn, w_out, norm_ffn, w_up, ffn_conv_w, ffn_conv_b, w_down, final_norm):
    x = jnp.concatenate([x_prompt.reshape(N_CTX, D_MODEL), x_sample.reshape(N_LAT, D_MODEL)], axis=0)
    cond = jnp.concatenate([c_ctx[None, :], c, jnp.zeros((MOD_ROWS - 1 - DEC_BATCH, D_MODEL), F32)], axis=0)
    mod_all = _modulation(cond, w_ada, b_ada)

    rope = _rope_tables()
    tables = {L: _dft_tables(L) for L in (SEQ, DEC_SEQ)}
    feats = {L: _filter_features(L) for L in (SEQ, DEC_SEQ)}
    deltas = jnp.linspace(MIN_DECAY, MAX_DECAY, D_HYENA, dtype=F32)[None, :]

    new_k = []
    new_v = []
    for i in range(DEPTH):
        mod = mod_all[i]
        h = _prenorm(x, norm_mix[i], mod, 0, 1)
        proj = _matmul(h, w_in[i], 1024, 512)

        hy_w_t = hy_conv_w[i].T
        hy_b = hy_conv_b[i][None, :]
        w1p = jnp.pad(filt_w1[i], ((0, FILT_PAD - FILTER_EMB), (0, 0)))
        y_a = None
        for L, tc, nseq, rb0 in ((SEQ, 1024, BATCH, 0), (DEC_SEQ, 256, DEC_BATCH, N_CTX // DEC_SEQ)):
            cm, sm, smt = tables[L]
            ka, kb = _filter_spectrum(L, feats[L], w1p, filt_b1[i][None, :], filt_w2[i], filt_b2[i][None, :],
                                      filt_w3[i], filt_freq[i][None, :], deltas, cm, sm)
            y_a = _hyena(proj, hy_w_t, hy_b, hy_bias[i][None, :], ka, kb, cm, sm, smt,
                         gn_hyena[i][None, :], L=L, tc=tc, nseq=nseq, row_block0=rb0, dest=y_a)
        gn_a = gn_attn[i][None, :]
        y_b, k_ctx, v_ctx = _attn_ctx(proj, attn_sink[i], gn_a)
        y_b = _attn_lat(proj, cache_k[:, i].reshape(DEC_BATCH, PAST_LEN, KV_DIM),
                        cache_v[:, i].reshape(DEC_BATCH, PAST_LEN, KV_DIM), attn_sink[i], rope, gn_a, y_b)
        x = _matmul_residual([y_a, y_b], w_out[i], x, mod, 2, 1024, 512, "out_proj")

        h = _prenorm(x, norm_ffn[i], mod, 3, 4)
        act = _ffn_up(h, w_up[i], ffn_conv_w[i].T, ffn_conv_b[i][None, :])
        x = _matmul_residual([act], w_down[i].astype(BF16), x, mod, 5, 512, 512, "ffn_down")

        new_k.append(k_ctx.reshape(BATCH, SEQ, N_KV_HEADS, HEAD_DIM))
        new_v.append(v_ctx.reshape(BATCH, SEQ, N_KV_HEADS, HEAD_DIM))

    y_prompt = _final_norm(x, final_norm, 0, N_CTX).reshape(BATCH, SEQ, D_MODEL)
    y_sample = _final_norm(x, final_norm, N_CTX // NORM_TR, N_LAT).reshape(DEC_BATCH, DEC_SEQ, D_MODEL)
    return (y_prompt, y_sample, jnp.stack(new_k, axis=1), jnp.stack(new_v, axis=1))
```

```python
import functools
import math

import jax
import jax.numpy as jnp
from jax import lax
from jax.experimental import pallas as pl
from jax.experimental.pallas import tpu as pltpu

F32 = jnp.float32
BF16 = jnp.bfloat16
HIGHEST = lax.Precision.HIGHEST

D_MODEL = 4096
BATCH = 32
SEQ = 256
DEPTH = 4
DEC_BATCH = 2
DEC_SEQ = 1024
PAST_LEN = 512
GRID_W = 64
D_HYENA = D_MODEL // 2
D_ATTN = D_MODEL - D_HYENA
HEAD_DIM = 128
N_HEADS = D_ATTN // HEAD_DIM
N_KV_HEADS = 4
KV_GROUP = N_HEADS // N_KV_HEADS
KV_DIM = N_KV_HEADS * HEAD_DIM
WINDOW = 128
BLOCK = 128
ATTN_SCALE = 1.0 / math.sqrt(HEAD_DIM)
ROPE_BASE = 10000.0
FILTER_EMB = 33
FILTER_BANDS = (FILTER_EMB - 1) // 2
FILTER_ORDER = 64
DECAY_TARGET = 1e-2
MAX_DECAY = math.log(DECAY_TARGET) / 0.3
MIN_DECAY = math.log(DECAY_TARGET) / 1.5
D_FF = 11008
RMS_EPS = 1e-6
NEG_INF = -1e30
O_Q = 3 * D_HYENA
O_K = O_Q + D_ATTN
O_V = O_K + KV_DIM
D_IN = O_V + KV_DIM

N_CTX = BATCH * SEQ
N_LAT = DEC_BATCH * DEC_SEQ
N_TOK = N_CTX + N_LAT
MOD_ROWS = 8
N_MOD = 6 * D_MODEL

V7X_VMEM_BYTES = 64 * 1024 * 1024
V7X_VMEM_USABLE = 58 * 1024 * 1024
MIB = 1024 * 1024


def _vmem_limit(estimate_bytes):
    return int(min(V7X_VMEM_USABLE, estimate_bytes + estimate_bytes // 4 + 2 * MIB))


def _mod_row(row_start):
    return jnp.where(row_start < N_CTX, 0, 1 + (row_start - N_CTX) // DEC_SEQ)


def _seq_len(row_start):
    return jnp.where(row_start < N_CTX, SEQ, DEC_SEQ)


SUBLANES = 8


def _zero_rows_at(x, sub, breaks):
    pick = lax.broadcasted_iota(jnp.int32, (SUBLANES, 1), 0) == sub
    pieces = []
    at = 0
    for start in sorted(breaks):
        if start > at:
            pieces.append(x[at:start])
        cond = pick if breaks[start] is True else jnp.logical_and(pick, breaks[start])
        pieces.append(jnp.where(cond, 0.0, x[start:start + SUBLANES]))
        at = start + SUBLANES
    if at < x.shape[0]:
        pieces.append(x[at:])
    return jnp.concatenate(pieces, axis=0)


def _conv3_rows(u, w, b, seq_rows=None, split=False):
    rows = u.shape[0]
    seq_rows = rows if seq_rows is None else seq_rows
    starts = range(0, rows, seq_rows)
    up = _zero_rows_at(pltpu.roll(u, 1, 0), 0, {s: (True if s == 0 else split) for s in starts})
    un = _zero_rows_at(pltpu.roll(u, rows - 1, 0), SUBLANES - 1,
                       {s + seq_rows - SUBLANES: (True if s + seq_rows == rows else split) for s in starts})
    return up * w[0:1, :] + u * w[1:2, :] + un * w[2:3, :] + b


MOD_TN = 512


def _mod_kernel(c_ref, w_ref, b_ref, o_ref):
    c = c_ref[...]
    s = (c * jax.nn.sigmoid(c)).astype(BF16)
    o_ref[0] = jnp.dot(s, w_ref[0].astype(BF16), preferred_element_type=F32) + b_ref[0]


def _modulation(cond, w_ada, b_ada):
    est = 2 * D_MODEL * MOD_TN * 4 + D_MODEL * MOD_TN * 2 + 4 * MOD_ROWS * D_MODEL * 4
    return pl.pallas_call(
        _mod_kernel,
        name="adaln_mod",
        grid=(DEPTH, N_MOD // MOD_TN),
        in_specs=[
            pl.BlockSpec((MOD_ROWS, D_MODEL), lambda l, n: (0, 0)),
            pl.BlockSpec((1, D_MODEL, MOD_TN), lambda l, n: (l, 0, n)),
            pl.BlockSpec((1, 1, MOD_TN), lambda l, n: (l, 0, n)),
        ],
        out_specs=pl.BlockSpec((1, MOD_ROWS, MOD_TN), lambda l, n: (l, 0, n)),
        out_shape=jax.ShapeDtypeStruct((DEPTH, MOD_ROWS, N_MOD), F32),
        compiler_params=pltpu.CompilerParams(
            dimension_semantics=("arbitrary", "arbitrary"), vmem_limit_bytes=_vmem_limit(est)),
    )(cond, w_ada, b_ada.reshape(DEPTH, 1, N_MOD))


NORM_TR = 256


def _prenorm_kernel(x_ref, g_ref, shift_ref, scale_ref, o_ref):
    r = _mod_row(pl.program_id(0) * NORM_TR)
    x = x_ref[...]
    y = x * lax.rsqrt(jnp.mean(x * x, axis=-1, keepdims=True) + RMS_EPS) * g_ref[...]
    o_ref[...] = (y * (1.0 + scale_ref[pl.ds(r, 1), :]) + shift_ref[pl.ds(r, 1), :]).astype(o_ref.dtype)


def _prenorm(x, g, mod, shift_chunk, scale_chunk):
    est = 2 * NORM_TR * D_MODEL * (4 + 2) + 3 * NORM_TR * D_MODEL * 4
    return pl.pallas_call(
        _prenorm_kernel,
        name="prenorm",
        grid=(N_TOK // NORM_TR,),
        in_specs=[
            pl.BlockSpec((NORM_TR, D_MODEL), lambda i: (i, 0)),
            pl.BlockSpec((1, D_MODEL), lambda i: (0, 0)),
            pl.BlockSpec((MOD_ROWS, D_MODEL), lambda i: (0, shift_chunk)),
            pl.BlockSpec((MOD_ROWS, D_MODEL), lambda i: (0, scale_chunk)),
        ],
        out_specs=pl.BlockSpec((NORM_TR, D_MODEL), lambda i: (i, 0)),
        out_shape=jax.ShapeDtypeStruct((N_TOK, D_MODEL), BF16),
        compiler_params=pltpu.CompilerParams(
            dimension_semantics=("arbitrary",), vmem_limit_bytes=_vmem_limit(est)),
    )(x, g.reshape(1, D_MODEL), mod, mod)


def _final_norm_kernel(x_ref, g_ref, o_ref):
    x = x_ref[...]
    o_ref[...] = x * lax.rsqrt(jnp.mean(x * x, axis=-1, keepdims=True) + RMS_EPS) * g_ref[...]


def _final_norm(x, g, row_block0, rows):
    est = 4 * NORM_TR * D_MODEL * 4 + 2 * NORM_TR * D_MODEL * 4
    return pl.pallas_call(
        _final_norm_kernel,
        name="final_norm",
        grid=(rows // NORM_TR,),
        in_specs=[
            pl.BlockSpec((NORM_TR, D_MODEL), lambda i: (row_block0 + i, 0)),
            pl.BlockSpec((1, D_MODEL), lambda i: (0, 0)),
        ],
        out_specs=pl.BlockSpec((NORM_TR, D_MODEL), lambda i: (i, 0)),
        out_shape=jax.ShapeDtypeStruct((rows, D_MODEL), F32),
        compiler_params=pltpu.CompilerParams(
            dimension_semantics=("arbitrary",), vmem_limit_bytes=_vmem_limit(est)),
    )(x, g.reshape(1, D_MODEL))


def _weight_tile(w_ref, wb_ref):
    if wb_ref is None:
        return w_ref

    @pl.when(pl.program_id(1) == 0)
    def _():
        wb_ref[...] = w_ref[...].astype(BF16)
    return wb_ref


def _matmul_kernel(x_ref, w_ref, o_ref, wb_ref):
    w = _weight_tile(w_ref, wb_ref)
    o_ref[...] = jnp.dot(x_ref[...], w[...], preferred_element_type=F32)


def _weight_spec(w_stack, layer, tn, single_buffer):
    k = w_stack.shape[1]
    mode = {"pipeline_mode": pl.Buffered(1)} if single_buffer else {}
    return pl.BlockSpec((None, k, tn), lambda j, i: (layer, 0, j), **mode)


def _matmul(x, w_stack, layer, tm, tn):
    m, k = x.shape
    n = w_stack.shape[2]
    est = 2 * (tm * k * 2 + tm * tn * 4) + k * tn * 4 + k * tn * 2 + tm * tn * 4
    return pl.pallas_call(
        _matmul_kernel,
        name="in_proj",
        grid=(n // tn, m // tm),
        in_specs=[
            pl.BlockSpec((tm, k), lambda j, i: (i, 0)),
            _weight_spec(w_stack, layer, tn, single_buffer=True),
        ],
        out_specs=pl.BlockSpec((tm, tn), lambda j, i: (i, j)),
        out_shape=jax.ShapeDtypeStruct((m, n), F32),
        scratch_shapes=[pltpu.VMEM((k, tn), BF16)],
        compiler_params=pltpu.CompilerParams(
            dimension_semantics=("arbitrary", "arbitrary"), vmem_limit_bytes=_vmem_limit(est)),
    )(x, w_stack)


def _matmul_residual_kernel(*refs, n_lhs, tm, cast_weight):
    lhs = refs[:n_lhs]
    w_ref, res_ref, gate_ref, o_ref = refs[n_lhs:n_lhs + 4]
    w = _weight_tile(w_ref, refs[n_lhs + 4] if cast_weight else None)
    r = _mod_row(pl.program_id(1) * tm)
    y = None
    k0 = 0
    for x_ref in lhs:
        kk = x_ref.shape[1]
        part = jnp.dot(x_ref[...], w[k0:k0 + kk, :], preferred_element_type=F32)
        y = part if y is None else y + part
        k0 += kk
    o_ref[...] = res_ref[...] + gate_ref[pl.ds(r, 1), :] * y


def _matmul_residual(lhs, w_stack, layer, res, mod, gate_chunk, tm, tn, name):
    m = res.shape[0]
    _, k, n = w_stack.shape
    cast_weight = w_stack.dtype != BF16
    w_bytes = k * tn * 4 + k * tn * 2 if cast_weight else 2 * k * tn * 2
    est = 2 * (tm * k * 2 + 2 * tm * tn * 4 + MOD_ROWS * tn * 4) + w_bytes + 2 * tm * tn * 4
    gate_block0 = gate_chunk * (D_MODEL // tn)
    return pl.pallas_call(
        functools.partial(_matmul_residual_kernel, n_lhs=len(lhs), tm=tm, cast_weight=cast_weight),
        name=name,
        grid=(n // tn, m // tm),
        in_specs=[pl.BlockSpec((tm, x.shape[1]), lambda j, i: (i, 0)) for x in lhs] + [
            _weight_spec(w_stack, layer, tn, single_buffer=cast_weight),
            pl.BlockSpec((tm, tn), lambda j, i: (i, j)),
            pl.BlockSpec((MOD_ROWS, tn), lambda j, i: (0, gate_block0 + j)),
        ],
        out_specs=pl.BlockSpec((tm, tn), lambda j, i: (i, j)),
        out_shape=jax.ShapeDtypeStruct((m, n), F32),
        scratch_shapes=[pltpu.VMEM((k, tn), BF16)] if cast_weight else [],
        compiler_params=pltpu.CompilerParams(
            dimension_semantics=("arbitrary", "arbitrary"), vmem_limit_bytes=_vmem_limit(est)),
    )(*lhs, w_stack, res, mod)


FFN_TM = 1024
FFN_TN = 256


def _ffn_up_kernel(h_ref, wg_ref, wv_ref, cwg_ref, cwv_ref, cbg_ref, cbv_ref, o_ref, wb_ref):
    @pl.when(pl.program_id(1) == 0)
    def _():
        wb_ref[:, :FFN_TN] = wg_ref[...].astype(BF16)
        wb_ref[:, FFN_TN:] = wv_ref[...].astype(BF16)

    u = jnp.dot(h_ref[...], wb_ref[...], preferred_element_type=F32)
    is_ctx = pl.program_id(1) * FFN_TM < N_CTX
    g = _conv3_rows(u[:, :FFN_TN], cwg_ref[...], cbg_ref[...], SEQ, is_ctx)
    v = _conv3_rows(u[:, FFN_TN:], cwv_ref[...], cbv_ref[...], SEQ, is_ctx)
    o_ref[...] = (g * jax.nn.sigmoid(g) * v).astype(o_ref.dtype)


def _ffn_up(h, w_up, layer, conv_w_t, conv_b):
    nb = D_FF // FFN_TN
    est = (2 * (FFN_TM * D_MODEL * 2 + 2 * D_MODEL * FFN_TN * 4 + FFN_TM * FFN_TN * 2)
           + 2 * D_MODEL * FFN_TN * 2 + 10 * FFN_TM * FFN_TN * 4)
    return pl.pallas_call(
        _ffn_up_kernel,
        name="ffn_up",
        grid=(nb, N_TOK // FFN_TM),
        in_specs=[
            pl.BlockSpec((FFN_TM, D_MODEL), lambda j, i: (i, 0)),
            pl.BlockSpec((None, D_MODEL, FFN_TN), lambda j, i: (layer, 0, j)),
            pl.BlockSpec((None, D_MODEL, FFN_TN), lambda j, i: (layer, 0, nb + j)),
            pl.BlockSpec((3, FFN_TN), lambda j, i: (0, j)),
            pl.BlockSpec((3, FFN_TN), lambda j, i: (0, nb + j)),
            pl.BlockSpec((1, FFN_TN), lambda j, i: (0, j)),
            pl.BlockSpec((1, FFN_TN), lambda j, i: (0, nb + j)),
        ],
        out_specs=pl.BlockSpec((FFN_TM, FFN_TN), lambda j, i: (i, j)),
        out_shape=jax.ShapeDtypeStruct((N_TOK, D_FF), BF16),
        scratch_shapes=[pltpu.VMEM((D_MODEL, 2 * FFN_TN), BF16)],
        compiler_params=pltpu.CompilerParams(
            dimension_semantics=("arbitrary", "arbitrary"), vmem_limit_bytes=_vmem_limit(est)),
    )(h, w_up, w_up, conv_w_t, conv_w_t, conv_b, conv_b)


FILT_PAD = 64


def _split_bf16(a):
    hi = a.astype(BF16)
    return hi, (a - hi.astype(F32)).astype(BF16)


def _dot_split(t_hi, t_lo, b):
    b_hi, b_lo = _split_bf16(b)
    return (jnp.dot(t_hi, b_hi, preferred_element_type=F32)
            + jnp.dot(t_hi, b_lo, preferred_element_type=F32)
            + jnp.dot(t_lo, b_hi, preferred_element_type=F32))


def _dft_tables(L):
    f = jnp.arange(L, dtype=jnp.int32)[:, None]
    s = jnp.arange(L, dtype=jnp.int32)[None, :]
    ang = ((f * s) % (2 * L)).astype(F32) * (math.pi / L)
    cm = jnp.cos(ang)
    nyq = jnp.where(s % 2 == 0, 1.0, -1.0).astype(F32)
    sm = jnp.where(f == 0, nyq, -jnp.sin(ang))
    return tuple(jnp.stack(_split_bf16(t)) for t in (cm, sm, sm.T))


def _filter_features(L):
    t = jnp.linspace(0.0, 1.0, L, dtype=F32)[:, None]
    w = 2.0 * math.pi * jnp.arange(L, dtype=F32)[:, None] / L
    f = jnp.linspace(1e-4, FILTER_BANDS - 1, FILTER_BANDS, dtype=F32)[None, :]
    z = jnp.concatenate([t, jnp.cos(f * w), -jnp.sin(f * w)], axis=-1)
    return jnp.pad(z, ((0, 0), (0, FILT_PAD - FILTER_EMB)))


def _filter_kernel(z_ref, w1_ref, b1_ref, w2_ref, b2_ref, w3f_ref, w3b_ref, freq_ref, delta_ref,
                   cm_ref, sm_ref, ka_ref, kb_ref, *, L):
    fr = freq_ref[...]
    z = z_ref[...]
    h = jnp.sin(fr * (jnp.dot(z, w1_ref[...], precision=HIGHEST, preferred_element_type=F32) + b1_ref[...]))
    h = jnp.sin(fr * (jnp.dot(h, w2_ref[...], precision=HIGHEST, preferred_element_type=F32) + b2_ref[...]))
    decay = jnp.exp(-z[:, 0:1] * jnp.abs(delta_ref[...]))
    h_fwd = jnp.dot(h, w3f_ref[...], precision=HIGHEST, preferred_element_type=F32) * decay
    h_bwd = jnp.dot(h, w3b_ref[...], precision=HIGHEST, preferred_element_type=F32) * decay
    pos = lax.broadcasted_iota(jnp.int32, (L, 1), 0)
    first = pos == 0
    h_bwd = jnp.where(first, 0.0, h_bwd)
    even = h_fwd + h_bwd
    odd = h_fwd - h_bwd
    sign = jnp.where((pos & 1) == 0, 1.0, -1.0)
    nyq = jnp.sum(sign * even, axis=0, keepdims=True)
    ka_ref[...] = _dot_split(cm_ref[0], cm_ref[1], even)
    kb = _dot_split(sm_ref[0], sm_ref[1], odd)
    kb_ref[...] = jnp.where(first, nyq, kb)


FILT_TC = 256


def _filter_spectrum(L, z, w1p, b1, w2, b2, w3, freq, deltas, cm, sm):
    nb = D_HYENA // FILT_TC
    est = 2 * (2 * 2 * L * L * 2) + 2 * 4 * L * FILT_TC * 4 + 10 * L * FILT_TC * 4 + 4 * L * FILT_PAD * 4
    full = lambda shape: pl.BlockSpec(shape, lambda c: (0,) * len(shape))
    return pl.pallas_call(
        functools.partial(_filter_kernel, L=L),
        name=f"hyena_filter_{L}",
        grid=(nb,),
        in_specs=[
            full((L, FILT_PAD)), full((FILT_PAD, FILTER_ORDER)), full((1, FILTER_ORDER)),
            full((FILTER_ORDER, FILTER_ORDER)), full((1, FILTER_ORDER)),
            pl.BlockSpec((FILTER_ORDER, FILT_TC), lambda c: (0, c)),
            pl.BlockSpec((FILTER_ORDER, FILT_TC), lambda c: (0, nb + c)),
            full((1, FILTER_ORDER)),
            pl.BlockSpec((1, FILT_TC), lambda c: (0, c)),
            full((2, L, L)), full((2, L, L)),
        ],
        out_specs=[pl.BlockSpec((L, FILT_TC), lambda c: (0, c)),
                   pl.BlockSpec((L, FILT_TC), lambda c: (0, c))],
        out_shape=[jax.ShapeDtypeStruct((L, D_HYENA), F32)] * 2,
        compiler_params=pltpu.CompilerParams(
            dimension_semantics=("arbitrary",), vmem_limit_bytes=_vmem_limit(est)),
    )(z, w1p, b1, w2, b2, w3, w3, freq, deltas, cm, sm)


def _hyena_kernel(x0_ref, x1_ref, v_ref, cw0_ref, cw1_ref, cw2_ref, cb0_ref, cb1_ref, cb2_ref, hb_ref,
                  ka_ref, kb_ref, cm_ref, sm_ref, smt_ref, gn_ref, o_ref, acc_ref, *, L, tc, nct):
    ct = pl.program_id(1)
    pos = lax.broadcasted_iota(jnp.int32, (L, 1), 0)
    first = pos == 0
    x0 = _conv3_rows(x0_ref[...], cw0_ref[...], cb0_ref[...])
    x1 = _conv3_rows(x1_ref[...], cw1_ref[...], cb1_ref[...])
    v = _conv3_rows(v_ref[...], cw2_ref[...], cb2_ref[...])
    u = x1 * v
    ub = u.astype(BF16)
    top = jnp.dot(cm_ref[...], ub, preferred_element_type=F32)
    bot = jnp.dot(sm_ref[...], ub, preferred_element_type=F32)
    ka = ka_ref[...]
    kb = kb_ref[...]
    y_top = top * ka - jnp.where(first, 0.0, bot * kb)
    y_bot = jnp.where(first, bot * kb, top * kb + bot * ka)
    wgt = jnp.where(first, 1.0 / (2 * L), 2.0 / (2 * L))
    y = (jnp.dot(cm_ref[...], (y_top * wgt).astype(BF16), preferred_element_type=F32)
         + jnp.dot(smt_ref[...], (y_bot * wgt).astype(BF16), preferred_element_type=F32))
    acc_ref[ct] = x0 * (y + u * hb_ref[...])

    @pl.when(ct == nct - 1)
    def _():
        ss = jnp.zeros((L, 1), F32)
        for j in range(nct):
            a = acc_ref[j]
            ss = ss + jnp.sum(a * a, axis=-1, keepdims=True)
        inv = lax.rsqrt(ss * (1.0 / D_HYENA) + RMS_EPS)
        for j in range(nct):
            o_ref[:, j * tc:(j + 1) * tc] = (acc_ref[j] * inv * gn_ref[:, j * tc:(j + 1) * tc]).astype(o_ref.dtype)


def _into_rows(kernel_fn, dest):
    if dest is None:
        return kernel_fn, [], [], {}

    def body(dest_ref, *refs):
        del dest_ref
        kernel_fn(*refs)
    return body, [pl.BlockSpec(memory_space=pl.ANY)], [dest], {0: 0}


def _hyena(proj, conv_w_t, conv_b, hy_bias, ka, kb, cm, sm, smt, gn, *, L, tc, nseq, row_block0, dest=None):
    nct = D_HYENA // tc
    cpb = D_HYENA // tc
    est = (3 * L * L * 2 + 2 * (3 * L * tc * 4 + 2 * L * tc * 4) + nct * L * tc * 4
           + 2 * L * D_HYENA * 2 + 14 * L * tc * 4)
    once = pl.Buffered(1)
    grp = lambda g: pl.BlockSpec((L, tc), lambda s, c: (row_block0 + s, g * cpb + c))
    tap = lambda g: pl.BlockSpec((3, tc), lambda s, c: (0, g * cpb + c))
    bias = lambda g: pl.BlockSpec((1, tc), lambda s, c: (0, g * cpb + c))
    col = lambda rows: pl.BlockSpec((rows, tc), lambda s, c: (0, c))
    table = pl.BlockSpec((None, L, L), lambda s, c: (0, 0, 0), pipeline_mode=once)
    body, dest_specs, dest_args, aliases = _into_rows(functools.partial(_hyena_kernel, L=L, tc=tc, nct=nct), dest)
    return pl.pallas_call(
        body,
        name=f"hyena_conv_{L}",
        grid=(nseq, nct),
        in_specs=dest_specs + [grp(0), grp(1), grp(2), tap(0), tap(1), tap(2), bias(0), bias(1), bias(2),
                               col(1), col(L), col(L), table, table, table,
                               pl.BlockSpec((1, D_HYENA), lambda s, c: (0, 0))],
        out_specs=pl.BlockSpec((L, D_HYENA), lambda s, c: (row_block0 + s, 0)),
        out_shape=jax.ShapeDtypeStruct((N_TOK, D_HYENA), BF16),
        input_output_aliases=aliases,
        scratch_shapes=[pltpu.VMEM((nct, L, tc), F32)],
        compiler_params=pltpu.CompilerParams(
            dimension_semantics=("arbitrary", "arbitrary"), vmem_limit_bytes=_vmem_limit(est)),
    )(*dest_args, proj, proj, proj, conv_w_t, conv_w_t, conv_w_t, conv_b, conv_b, conv_b,
      hy_bias, ka, kb, cm, sm, smt, gn)


def _dot_nt(a, b):
    return lax.dot_general(a, b, (((1,), (1,)), ((), ())), preferred_element_type=F32)


def _head_lanes(kvh, g):
    hd = kvh * KV_GROUP + g
    return slice(hd * HEAD_DIM, (hd + 1) * HEAD_DIM)


def _group_sinks(sink_ref, kvh, rows):
    return jnp.concatenate([jnp.full((rows, 1), sink_ref[kvh * KV_GROUP + g], F32) for g in range(KV_GROUP)],
                           axis=0)


def _head_norm_store(acc_ref, gn_ref, o_ref):
    y = acc_ref[...]
    inv = lax.rsqrt(jnp.mean(y * y, axis=-1, keepdims=True) + RMS_EPS)
    o_ref[...] = (y * inv * gn_ref[...]).astype(o_ref.dtype)


def _attn_ctx_kernel(sink_ref, q_ref, k_ref, v_ref, gn_ref, o_ref, ko_ref, vo_ref, acc_ref):
    ko_ref[...] = k_ref[...]
    vo_ref[...] = v_ref[...]
    for kvh in range(N_KV_HEADS):
        ksl = slice(kvh * HEAD_DIM, (kvh + 1) * HEAD_DIM)
        kh = k_ref[:, ksl].astype(BF16)
        vh = v_ref[:, ksl].astype(BF16)
        for g in range(KV_GROUP):
            hsl = _head_lanes(kvh, g)
            s = _dot_nt(q_ref[:, hsl].astype(BF16), kh) * ATTN_SCALE
            sk = sink_ref[kvh * KV_GROUP + g]
            m = jnp.maximum(jnp.max(s, axis=-1, keepdims=True), sk)
            e = jnp.exp(s - m)
            den = jnp.sum(e, axis=-1, keepdims=True) + jnp.exp(sk - m)
            acc_ref[:, hsl] = jnp.dot((e / den).astype(BF16), vh, preferred_element_type=F32)
    _head_norm_store(acc_ref, gn_ref, o_ref)


def _attn_ctx(proj, sink, gn):
    est = 2 * (SEQ * D_ATTN * 4 + 4 * SEQ * KV_DIM * 4 + SEQ * D_ATTN * 2) + 3 * SEQ * D_ATTN * 4
    return pl.pallas_call(
        _attn_ctx_kernel,
        name="attn_ctx",
        grid=(BATCH,),
        in_specs=[
            pl.BlockSpec(memory_space=pltpu.SMEM),
            pl.BlockSpec((SEQ, D_ATTN), lambda b: (b, O_Q // D_ATTN)),
            pl.BlockSpec((SEQ, KV_DIM), lambda b: (b, O_K // KV_DIM)),
            pl.BlockSpec((SEQ, KV_DIM), lambda b: (b, O_V // KV_DIM)),
            pl.BlockSpec((1, D_ATTN), lambda b: (0, 0)),
        ],
        out_specs=[pl.BlockSpec((SEQ, D_ATTN), lambda b: (b, 0)),
                   pl.BlockSpec((SEQ, KV_DIM), lambda b: (b, 0)),
                   pl.BlockSpec((SEQ, KV_DIM), lambda b: (b, 0))],
        out_shape=[jax.ShapeDtypeStruct((N_TOK, D_ATTN), BF16),
                   jax.ShapeDtypeStruct((N_CTX, KV_DIM), F32),
                   jax.ShapeDtypeStruct((N_CTX, KV_DIM), F32)],
        scratch_shapes=[pltpu.VMEM((SEQ, D_ATTN), F32)],
        compiler_params=pltpu.CompilerParams(
            dimension_semantics=("arbitrary",), vmem_limit_bytes=_vmem_limit(est)),
    )(sink, proj, proj, proj, gn)


def _rope_tables():
    half = HEAD_DIM // 2
    quarter = half // 2
    pos = jnp.arange(DEC_SEQ, dtype=jnp.int32)
    row_pos = (pos // GRID_W).astype(F32)
    col_pos = (pos % GRID_W).astype(F32)
    inv = ROPE_BASE ** (-jnp.arange(quarter, dtype=F32) / quarter)
    lane = jnp.arange(HEAD_DIM, dtype=jnp.int32)
    p = jnp.where((lane < half)[None, :], row_pos[:, None], col_pos[:, None])
    ang = p * inv[lane % quarter][None, :]
    low = ((lane % half) < quarter)[None, :]
    cos = jnp.cos(ang)
    sin = jnp.sin(ang)
    return cos, jnp.where(low, -sin, 0.0), jnp.where(low, 0.0, sin)


ROPE_SHIFT = HEAD_DIM // 4
LAT_SPAN = BLOCK + 2 * WINDOW


def _rope(x, cos, sin_lo, sin_hi):
    return (x * cos + pltpu.roll(x, HEAD_DIM - ROPE_SHIFT, 1) * sin_lo
            + pltpu.roll(x, ROPE_SHIFT, 1) * sin_hi)


def _attn_lat_kernel(sink_ref, q_ref, k_ref, v_ref, ck_ref, cv_ref, cos_ref, slo_ref, shi_ref, gn_ref,
                     o_ref, acc_ref):
    i = pl.program_id(1)
    q0 = pl.multiple_of(i * BLOCK, BLOCK)
    k0 = pl.multiple_of(jnp.clip((i - 1) * BLOCK, 0, DEC_SEQ - LAT_SPAN), BLOCK)
    q_tab = (cos_ref[pl.ds(q0, BLOCK), :], slo_ref[pl.ds(q0, BLOCK), :], shi_ref[pl.ds(q0, BLOCK), :])
    k_tab = (cos_ref[pl.ds(k0, LAT_SPAN), :], slo_ref[pl.ds(k0, LAT_SPAN), :], shi_ref[pl.ds(k0, LAT_SPAN), :])
    q_pos = q0 + lax.broadcasted_iota(jnp.int32, (BLOCK, 1), 0)
    k_pos = k0 + lax.broadcasted_iota(jnp.int32, (1, LAT_SPAN), 1)
    valid = jnp.abs(q_pos - k_pos) <= WINDOW
    valid = jnp.concatenate([valid] * KV_GROUP, axis=0)
    for kvh in range(N_KV_HEADS):
        ksl = slice(kvh * HEAD_DIM, (kvh + 1) * HEAD_DIM)
        kw = _rope(k_ref[pl.ds(k0, LAT_SPAN), ksl], *k_tab).astype(BF16)
        vw = v_ref[pl.ds(k0, LAT_SPAN), ksl].astype(BF16)
        kc = ck_ref[0, :, ksl].astype(BF16)
        vc = cv_ref[0, :, ksl].astype(BF16)
        q = jnp.concatenate([_rope(q_ref[:, _head_lanes(kvh, g)], *q_tab) for g in range(KV_GROUP)],
                            axis=0).astype(BF16)
        sk = _group_sinks(sink_ref, kvh, BLOCK)
        s_lat = jnp.where(valid, _dot_nt(q, kw) * ATTN_SCALE, NEG_INF)
        s_ctx = _dot_nt(q, kc) * ATTN_SCALE
        m = jnp.maximum(jnp.maximum(jnp.max(s_lat, axis=-1, keepdims=True),
                                    jnp.max(s_ctx, axis=-1, keepdims=True)), sk)
        e_lat = jnp.exp(s_lat - m)
        e_ctx = jnp.exp(s_ctx - m)
        den = (jnp.sum(e_lat, axis=-1, keepdims=True) + jnp.sum(e_ctx, axis=-1, keepdims=True)
               + jnp.exp(sk - m))
        o = (jnp.dot((e_lat / den).astype(BF16), vw, preferred_element_type=F32)
             + jnp.dot((e_ctx / den).astype(BF16), vc, preferred_element_type=F32))
        for g in range(KV_GROUP):
            acc_ref[:, _head_lanes(kvh, g)] = o[g * BLOCK:(g + 1) * BLOCK]
    _head_norm_store(acc_ref, gn_ref, o_ref)


def _attn_lat(proj, cache_k, cache_v, sink, rope, gn, dest):
    nqb = DEC_SEQ // BLOCK
    est = (2 * (BLOCK * D_ATTN * 4 + 2 * DEC_SEQ * KV_DIM * 4 + 2 * PAST_LEN * KV_DIM * 4
                + 3 * DEC_SEQ * HEAD_DIM * 4 + BLOCK * D_ATTN * 2) + 8 * BLOCK * D_ATTN * 4)
    lat_row_block0 = N_CTX // BLOCK
    lat_seq_block0 = N_CTX // DEC_SEQ
    tab = pl.BlockSpec((DEC_SEQ, HEAD_DIM), lambda b, i: (0, 0))
    body, dest_specs, dest_args, aliases = _into_rows(_attn_lat_kernel, dest)
    return pl.pallas_call(
        body,
        name="attn_lat",
        grid=(DEC_BATCH, nqb),
        input_output_aliases=aliases,
        in_specs=dest_specs + [
            pl.BlockSpec(memory_space=pltpu.SMEM),
            pl.BlockSpec((BLOCK, D_ATTN), lambda b, i: (lat_row_block0 + b * nqb + i, O_Q // D_ATTN)),
            pl.BlockSpec((DEC_SEQ, KV_DIM), lambda b, i: (lat_seq_block0 + b, O_K // KV_DIM)),
            pl.BlockSpec((DEC_SEQ, KV_DIM), lambda b, i: (lat_seq_block0 + b, O_V // KV_DIM)),
            pl.BlockSpec((1, PAST_LEN, KV_DIM), lambda b, i: (b, 0, 0)),
            pl.BlockSpec((1, PAST_LEN, KV_DIM), lambda b, i: (b, 0, 0)),
            tab, tab, tab,
            pl.BlockSpec((1, D_ATTN), lambda b, i: (0, 0)),
        ],
        out_specs=pl.BlockSpec((BLOCK, D_ATTN), lambda b, i: (lat_row_block0 + b * nqb + i, 0)),
        out_shape=jax.ShapeDtypeStruct((N_TOK, D_ATTN), BF16),
        scratch_shapes=[pltpu.VMEM((BLOCK, D_ATTN), F32)],
        compiler_params=pltpu.CompilerParams(
            dimension_semantics=("arbitrary", "arbitrary"), vmem_limit_bytes=_vmem_limit(est)),
    )(*dest_args, sink, proj, proj, proj, cache_k, cache_v, *rope, gn)


def kernel(x_prompt, x_sample, cache_k, cache_v, c, c_ctx, w_ada, b_ada, norm_mix, w_in, hy_conv_w, hy_conv_b, filt_w1, filt_b1, filt_w2, filt_b2, filt_w3, filt_freq, hy_bias, attn_sink, gn_hyena, gn_attn, w_out, norm_ffn, w_up, ffn_conv_w, ffn_conv_b, w_down, final_norm):
    x = jnp.concatenate([x_prompt.reshape(N_CTX, D_MODEL), x_sample.reshape(N_LAT, D_MODEL)], axis=0)
    cond = jnp.concatenate([c_ctx[None, :], c, jnp.zeros((MOD_ROWS - 1 - DEC_BATCH, D_MODEL), F32)], axis=0)
    mod_all = _modulation(cond, w_ada, b_ada)

    rope = _rope_tables()
    tables = {L: _dft_tables(L) for L in (SEQ, DEC_SEQ)}
    feats = {L: _filter_features(L) for L in (SEQ, DEC_SEQ)}
    deltas = jnp.linspace(MIN_DECAY, MAX_DECAY, D_HYENA, dtype=F32)[None, :]

    w_down_bf16 = w_down.astype(BF16)
    new_k = []
    new_v = []
    for i in range(DEPTH):
        mod = mod_all[i]
        h = _prenorm(x, norm_mix[i], mod, 0, 1)
        proj = _matmul(h, w_in, i, 1024, 1024)

        hy_w_t = hy_conv_w[i].T
        hy_b = hy_conv_b[i][None, :]
        w1p = jnp.pad(filt_w1[i], ((0, FILT_PAD - FILTER_EMB), (0, 0)))
        y_a = None
        for L, tc, nseq, rb0 in ((SEQ, 1024, BATCH, 0), (DEC_SEQ, 256, DEC_BATCH, N_CTX // DEC_SEQ)):
            cm, sm, smt = tables[L]
            ka, kb = _filter_spectrum(L, feats[L], w1p, filt_b1[i][None, :], filt_w2[i], filt_b2[i][None, :],
                                      filt_w3[i], filt_freq[i][None, :], deltas, cm, sm)
            y_a = _hyena(proj, hy_w_t, hy_b, hy_bias[i][None, :], ka, kb, cm, sm, smt,
                         gn_hyena[i][None, :], L=L, tc=tc, nseq=nseq, row_block0=rb0, dest=y_a)
        gn_a = gn_attn[i][None, :]
        y_b, k_ctx, v_ctx = _attn_ctx(proj, attn_sink[i], gn_a)
        y_b = _attn_lat(proj, cache_k[:, i].reshape(DEC_BATCH, PAST_LEN, KV_DIM),
                        cache_v[:, i].reshape(DEC_BATCH, PAST_LEN, KV_DIM), attn_sink[i], rope, gn_a, y_b)
        x = _matmul_residual([y_a, y_b], w_out, i, x, mod, 2, 512, 1024, "out_proj")

        h = _prenorm(x, norm_ffn[i], mod, 3, 4)
        act = _ffn_up(h, w_up, i, ffn_conv_w[i].T, ffn_conv_b[i][None, :])
        x = _matmul_residual([act], w_down_bf16, i, x, mod, 5, 512, 512, "ffn_down")

        new_k.append(k_ctx.reshape(BATCH, SEQ, N_KV_HEADS, HEAD_DIM))
        new_v.append(v_ctx.reshape(BATCH, SEQ, N_KV_HEADS, HEAD_DIM))

    y_prompt = _final_norm(x, final_norm, 0, N_CTX).reshape(BATCH, SEQ, D_MODEL)
    y_sample = _final_norm(x, final_norm, N_CTX // NORM_TR, N_LAT).reshape(DEC_BATCH, DEC_SEQ, D_MODEL)
    return (y_prompt, y_sample, jnp.stack(new_k, axis=1), jnp.stack(new_v, axis=1))
```

```python
import functools
import math

import jax
import jax.numpy as jnp
from jax import lax
from jax.experimental import pallas as pl
from jax.experimental.pallas import tpu as pltpu

F32 = jnp.float32
BF16 = jnp.bfloat16
HIGHEST = lax.Precision.HIGHEST

D_MODEL = 4096
BATCH = 32
SEQ = 256
DEPTH = 4
DEC_BATCH = 2
DEC_SEQ = 1024
PAST_LEN = 512
GRID_W = 64
D_HYENA = D_MODEL // 2
D_ATTN = D_MODEL - D_HYENA
HEAD_DIM = 128
N_HEADS = D_ATTN // HEAD_DIM
N_KV_HEADS = 4
KV_GROUP = N_HEADS // N_KV_HEADS
KV_DIM = N_KV_HEADS * HEAD_DIM
WINDOW = 128
BLOCK = 128
ATTN_SCALE = 1.0 / math.sqrt(HEAD_DIM)
ROPE_BASE = 10000.0
FILTER_EMB = 33
FILTER_BANDS = (FILTER_EMB - 1) // 2
FILTER_ORDER = 64
DECAY_TARGET = 1e-2
MAX_DECAY = math.log(DECAY_TARGET) / 0.3
MIN_DECAY = math.log(DECAY_TARGET) / 1.5
D_FF = 11008
RMS_EPS = 1e-6
NEG_INF = -1e30
O_Q = 3 * D_HYENA
O_K = O_Q + D_ATTN
O_V = O_K + KV_DIM
D_IN = O_V + KV_DIM

N_CTX = BATCH * SEQ
N_LAT = DEC_BATCH * DEC_SEQ
N_TOK = N_CTX + N_LAT
MOD_ROWS = 8
N_MOD = 6 * D_MODEL

V7X_VMEM_BYTES = 64 * 1024 * 1024
V7X_VMEM_USABLE = 58 * 1024 * 1024
MIB = 1024 * 1024


def _vmem_limit(estimate_bytes):
    return int(min(V7X_VMEM_USABLE, estimate_bytes + estimate_bytes // 4 + 2 * MIB))


def _mod_row(row_start):
    return jnp.where(row_start < N_CTX, 0, 1 + (row_start - N_CTX) // DEC_SEQ)


def _seq_len(row_start):
    return jnp.where(row_start < N_CTX, SEQ, DEC_SEQ)


SUBLANES = 8


def _zero_rows_at(x, sub, breaks):
    pick = lax.broadcasted_iota(jnp.int32, (SUBLANES, 1), 0) == sub
    pieces = []
    at = 0
    for start in sorted(breaks):
        if start > at:
            pieces.append(x[at:start])
        cond = pick if breaks[start] is True else jnp.logical_and(pick, breaks[start])
        pieces.append(jnp.where(cond, 0.0, x[start:start + SUBLANES]))
        at = start + SUBLANES
    if at < x.shape[0]:
        pieces.append(x[at:])
    return jnp.concatenate(pieces, axis=0)


def _conv3_rows(u, w, b, seq_rows=None, split=False):
    rows = u.shape[0]
    seq_rows = rows if seq_rows is None else seq_rows
    starts = range(0, rows, seq_rows)
    up = _zero_rows_at(pltpu.roll(u, 1, 0), 0, {s: (True if s == 0 else split) for s in starts})
    un = _zero_rows_at(pltpu.roll(u, rows - 1, 0), SUBLANES - 1,
                       {s + seq_rows - SUBLANES: (True if s + seq_rows == rows else split) for s in starts})
    return up * w[0:1, :] + u * w[1:2, :] + un * w[2:3, :] + b


MOD_TN = 512


def _mod_kernel(c_ref, w_ref, b_ref, o_ref):
    c = c_ref[...]
    s = (c * jax.nn.sigmoid(c)).astype(BF16)
    o_ref[0] = jnp.dot(s, w_ref[0].astype(BF16), preferred_element_type=F32) + b_ref[0]


def _modulation(cond, w_ada, b_ada):
    est = 2 * D_MODEL * MOD_TN * 4 + D_MODEL * MOD_TN * 2 + 4 * MOD_ROWS * D_MODEL * 4
    return pl.pallas_call(
        _mod_kernel,
        name="adaln_mod",
        grid=(DEPTH, N_MOD // MOD_TN),
        in_specs=[
            pl.BlockSpec((MOD_ROWS, D_MODEL), lambda l, n: (0, 0)),
            pl.BlockSpec((1, D_MODEL, MOD_TN), lambda l, n: (l, 0, n)),
            pl.BlockSpec((1, 1, MOD_TN), lambda l, n: (l, 0, n)),
        ],
        out_specs=pl.BlockSpec((1, MOD_ROWS, MOD_TN), lambda l, n: (l, 0, n)),
        out_shape=jax.ShapeDtypeStruct((DEPTH, MOD_ROWS, N_MOD), F32),
        compiler_params=pltpu.CompilerParams(
            dimension_semantics=("arbitrary", "arbitrary"), vmem_limit_bytes=_vmem_limit(est)),
    )(cond, w_ada, b_ada.reshape(DEPTH, 1, N_MOD))


NORM_TR = 256


def _prenorm_kernel(x_ref, g_ref, shift_ref, scale_ref, o_ref):
    r = _mod_row(pl.program_id(0) * NORM_TR)
    x = x_ref[...]
    y = x * lax.rsqrt(jnp.mean(x * x, axis=-1, keepdims=True) + RMS_EPS) * g_ref[...]
    o_ref[...] = (y * (1.0 + scale_ref[pl.ds(r, 1), :]) + shift_ref[pl.ds(r, 1), :]).astype(o_ref.dtype)


def _prenorm(x, g, mod, shift_chunk, scale_chunk):
    est = 2 * NORM_TR * D_MODEL * (4 + 2) + 3 * NORM_TR * D_MODEL * 4
    return pl.pallas_call(
        _prenorm_kernel,
        name="prenorm",
        grid=(N_TOK // NORM_TR,),
        in_specs=[
            pl.BlockSpec((NORM_TR, D_MODEL), lambda i: (i, 0)),
            pl.BlockSpec((1, D_MODEL), lambda i: (0, 0)),
            pl.BlockSpec((MOD_ROWS, D_MODEL), lambda i: (0, shift_chunk)),
            pl.BlockSpec((MOD_ROWS, D_MODEL), lambda i: (0, scale_chunk)),
        ],
        out_specs=pl.BlockSpec((NORM_TR, D_MODEL), lambda i: (i, 0)),
        out_shape=jax.ShapeDtypeStruct((N_TOK, D_MODEL), BF16),
        compiler_params=pltpu.CompilerParams(
            dimension_semantics=("arbitrary",), vmem_limit_bytes=_vmem_limit(est)),
    )(x, g.reshape(1, D_MODEL), mod, mod)


def _final_norm_kernel(x_ref, g_ref, o_ref):
    x = x_ref[...]
    o_ref[...] = x * lax.rsqrt(jnp.mean(x * x, axis=-1, keepdims=True) + RMS_EPS) * g_ref[...]


def _final_norm(x, g, row_block0, rows):
    est = 4 * NORM_TR * D_MODEL * 4 + 2 * NORM_TR * D_MODEL * 4
    return pl.pallas_call(
        _final_norm_kernel,
        name="final_norm",
        grid=(rows // NORM_TR,),
        in_specs=[
            pl.BlockSpec((NORM_TR, D_MODEL), lambda i: (row_block0 + i, 0)),
            pl.BlockSpec((1, D_MODEL), lambda i: (0, 0)),
        ],
        out_specs=pl.BlockSpec((NORM_TR, D_MODEL), lambda i: (i, 0)),
        out_shape=jax.ShapeDtypeStruct((rows, D_MODEL), F32),
        compiler_params=pltpu.CompilerParams(
            dimension_semantics=("arbitrary",), vmem_limit_bytes=_vmem_limit(est)),
    )(x, g.reshape(1, D_MODEL))


def _weight_tile(w_ref, wb_ref):
    if wb_ref is None:
        return w_ref

    @pl.when(pl.program_id(1) == 0)
    def _():
        wb_ref[...] = w_ref[...].astype(BF16)
    return wb_ref


def _matmul_kernel(x_ref, w_ref, o_ref, wb_ref):
    w = _weight_tile(w_ref, wb_ref)
    o_ref[...] = jnp.dot(x_ref[...], w[...], preferred_element_type=F32)


def _weight_spec(w_stack, layer, tn, single_buffer):
    k = w_stack.shape[1]
    mode = {"pipeline_mode": pl.Buffered(1)} if single_buffer else {}
    return pl.BlockSpec((None, k, tn), lambda j, i: (layer, 0, j), **mode)


def _matmul(x, w_stack, layer, tm, tn):
    m, k = x.shape
    n = w_stack.shape[2]
    est = 2 * (tm * k * 2 + tm * tn * 4) + k * tn * 4 + k * tn * 2 + tm * tn * 4
    return pl.pallas_call(
        _matmul_kernel,
        name="in_proj",
        grid=(n // tn, m // tm),
        in_specs=[
            pl.BlockSpec((tm, k), lambda j, i: (i, 0)),
            _weight_spec(w_stack, layer, tn, single_buffer=True),
        ],
        out_specs=pl.BlockSpec((tm, tn), lambda j, i: (i, j)),
        out_shape=jax.ShapeDtypeStruct((m, n), F32),
        scratch_shapes=[pltpu.VMEM((k, tn), BF16)],
        compiler_params=pltpu.CompilerParams(
            dimension_semantics=("arbitrary", "arbitrary"), vmem_limit_bytes=_vmem_limit(est)),
    )(x, w_stack)


def _matmul_residual_kernel(*refs, n_lhs, tm, cast_weight):
    lhs = refs[:n_lhs]
    w_ref, res_ref, gate_ref, o_ref = refs[n_lhs:n_lhs + 4]
    w = _weight_tile(w_ref, refs[n_lhs + 4] if cast_weight else None)
    r = _mod_row(pl.program_id(1) * tm)
    y = None
    k0 = 0
    for x_ref in lhs:
        kk = x_ref.shape[1]
        part = jnp.dot(x_ref[...], w[k0:k0 + kk, :], preferred_element_type=F32)
        y = part if y is None else y + part
        k0 += kk
    o_ref[...] = res_ref[...] + gate_ref[pl.ds(r, 1), :] * y


def _matmul_residual(lhs, w_stack, layer, res, mod, gate_chunk, tm, tn, name):
    m = res.shape[0]
    _, k, n = w_stack.shape
    cast_weight = w_stack.dtype != BF16
    w_bytes = k * tn * 4 + k * tn * 2 if cast_weight else 2 * k * tn * 2
    est = 2 * (tm * k * 2 + 2 * tm * tn * 4 + MOD_ROWS * tn * 4) + w_bytes + 2 * tm * tn * 4
    gate_block0 = gate_chunk * (D_MODEL // tn)
    return pl.pallas_call(
        functools.partial(_matmul_residual_kernel, n_lhs=len(lhs), tm=tm, cast_weight=cast_weight),
        name=name,
        grid=(n // tn, m // tm),
        in_specs=[pl.BlockSpec((tm, x.shape[1]), lambda j, i: (i, 0)) for x in lhs] + [
            _weight_spec(w_stack, layer, tn, single_buffer=cast_weight),
            pl.BlockSpec((tm, tn), lambda j, i: (i, j)),
            pl.BlockSpec((MOD_ROWS, tn), lambda j, i: (0, gate_block0 + j)),
        ],
        out_specs=pl.BlockSpec((tm, tn), lambda j, i: (i, j)),
        out_shape=jax.ShapeDtypeStruct((m, n), F32),
        scratch_shapes=[pltpu.VMEM((k, tn), BF16)] if cast_weight else [],
        compiler_params=pltpu.CompilerParams(
            dimension_semantics=("arbitrary", "arbitrary"), vmem_limit_bytes=_vmem_limit(est)),
    )(*lhs, w_stack, res, mod)


FFN_TM = 1024
FFN_TN = 256


def _ffn_up_kernel(h_ref, wg_ref, wv_ref, cwg_ref, cwv_ref, cbg_ref, cbv_ref, o_ref, wb_ref):
    @pl.when(pl.program_id(1) == 0)
    def _():
        wb_ref[:, :FFN_TN] = wg_ref[...].astype(BF16)
        wb_ref[:, FFN_TN:] = wv_ref[...].astype(BF16)

    u = jnp.dot(h_ref[...], wb_ref[...], preferred_element_type=F32)
    is_ctx = pl.program_id(1) * FFN_TM < N_CTX
    g = _conv3_rows(u[:, :FFN_TN], cwg_ref[...], cbg_ref[...], SEQ, is_ctx)
    v = _conv3_rows(u[:, FFN_TN:], cwv_ref[...], cbv_ref[...], SEQ, is_ctx)
    o_ref[...] = (g * jax.nn.sigmoid(g) * v).astype(o_ref.dtype)


FFN_PAIR = 2 * FFN_TN


def _ffn_up2_kernel(h_ref, wg_ref, wva_ref, wvb_ref, cwg_ref, cwva_ref, cwvb_ref, cbg_ref, cbva_ref, cbvb_ref,
                    o_ref, wb_ref):
    @pl.when(pl.program_id(1) == 0)
    def _():
        wb_ref[:, :FFN_PAIR] = wg_ref[...].astype(BF16)
        wb_ref[:, FFN_PAIR:FFN_PAIR + FFN_TN] = wva_ref[...].astype(BF16)
        wb_ref[:, FFN_PAIR + FFN_TN:] = wvb_ref[...].astype(BF16)

    u = jnp.dot(h_ref[...], wb_ref[...], preferred_element_type=F32)
    is_ctx = pl.program_id(1) * FFN_TM < N_CTX
    cwg = cwg_ref[...]
    cbg = cbg_ref[...]
    for blk, (cwv_ref, cbv_ref) in enumerate(((cwva_ref, cbva_ref), (cwvb_ref, cbvb_ref))):
        lanes = slice(blk * FFN_TN, (blk + 1) * FFN_TN)
        g = _conv3_rows(u[:, lanes], cwg[:, lanes], cbg[:, lanes], SEQ, is_ctx)
        v = _conv3_rows(u[:, FFN_PAIR + blk * FFN_TN:FFN_PAIR + (blk + 1) * FFN_TN], cwv_ref[...], cbv_ref[...],
                        SEQ, is_ctx)
        o_ref[:, lanes] = (g * jax.nn.sigmoid(g) * v).astype(o_ref.dtype)


def _ffn_up2(h, w_up, layer, conv_w_t, conv_b):
    nb = D_FF // FFN_TN
    npair = nb // 2
    est = (2 * (FFN_TM * D_MODEL * 2 + FFN_TM * FFN_PAIR * 2) + 2 * D_MODEL * FFN_PAIR * 4
           + 2 * D_MODEL * FFN_PAIR * 2 + 12 * FFN_TM * FFN_PAIR * 4)
    once = pl.Buffered(1)
    return pl.pallas_call(
        _ffn_up2_kernel,
        name="ffn_up2",
        grid=(npair, N_TOK // FFN_TM),
        in_specs=[
            pl.BlockSpec((FFN_TM, D_MODEL), lambda j, i: (i, 0)),
            pl.BlockSpec((None, D_MODEL, FFN_PAIR), lambda j, i: (layer, 0, j), pipeline_mode=once),
            pl.BlockSpec((None, D_MODEL, FFN_TN), lambda j, i: (layer, 0, nb + 2 * j), pipeline_mode=once),
            pl.BlockSpec((None, D_MODEL, FFN_TN), lambda j, i: (layer, 0, nb + 2 * j + 1), pipeline_mode=once),
            pl.BlockSpec((3, FFN_PAIR), lambda j, i: (0, j)),
            pl.BlockSpec((3, FFN_TN), lambda j, i: (0, nb + 2 * j)),
            pl.BlockSpec((3, FFN_TN), lambda j, i: (0, nb + 2 * j + 1)),
            pl.BlockSpec((1, FFN_PAIR), lambda j, i: (0, j)),
            pl.BlockSpec((1, FFN_TN), lambda j, i: (0, nb + 2 * j)),
            pl.BlockSpec((1, FFN_TN), lambda j, i: (0, nb + 2 * j + 1)),
        ],
        out_specs=pl.BlockSpec((FFN_TM, FFN_PAIR), lambda j, i: (i, j)),
        out_shape=jax.ShapeDtypeStruct((N_TOK, npair * FFN_PAIR), BF16),
        scratch_shapes=[pltpu.VMEM((D_MODEL, 2 * FFN_PAIR), BF16)],
        compiler_params=pltpu.CompilerParams(
            dimension_semantics=("arbitrary", "arbitrary"), vmem_limit_bytes=_vmem_limit(est)),
    )(h, w_up, w_up, w_up, conv_w_t, conv_w_t, conv_w_t, conv_b, conv_b, conv_b)


def _ffn_up(h, w_up, layer, conv_w_t, conv_b, block0, n_blocks):
    nb = D_FF // FFN_TN
    est = (2 * (FFN_TM * D_MODEL * 2 + 2 * D_MODEL * FFN_TN * 4 + FFN_TM * FFN_TN * 2)
           + 2 * D_MODEL * FFN_TN * 2 + 10 * FFN_TM * FFN_TN * 4)
    return pl.pallas_call(
        _ffn_up_kernel,
        name="ffn_up",
        grid=(n_blocks, N_TOK // FFN_TM),
        in_specs=[
            pl.BlockSpec((FFN_TM, D_MODEL), lambda j, i: (i, 0)),
            pl.BlockSpec((None, D_MODEL, FFN_TN), lambda j, i: (layer, 0, block0 + j)),
            pl.BlockSpec((None, D_MODEL, FFN_TN), lambda j, i: (layer, 0, nb + block0 + j)),
            pl.BlockSpec((3, FFN_TN), lambda j, i: (0, block0 + j)),
            pl.BlockSpec((3, FFN_TN), lambda j, i: (0, nb + block0 + j)),
            pl.BlockSpec((1, FFN_TN), lambda j, i: (0, block0 + j)),
            pl.BlockSpec((1, FFN_TN), lambda j, i: (0, nb + block0 + j)),
        ],
        out_specs=pl.BlockSpec((FFN_TM, FFN_TN), lambda j, i: (i, j)),
        out_shape=jax.ShapeDtypeStruct((N_TOK, n_blocks * FFN_TN), BF16),
        scratch_shapes=[pltpu.VMEM((D_MODEL, 2 * FFN_TN), BF16)],
        compiler_params=pltpu.CompilerParams(
            dimension_semantics=("arbitrary", "arbitrary"), vmem_limit_bytes=_vmem_limit(est)),
    )(h, w_up, w_up, conv_w_t, conv_w_t, conv_b, conv_b)


FILT_PAD = 64


def _split_bf16(a):
    hi = a.astype(BF16)
    return hi, (a - hi.astype(F32)).astype(BF16)


def _dot_split(t_hi, t_lo, b):
    b_hi, b_lo = _split_bf16(b)
    return (jnp.dot(t_hi, b_hi, preferred_element_type=F32)
            + jnp.dot(t_hi, b_lo, preferred_element_type=F32)
            + jnp.dot(t_lo, b_hi, preferred_element_type=F32))


def _dft_tables(L):
    f = jnp.arange(L, dtype=jnp.int32)[:, None]
    s = jnp.arange(L, dtype=jnp.int32)[None, :]
    ang = ((f * s) % (2 * L)).astype(F32) * (math.pi / L)
    cm = jnp.cos(ang)
    nyq = jnp.where(s % 2 == 0, 1.0, -1.0).astype(F32)
    sm = jnp.where(f == 0, nyq, -jnp.sin(ang))
    return tuple(jnp.stack(_split_bf16(t)) for t in (cm, sm, sm.T))


def _filter_features(L):
    t = jnp.linspace(0.0, 1.0, L, dtype=F32)[:, None]
    w = 2.0 * math.pi * jnp.arange(L, dtype=F32)[:, None] / L
    f = jnp.linspace(1e-4, FILTER_BANDS - 1, FILTER_BANDS, dtype=F32)[None, :]
    z = jnp.concatenate([t, jnp.cos(f * w), -jnp.sin(f * w)], axis=-1)
    return jnp.pad(z, ((0, 0), (0, FILT_PAD - FILTER_EMB)))


def _filter_kernel(z_ref, w1_ref, b1_ref, w2_ref, b2_ref, w3f_ref, w3b_ref, freq_ref, delta_ref,
                   cm_ref, sm_ref, ka_ref, kb_ref, *, L):
    fr = freq_ref[...]
    z = z_ref[...]
    h = jnp.sin(fr * (jnp.dot(z, w1_ref[...], precision=HIGHEST, preferred_element_type=F32) + b1_ref[...]))
    h = jnp.sin(fr * (jnp.dot(h, w2_ref[...], precision=HIGHEST, preferred_element_type=F32) + b2_ref[...]))
    decay = jnp.exp(-z[:, 0:1] * jnp.abs(delta_ref[...]))
    h_fwd = jnp.dot(h, w3f_ref[...], precision=HIGHEST, preferred_element_type=F32) * decay
    h_bwd = jnp.dot(h, w3b_ref[...], precision=HIGHEST, preferred_element_type=F32) * decay
    pos = lax.broadcasted_iota(jnp.int32, (L, 1), 0)
    first = pos == 0
    h_bwd = jnp.where(first, 0.0, h_bwd)
    even = h_fwd + h_bwd
    odd = h_fwd - h_bwd
    sign = jnp.where((pos & 1) == 0, 1.0, -1.0)
    nyq = jnp.sum(sign * even, axis=0, keepdims=True)
    ka_ref[...] = _dot_split(cm_ref[0], cm_ref[1], even)
    kb = _dot_split(sm_ref[0], sm_ref[1], odd)
    kb_ref[...] = jnp.where(first, nyq, kb)


FILT_TC = 256


def _filter_spectrum(L, z, w1p, b1, w2, b2, w3, freq, deltas, cm, sm):
    nb = D_HYENA // FILT_TC
    est = 2 * (2 * 2 * L * L * 2) + 2 * 4 * L * FILT_TC * 4 + 10 * L * FILT_TC * 4 + 4 * L * FILT_PAD * 4
    full = lambda shape: pl.BlockSpec(shape, lambda c: (0,) * len(shape))
    return pl.pallas_call(
        functools.partial(_filter_kernel, L=L),
        name=f"hyena_filter_{L}",
        grid=(nb,),
        in_specs=[
            full((L, FILT_PAD)), full((FILT_PAD, FILTER_ORDER)), full((1, FILTER_ORDER)),
            full((FILTER_ORDER, FILTER_ORDER)), full((1, FILTER_ORDER)),
            pl.BlockSpec((FILTER_ORDER, FILT_TC), lambda c: (0, c)),
            pl.BlockSpec((FILTER_ORDER, FILT_TC), lambda c: (0, nb + c)),
            full((1, FILTER_ORDER)),
            pl.BlockSpec((1, FILT_TC), lambda c: (0, c)),
            full((2, L, L)), full((2, L, L)),
        ],
        out_specs=[pl.BlockSpec((L, FILT_TC), lambda c: (0, c)),
                   pl.BlockSpec((L, FILT_TC), lambda c: (0, c))],
        out_shape=[jax.ShapeDtypeStruct((L, D_HYENA), F32)] * 2,
        compiler_params=pltpu.CompilerParams(
            dimension_semantics=("arbitrary",), vmem_limit_bytes=_vmem_limit(est)),
    )(z, w1p, b1, w2, b2, w3, w3, freq, deltas, cm, sm)


def _hyena_kernel(x0_ref, x1_ref, v_ref, cw0_ref, cw1_ref, cw2_ref, cb0_ref, cb1_ref, cb2_ref, hb_ref,
                  ka_ref, kb_ref, cm_ref, sm_ref, smt_ref, gn_ref, o_ref, acc_ref, *, L, tc, nct, grid_ids=None):
    ct = pl.program_id(1) if grid_ids is None else grid_ids[1]
    pos = lax.broadcasted_iota(jnp.int32, (L, 1), 0)
    first = pos == 0
    x0 = _conv3_rows(x0_ref[...], cw0_ref[...], cb0_ref[...])
    x1 = _conv3_rows(x1_ref[...], cw1_ref[...], cb1_ref[...])
    v = _conv3_rows(v_ref[...], cw2_ref[...], cb2_ref[...])
    u = x1 * v
    ub = u.astype(BF16)
    top = jnp.dot(cm_ref[...], ub, preferred_element_type=F32)
    bot = jnp.dot(sm_ref[...], ub, preferred_element_type=F32)
    ka = ka_ref[...]
    kb = kb_ref[...]
    y_top = top * ka - jnp.where(first, 0.0, bot * kb)
    y_bot = jnp.where(first, bot * kb, top * kb + bot * ka)
    wgt = jnp.where(first, 1.0 / (2 * L), 2.0 / (2 * L))
    y = (jnp.dot(cm_ref[...], (y_top * wgt).astype(BF16), preferred_element_type=F32)
         + jnp.dot(smt_ref[...], (y_bot * wgt).astype(BF16), preferred_element_type=F32))
    acc_ref[ct] = x0 * (y + u * hb_ref[...])

    @pl.when(ct == nct - 1)
    def _():
        ss = jnp.zeros((L, 1), F32)
        for j in range(nct):
            a = acc_ref[j]
            ss = ss + jnp.sum(a * a, axis=-1, keepdims=True)
        inv = lax.rsqrt(ss * (1.0 / D_HYENA) + RMS_EPS)
        for j in range(nct):
            o_ref[:, j * tc:(j + 1) * tc] = (acc_ref[j] * inv * gn_ref[:, j * tc:(j + 1) * tc]).astype(o_ref.dtype)


def _with_fill(kernel_fn, n_real, out_pos, n_axes):
    def body(*refs):
        grid_ids = tuple(pl.program_id(a) for a in range(n_axes))

        @pl.when(grid_ids[0] < n_real)
        def _():
            kernel_fn(*refs, grid_ids=grid_ids)

        @pl.when(grid_ids[0] >= n_real)
        def _():
            refs[out_pos][...] = jnp.zeros_like(refs[out_pos])
    return body


def _into_rows(kernel_fn, dest):
    if dest is None:
        return kernel_fn, [], [], {}

    def body(dest_ref, *refs):
        del dest_ref
        kernel_fn(*refs)
    return body, [pl.BlockSpec(memory_space=pl.ANY)], [dest], {0: 0}


def _hyena(proj, conv_w_t, conv_b, hy_bias, ka, kb, cm, sm, smt, gn, *, L, tc, nseq, row_block0, dest=None):
    nct = D_HYENA // tc
    cpb = D_HYENA // tc
    est = (3 * L * L * 2 + 2 * (3 * L * tc * 4 + 2 * L * tc * 4) + nct * L * tc * 4
           + 2 * L * D_HYENA * 2 + 14 * L * tc * 4)
    once = pl.Buffered(1)
    n_fill = N_TOK // L - row_block0 - nseq if dest is None else 0
    grp = lambda g: pl.BlockSpec((L, tc), lambda s, c: (row_block0 + jnp.minimum(s, nseq - 1), g * cpb + c))
    tap = lambda g: pl.BlockSpec((3, tc), lambda s, c: (0, g * cpb + c))
    bias = lambda g: pl.BlockSpec((1, tc), lambda s, c: (0, g * cpb + c))
    col = lambda rows: pl.BlockSpec((rows, tc), lambda s, c: (0, c))
    table = pl.BlockSpec((None, L, L), lambda s, c: (0, 0, 0), pipeline_mode=once)
    kernel_fn = functools.partial(_hyena_kernel, L=L, tc=tc, nct=nct)
    if n_fill:
        kernel_fn = _with_fill(kernel_fn, nseq, -2, 2)
    body, dest_specs, dest_args, aliases = _into_rows(kernel_fn, dest)
    return pl.pallas_call(
        body,
        name=f"hyena_conv_{L}",
        grid=(nseq + n_fill, nct),
        in_specs=dest_specs + [grp(0), grp(1), grp(2), tap(0), tap(1), tap(2), bias(0), bias(1), bias(2),
                               col(1), col(L), col(L), table, table, table,
                               pl.BlockSpec((1, D_HYENA), lambda s, c: (0, 0))],
        out_specs=pl.BlockSpec((L, D_HYENA), lambda s, c: (row_block0 + s, 0)),
        out_shape=jax.ShapeDtypeStruct((N_TOK, D_HYENA), BF16),
        input_output_aliases=aliases,
        scratch_shapes=[pltpu.VMEM((nct, L, tc), F32)],
        compiler_params=pltpu.CompilerParams(
            dimension_semantics=("arbitrary", "arbitrary"), vmem_limit_bytes=_vmem_limit(est)),
    )(*dest_args, proj, proj, proj, conv_w_t, conv_w_t, conv_w_t, conv_b, conv_b, conv_b,
      hy_bias, ka, kb, cm, sm, smt, gn)


def _dot_nt(a, b):
    return lax.dot_general(a, b, (((1,), (1,)), ((), ())), preferred_element_type=F32)


def _head_lanes(kvh, g):
    hd = kvh * KV_GROUP + g
    return slice(hd * HEAD_DIM, (hd + 1) * HEAD_DIM)


def _group_sinks(sink_ref, kvh, rows):
    return jnp.concatenate([jnp.full((rows, 1), sink_ref[kvh * KV_GROUP + g], F32) for g in range(KV_GROUP)],
                           axis=0)


def _head_norm_store(acc_ref, gn_ref, o_ref):
    y = acc_ref[...]
    inv = lax.rsqrt(jnp.mean(y * y, axis=-1, keepdims=True) + RMS_EPS)
    o_ref[...] = (y * inv * gn_ref[...]).astype(o_ref.dtype)


def _attn_ctx_kernel(sink_ref, q_ref, k_ref, v_ref, gn_ref, o_ref, ko_ref, vo_ref, acc_ref, grid_ids=None):
    del grid_ids
    ko_ref[...] = k_ref[...]
    vo_ref[...] = v_ref[...]
    for kvh in range(N_KV_HEADS):
        ksl = slice(kvh * HEAD_DIM, (kvh + 1) * HEAD_DIM)
        kh = k_ref[:, ksl].astype(BF16)
        vh = v_ref[:, ksl].astype(BF16)
        for g in range(KV_GROUP):
            hsl = _head_lanes(kvh, g)
            s = _dot_nt(q_ref[:, hsl].astype(BF16), kh) * ATTN_SCALE
            sk = sink_ref[kvh * KV_GROUP + g]
            m = jnp.maximum(jnp.max(s, axis=-1, keepdims=True), sk)
            e = jnp.exp(s - m)
            den = jnp.sum(e, axis=-1, keepdims=True) + jnp.exp(sk - m)
            acc_ref[:, hsl] = jnp.dot((e / den).astype(BF16), vh, preferred_element_type=F32)
    _head_norm_store(acc_ref, gn_ref, o_ref)


def _attn_ctx(proj, sink, gn):
    est = 2 * (SEQ * D_ATTN * 4 + 4 * SEQ * KV_DIM * 4 + SEQ * D_ATTN * 2) + 3 * SEQ * D_ATTN * 4
    n_fill = N_LAT // SEQ
    real = lambda b: jnp.minimum(b, BATCH - 1)
    return pl.pallas_call(
        _with_fill(_attn_ctx_kernel, BATCH, -4, 1),
        name="attn_ctx",
        grid=(BATCH + n_fill,),
        in_specs=[
            pl.BlockSpec(memory_space=pltpu.SMEM),
            pl.BlockSpec((SEQ, D_ATTN), lambda b: (real(b), O_Q // D_ATTN)),
            pl.BlockSpec((SEQ, KV_DIM), lambda b: (real(b), O_K // KV_DIM)),
            pl.BlockSpec((SEQ, KV_DIM), lambda b: (real(b), O_V // KV_DIM)),
            pl.BlockSpec((1, D_ATTN), lambda b: (0, 0)),
        ],
        out_specs=[pl.BlockSpec((SEQ, D_ATTN), lambda b: (b, 0)),
                   pl.BlockSpec((SEQ, KV_DIM), lambda b: (real(b), 0)),
                   pl.BlockSpec((SEQ, KV_DIM), lambda b: (real(b), 0))],
        out_shape=[jax.ShapeDtypeStruct((N_TOK, D_ATTN), BF16),
                   jax.ShapeDtypeStruct((N_CTX, KV_DIM), F32),
                   jax.ShapeDtypeStruct((N_CTX, KV_DIM), F32)],
        scratch_shapes=[pltpu.VMEM((SEQ, D_ATTN), F32)],
        compiler_params=pltpu.CompilerParams(
            dimension_semantics=("arbitrary",), vmem_limit_bytes=_vmem_limit(est)),
    )(sink, proj, proj, proj, gn)


def _rope_tables():
    half = HEAD_DIM // 2
    quarter = half // 2
    pos = jnp.arange(DEC_SEQ, dtype=jnp.int32)
    row_pos = (pos // GRID_W).astype(F32)
    col_pos = (pos % GRID_W).astype(F32)
    inv = ROPE_BASE ** (-jnp.arange(quarter, dtype=F32) / quarter)
    lane = jnp.arange(HEAD_DIM, dtype=jnp.int32)
    p = jnp.where((lane < half)[None, :], row_pos[:, None], col_pos[:, None])
    ang = p * inv[lane % quarter][None, :]
    low = ((lane % half) < quarter)[None, :]
    cos = jnp.cos(ang)
    sin = jnp.sin(ang)
    return cos, jnp.where(low, -sin, 0.0), jnp.where(low, 0.0, sin)


ROPE_SHIFT = HEAD_DIM // 4
LAT_SPAN = BLOCK + 2 * WINDOW


def _rope(x, cos, sin_lo, sin_hi):
    return (x * cos + pltpu.roll(x, HEAD_DIM - ROPE_SHIFT, 1) * sin_lo
            + pltpu.roll(x, ROPE_SHIFT, 1) * sin_hi)


def _attn_lat_kernel(sink_ref, q_ref, k_ref, v_ref, ck_ref, cv_ref, cos_ref, slo_ref, shi_ref, gn_ref,
                     o_ref, acc_ref):
    i = pl.program_id(1)
    q0 = pl.multiple_of(i * BLOCK, BLOCK)
    k0 = pl.multiple_of(jnp.clip((i - 1) * BLOCK, 0, DEC_SEQ - LAT_SPAN), BLOCK)
    q_tab = (cos_ref[pl.ds(q0, BLOCK), :], slo_ref[pl.ds(q0, BLOCK), :], shi_ref[pl.ds(q0, BLOCK), :])
    k_tab = (cos_ref[pl.ds(k0, LAT_SPAN), :], slo_ref[pl.ds(k0, LAT_SPAN), :], shi_ref[pl.ds(k0, LAT_SPAN), :])
    q_pos = q0 + lax.broadcasted_iota(jnp.int32, (BLOCK, 1), 0)
    k_pos = k0 + lax.broadcasted_iota(jnp.int32, (1, LAT_SPAN), 1)
    valid = jnp.abs(q_pos - k_pos) <= WINDOW
    valid = jnp.concatenate([valid] * KV_GROUP, axis=0)
    for kvh in range(N_KV_HEADS):
        ksl = slice(kvh * HEAD_DIM, (kvh + 1) * HEAD_DIM)
        kw = _rope(k_ref[pl.ds(k0, LAT_SPAN), ksl], *k_tab).astype(BF16)
        vw = v_ref[pl.ds(k0, LAT_SPAN), ksl].astype(BF16)
        kc = ck_ref[0, :, ksl].astype(BF16)
        vc = cv_ref[0, :, ksl].astype(BF16)
        q = jnp.concatenate([_rope(q_ref[:, _head_lanes(kvh, g)], *q_tab) for g in range(KV_GROUP)],
                            axis=0).astype(BF16)
        sk = _group_sinks(sink_ref, kvh, BLOCK)
        s_lat = jnp.where(valid, _dot_nt(q, kw) * ATTN_SCALE, NEG_INF)
        s_ctx = _dot_nt(q, kc) * ATTN_SCALE
        m = jnp.maximum(jnp.maximum(jnp.max(s_lat, axis=-1, keepdims=True),
                                    jnp.max(s_ctx, axis=-1, keepdims=True)), sk)
        e_lat = jnp.exp(s_lat - m)
        e_ctx = jnp.exp(s_ctx - m)
        den = (jnp.sum(e_lat, axis=-1, keepdims=True) + jnp.sum(e_ctx, axis=-1, keepdims=True)
               + jnp.exp(sk - m))
        o = (jnp.dot((e_lat / den).astype(BF16), vw, preferred_element_type=F32)
             + jnp.dot((e_ctx / den).astype(BF16), vc, preferred_element_type=F32))
        for g in range(KV_GROUP):
            acc_ref[:, _head_lanes(kvh, g)] = o[g * BLOCK:(g + 1) * BLOCK]
    _head_norm_store(acc_ref, gn_ref, o_ref)


def _attn_lat(proj, cache_k, cache_v, sink, rope, gn, dest):
    nqb = DEC_SEQ // BLOCK
    est = (2 * (BLOCK * D_ATTN * 4 + 2 * DEC_SEQ * KV_DIM * 4 + 2 * PAST_LEN * KV_DIM * 4
                + 3 * DEC_SEQ * HEAD_DIM * 4 + BLOCK * D_ATTN * 2) + 8 * BLOCK * D_ATTN * 4)
    lat_row_block0 = N_CTX // BLOCK
    lat_seq_block0 = N_CTX // DEC_SEQ
    tab = pl.BlockSpec((DEC_SEQ, HEAD_DIM), lambda b, i: (0, 0))
    body, dest_specs, dest_args, aliases = _into_rows(_attn_lat_kernel, dest)
    return pl.pallas_call(
        body,
        name="attn_lat",
        grid=(DEC_BATCH, nqb),
        input_output_aliases=aliases,
        in_specs=dest_specs + [
            pl.BlockSpec(memory_space=pltpu.SMEM),
            pl.BlockSpec((BLOCK, D_ATTN), lambda b, i: (lat_row_block0 + b * nqb + i, O_Q // D_ATTN)),
            pl.BlockSpec((DEC_SEQ, KV_DIM), lambda b, i: (lat_seq_block0 + b, O_K // KV_DIM)),
            pl.BlockSpec((DEC_SEQ, KV_DIM), lambda b, i: (lat_seq_block0 + b, O_V // KV_DIM)),
            pl.BlockSpec((1, PAST_LEN, KV_DIM), lambda b, i: (b, 0, 0)),
            pl.BlockSpec((1, PAST_LEN, KV_DIM), lambda b, i: (b, 0, 0)),
            tab, tab, tab,
            pl.BlockSpec((1, D_ATTN), lambda b, i: (0, 0)),
        ],
        out_specs=pl.BlockSpec((BLOCK, D_ATTN), lambda b, i: (lat_row_block0 + b * nqb + i, 0)),
        out_shape=jax.ShapeDtypeStruct((N_TOK, D_ATTN), BF16),
        scratch_shapes=[pltpu.VMEM((BLOCK, D_ATTN), F32)],
        compiler_params=pltpu.CompilerParams(
            dimension_semantics=("arbitrary", "arbitrary"), vmem_limit_bytes=_vmem_limit(est)),
    )(*dest_args, sink, proj, proj, proj, cache_k, cache_v, *rope, gn)


def kernel(x_prompt, x_sample, cache_k, cache_v, c, c_ctx, w_ada, b_ada, norm_mix, w_in, hy_conv_w, hy_conv_b, filt_w1, filt_b1, filt_w2, filt_b2, filt_w3, filt_freq, hy_bias, attn_sink, gn_hyena, gn_attn, w_out, norm_ffn, w_up, ffn_conv_w, ffn_conv_b, w_down, final_norm):
    x = jnp.concatenate([x_prompt.reshape(N_CTX, D_MODEL), x_sample.reshape(N_LAT, D_MODEL)], axis=0)
    cond = jnp.concatenate([c_ctx[None, :], c, jnp.zeros((MOD_ROWS - 1 - DEC_BATCH, D_MODEL), F32)], axis=0)
    mod_all = _modulation(cond, w_ada, b_ada)

    rope = _rope_tables()
    tables = {L: _dft_tables(L) for L in (SEQ, DEC_SEQ)}
    feats = {L: _filter_features(L) for L in (SEQ, DEC_SEQ)}
    deltas = jnp.linspace(MIN_DECAY, MAX_DECAY, D_HYENA, dtype=F32)[None, :]

    w_down_bf16 = w_down.astype(BF16)
    new_k = []
    new_v = []
    for i in range(DEPTH):
        mod = mod_all[i]
        h = _prenorm(x, norm_mix[i], mod, 0, 1)
        proj = _matmul(h, w_in, i, 1024, 1024)

        hy_w_t = hy_conv_w[i].T
        hy_b = hy_conv_b[i][None, :]
        w1p = jnp.pad(filt_w1[i], ((0, FILT_PAD - FILTER_EMB), (0, 0)))
        y_a = None
        for L, tc, nseq, rb0 in ((SEQ, 1024, BATCH, 0), (DEC_SEQ, 256, DEC_BATCH, N_CTX // DEC_SEQ)):
            cm, sm, smt = tables[L]
            ka, kb = _filter_spectrum(L, feats[L], w1p, filt_b1[i][None, :], filt_w2[i], filt_b2[i][None, :],
                                      filt_w3[i], filt_freq[i][None, :], deltas, cm, sm)
            y_a = _hyena(proj, hy_w_t, hy_b, hy_bias[i][None, :], ka, kb, cm, sm, smt,
                         gn_hyena[i][None, :], L=L, tc=tc, nseq=nseq, row_block0=rb0, dest=y_a)
        gn_a = gn_attn[i][None, :]
        y_b, k_ctx, v_ctx = _attn_ctx(proj, attn_sink[i], gn_a)
        y_b = _attn_lat(proj, cache_k[:, i].reshape(DEC_BATCH, PAST_LEN, KV_DIM),
                        cache_v[:, i].reshape(DEC_BATCH, PAST_LEN, KV_DIM), attn_sink[i], rope, gn_a, y_b)
        x = _matmul_residual([y_a, y_b], w_out, i, x, mod, 2, 512, 1024, "out_proj")

        h = _prenorm(x, norm_ffn[i], mod, 3, 4)
        ffn_w_t = ffn_conv_w[i].T
        ffn_b = ffn_conv_b[i][None, :]
        n_ffn_blocks = D_FF // FFN_TN
        acts = [_ffn_up2(h, w_up, i, ffn_w_t, ffn_b)]
        if n_ffn_blocks % 2:
            acts.append(_ffn_up(h, w_up, i, ffn_w_t, ffn_b, n_ffn_blocks - 1, 1))
        x = _matmul_residual(acts, w_down_bf16, i, x, mod, 5, 512, 512, "ffn_down")

        new_k.append(k_ctx.reshape(BATCH, SEQ, N_KV_HEADS, HEAD_DIM))
        new_v.append(v_ctx.reshape(BATCH, SEQ, N_KV_HEADS, HEAD_DIM))

    y_prompt = _final_norm(x, final_norm, 0, N_CTX).reshape(BATCH, SEQ, D_MODEL)
    y_sample = _final_norm(x, final_norm, N_CTX // NORM_TR, N_LAT).reshape(DEC_BATCH, DEC_SEQ, D_MODEL)
    return (y_prompt, y_sample, jnp.stack(new_k, axis=1), jnp.stack(new_v, axis=1))
```

```python
import functools
import math

import jax
import jax.numpy as jnp
from jax import lax
from jax.experimental import pallas as pl
from jax.experimental.pallas import tpu as pltpu

F32 = jnp.float32
BF16 = jnp.bfloat16
HIGHEST = lax.Precision.HIGHEST

D_MODEL = 4096
BATCH = 32
SEQ = 256
DEPTH = 4
DEC_BATCH = 2
DEC_SEQ = 1024
PAST_LEN = 512
GRID_W = 64
D_HYENA = D_MODEL // 2
D_ATTN = D_MODEL - D_HYENA
HEAD_DIM = 128
N_HEADS = D_ATTN // HEAD_DIM
N_KV_HEADS = 4
KV_GROUP = N_HEADS // N_KV_HEADS
KV_DIM = N_KV_HEADS * HEAD_DIM
WINDOW = 128
BLOCK = 128
ATTN_SCALE = 1.0 / math.sqrt(HEAD_DIM)
ROPE_BASE = 10000.0
FILTER_EMB = 33
FILTER_BANDS = (FILTER_EMB - 1) // 2
FILTER_ORDER = 64
DECAY_TARGET = 1e-2
MAX_DECAY = math.log(DECAY_TARGET) / 0.3
MIN_DECAY = math.log(DECAY_TARGET) / 1.5
D_FF = 11008
RMS_EPS = 1e-6
NEG_INF = -1e30
O_Q = 3 * D_HYENA
O_K = O_Q + D_ATTN
O_V = O_K + KV_DIM
D_IN = O_V + KV_DIM

N_CTX = BATCH * SEQ
N_LAT = DEC_BATCH * DEC_SEQ
N_TOK = N_CTX + N_LAT
MOD_ROWS = 8
N_MOD = 6 * D_MODEL

V7X_VMEM_BYTES = 64 * 1024 * 1024
V7X_VMEM_USABLE = 58 * 1024 * 1024
MIB = 1024 * 1024


def _vmem_limit(estimate_bytes):
    return int(min(V7X_VMEM_USABLE, estimate_bytes + estimate_bytes // 4 + 2 * MIB))


def _mod_row(row_start):
    return jnp.where(row_start < N_CTX, 0, 1 + (row_start - N_CTX) // DEC_SEQ)


def _seq_len(row_start):
    return jnp.where(row_start < N_CTX, SEQ, DEC_SEQ)


SUBLANES = 8


def _zero_rows_at(x, sub, breaks):
    pick = lax.broadcasted_iota(jnp.int32, (SUBLANES, 1), 0) == sub
    pieces = []
    at = 0
    for start in sorted(breaks):
        if start > at:
            pieces.append(x[at:start])
        cond = pick if breaks[start] is True else jnp.logical_and(pick, breaks[start])
        pieces.append(jnp.where(cond, 0.0, x[start:start + SUBLANES]))
        at = start + SUBLANES
    if at < x.shape[0]:
        pieces.append(x[at:])
    return jnp.concatenate(pieces, axis=0)


def _conv3_rows(u, w, b, seq_rows=None, split=False):
    rows = u.shape[0]
    seq_rows = rows if seq_rows is None else seq_rows
    starts = range(0, rows, seq_rows)
    up = _zero_rows_at(pltpu.roll(u, 1, 0), 0, {s: (True if s == 0 else split) for s in starts})
    un = _zero_rows_at(pltpu.roll(u, rows - 1, 0), SUBLANES - 1,
                       {s + seq_rows - SUBLANES: (True if s + seq_rows == rows else split) for s in starts})
    return up * w[0:1, :] + u * w[1:2, :] + un * w[2:3, :] + b


MOD_TN = 512


def _mod_kernel(c_ref, w_ref, b_ref, o_ref):
    c = c_ref[...]
    s = (c * jax.nn.sigmoid(c)).astype(BF16)
    o_ref[0] = jnp.dot(s, w_ref[0].astype(BF16), preferred_element_type=F32) + b_ref[0]


def _modulation(cond, w_ada, b_ada):
    est = 2 * D_MODEL * MOD_TN * 4 + D_MODEL * MOD_TN * 2 + 4 * MOD_ROWS * D_MODEL * 4
    return pl.pallas_call(
        _mod_kernel,
        name="adaln_mod",
        grid=(DEPTH, N_MOD // MOD_TN),
        in_specs=[
            pl.BlockSpec((MOD_ROWS, D_MODEL), lambda l, n: (0, 0)),
            pl.BlockSpec((1, D_MODEL, MOD_TN), lambda l, n: (l, 0, n)),
            pl.BlockSpec((1, 1, MOD_TN), lambda l, n: (l, 0, n)),
        ],
        out_specs=pl.BlockSpec((1, MOD_ROWS, MOD_TN), lambda l, n: (l, 0, n)),
        out_shape=jax.ShapeDtypeStruct((DEPTH, MOD_ROWS, N_MOD), F32),
        compiler_params=pltpu.CompilerParams(
            dimension_semantics=("arbitrary", "arbitrary"), vmem_limit_bytes=_vmem_limit(est)),
    )(cond, w_ada, b_ada.reshape(DEPTH, 1, N_MOD))


NORM_TR = 256


def _prenorm_kernel(x_ref, g_ref, shift_ref, scale_ref, o_ref):
    r = _mod_row(pl.program_id(0) * NORM_TR)
    x = x_ref[...]
    y = x * lax.rsqrt(jnp.mean(x * x, axis=-1, keepdims=True) + RMS_EPS) * g_ref[...]
    o_ref[...] = (y * (1.0 + scale_ref[pl.ds(r, 1), :]) + shift_ref[pl.ds(r, 1), :]).astype(o_ref.dtype)


def _prenorm(x, g, mod, shift_chunk, scale_chunk):
    est = 2 * NORM_TR * D_MODEL * (4 + 2) + 3 * NORM_TR * D_MODEL * 4
    return pl.pallas_call(
        _prenorm_kernel,
        name="prenorm",
        grid=(N_TOK // NORM_TR,),
        in_specs=[
            pl.BlockSpec((NORM_TR, D_MODEL), lambda i: (i, 0)),
            pl.BlockSpec((1, D_MODEL), lambda i: (0, 0)),
            pl.BlockSpec((MOD_ROWS, D_MODEL), lambda i: (0, shift_chunk)),
            pl.BlockSpec((MOD_ROWS, D_MODEL), lambda i: (0, scale_chunk)),
        ],
        out_specs=pl.BlockSpec((NORM_TR, D_MODEL), lambda i: (i, 0)),
        out_shape=jax.ShapeDtypeStruct((N_TOK, D_MODEL), BF16),
        compiler_params=pltpu.CompilerParams(
            dimension_semantics=("arbitrary",), vmem_limit_bytes=_vmem_limit(est)),
    )(x, g.reshape(1, D_MODEL), mod, mod)


def _final_norm_kernel(x_ref, g_ref, o_ref):
    x = x_ref[...]
    o_ref[...] = x * lax.rsqrt(jnp.mean(x * x, axis=-1, keepdims=True) + RMS_EPS) * g_ref[...]


def _final_norm(x, g, row_block0, rows):
    est = 4 * NORM_TR * D_MODEL * 4 + 2 * NORM_TR * D_MODEL * 4
    return pl.pallas_call(
        _final_norm_kernel,
        name="final_norm",
        grid=(rows // NORM_TR,),
        in_specs=[
            pl.BlockSpec((NORM_TR, D_MODEL), lambda i: (row_block0 + i, 0)),
            pl.BlockSpec((1, D_MODEL), lambda i: (0, 0)),
        ],
        out_specs=pl.BlockSpec((NORM_TR, D_MODEL), lambda i: (i, 0)),
        out_shape=jax.ShapeDtypeStruct((rows, D_MODEL), F32),
        compiler_params=pltpu.CompilerParams(
            dimension_semantics=("arbitrary",), vmem_limit_bytes=_vmem_limit(est)),
    )(x, g.reshape(1, D_MODEL))


W_CHUNKS = 8


class _Stream:
    def __init__(self, n_tiles, k):
        self.n_tiles = n_tiles
        self.kc = k // W_CHUNKS

    def token(self, jj, i):
        return jnp.where(jj == 0, 0, i)

    def tile(self, jj):
        return jnp.maximum(jj - 1, 0)

    def chunk_spec(self, layer, width, col_block):
        last = W_CHUNKS - 1

        def index(jj, i):
            arriving = jnp.minimum(jj, self.n_tiles - 1)
            chunk = jnp.where(jj < self.n_tiles, jnp.minimum(i, last), last)
            return layer, chunk, col_block(arriving)
        return pl.BlockSpec((None, self.kc, width), index)

    def cast_chunk(self, chunk_refs, wb_ref):
        jj = pl.program_id(0)
        i = pl.program_id(1)

        @pl.when(jnp.logical_and(jj < self.n_tiles, i < W_CHUNKS))
        def _():
            rows = pl.ds(pl.multiple_of(i * self.kc, self.kc), self.kc)
            lane = 0
            for ref in chunk_refs:
                width = ref.shape[-1]
                wb_ref[jj % 2, rows, lane:lane + width] = ref[...].astype(BF16)
                lane += width

    def multiplying(self):
        return pl.program_id(0) >= 1

    def weights(self, wb_ref):
        return wb_ref.at[(pl.program_id(0) + 1) % 2]


def _matmul_kernel(x_ref, w_ref, o_ref, wb_ref, *, stream):
    stream.cast_chunk([w_ref], wb_ref)
    w = stream.weights(wb_ref)

    @pl.when(stream.multiplying())
    def _():
        o_ref[...] = jnp.dot(x_ref[...], w[...], preferred_element_type=F32)


def _matmul(x, w_stack, layer, tm, tn):
    m, k = x.shape
    n = w_stack.shape[2]
    assert m // tm > W_CHUNKS
    stream = _Stream(n // tn, k)
    est = 2 * (tm * k * 2 + tm * tn * 4 + stream.kc * tn * 4) + 2 * k * tn * 2 + tm * tn * 4
    return pl.pallas_call(
        functools.partial(_matmul_kernel, stream=stream),
        name="in_proj",
        grid=(n // tn + 1, m // tm),
        in_specs=[
            pl.BlockSpec((tm, k), lambda jj, i: (stream.token(jj, i), 0)),
            stream.chunk_spec(layer, tn, lambda t: t),
        ],
        out_specs=pl.BlockSpec((tm, tn), lambda jj, i: (stream.token(jj, i), stream.tile(jj))),
        out_shape=jax.ShapeDtypeStruct((m, n), F32),
        scratch_shapes=[pltpu.VMEM((2, k, tn), BF16)],
        compiler_params=pltpu.CompilerParams(
            dimension_semantics=("arbitrary", "arbitrary"), vmem_limit_bytes=_vmem_limit(est)),
    )(x, w_stack)


def _residual_update(lhs, w, res_ref, gate_ref, o_ref, mod_row):
    y = None
    k0 = 0
    for x_ref in lhs:
        kk = x_ref.shape[1]
        part = jnp.dot(x_ref[...], w[k0:k0 + kk, :], preferred_element_type=F32)
        y = part if y is None else y + part
        k0 += kk
    o_ref[...] = res_ref[...] + gate_ref[pl.ds(mod_row, 1), :] * y


def _matmul_residual_kernel(*refs, n_lhs, tm):
    w_ref, res_ref, gate_ref, o_ref = refs[n_lhs:]
    _residual_update(refs[:n_lhs], w_ref, res_ref, gate_ref, o_ref, _mod_row(pl.program_id(1) * tm))


def _matmul_residual_stream_kernel(*refs, n_lhs, tm, stream):
    w_ref, res_ref, gate_ref, o_ref, wb_ref = refs[n_lhs:]
    stream.cast_chunk([w_ref], wb_ref)
    mod_row = _mod_row(pl.program_id(1) * tm)
    w = stream.weights(wb_ref)

    @pl.when(stream.multiplying())
    def _():
        _residual_update(refs[:n_lhs], w, res_ref, gate_ref, o_ref, mod_row)


def _matmul_residual(lhs, w_stack, layer, res, mod, gate_chunk, tm, tn, name):
    m = res.shape[0]
    _, k, n = w_stack.shape
    gate_block0 = gate_chunk * (D_MODEL // tn)
    io_bytes = 2 * (tm * k * 2 + 2 * tm * tn * 4 + MOD_ROWS * tn * 4) + 2 * tm * tn * 4
    if w_stack.dtype == BF16:
        token = lambda j, i: i
        tile = lambda j: j
        grid = (n // tn, m // tm)
        w_spec = pl.BlockSpec((None, k, tn), lambda j, i: (layer, 0, j))
        body = functools.partial(_matmul_residual_kernel, n_lhs=len(lhs), tm=tm)
        scratch = []
        est = io_bytes + 2 * k * tn * 2
    else:
        assert m // tm > W_CHUNKS
        stream = _Stream(n // tn, k)
        token, tile = stream.token, stream.tile
        grid = (n // tn + 1, m // tm)
        w_spec = stream.chunk_spec(layer, tn, lambda t: t)
        body = functools.partial(_matmul_residual_stream_kernel, n_lhs=len(lhs), tm=tm, stream=stream)
        scratch = [pltpu.VMEM((2, k, tn), BF16)]
        est = io_bytes + 2 * k * tn * 2 + 2 * stream.kc * tn * 4
    return pl.pallas_call(
        body,
        name=name,
        grid=grid,
        in_specs=[pl.BlockSpec((tm, x.shape[1]), lambda j, i: (token(j, i), 0)) for x in lhs] + [
            w_spec,
            pl.BlockSpec((tm, tn), lambda j, i: (token(j, i), tile(j))),
            pl.BlockSpec((MOD_ROWS, tn), lambda j, i: (0, gate_block0 + tile(j))),
        ],
        out_specs=pl.BlockSpec((tm, tn), lambda j, i: (token(j, i), tile(j))),
        out_shape=jax.ShapeDtypeStruct((m, n), F32),
        scratch_shapes=scratch,
        compiler_params=pltpu.CompilerParams(
            dimension_semantics=("arbitrary", "arbitrary"), vmem_limit_bytes=_vmem_limit(est)),
    )(*lhs, w_stack, res, mod)


FFN_TM = 1024
FFN_TN = 256


def _ffn_up_kernel(h_ref, wg_ref, wv_ref, cwg_ref, cwv_ref, cbg_ref, cbv_ref, o_ref, wb_ref):
    @pl.when(pl.program_id(1) == 0)
    def _():
        wb_ref[:, :FFN_TN] = wg_ref[...].astype(BF16)
        wb_ref[:, FFN_TN:] = wv_ref[...].astype(BF16)

    u = jnp.dot(h_ref[...], wb_ref[...], preferred_element_type=F32)
    is_ctx = pl.program_id(1) * FFN_TM < N_CTX
    g = _conv3_rows(u[:, :FFN_TN], cwg_ref[...], cbg_ref[...], SEQ, is_ctx)
    v = _conv3_rows(u[:, FFN_TN:], cwv_ref[...], cbv_ref[...], SEQ, is_ctx)
    o_ref[...] = (g * jax.nn.sigmoid(g) * v).astype(o_ref.dtype)


FFN_PAIR = 2 * FFN_TN


def _ffn_up2_kernel(h_ref, wg_ref, wva_ref, wvb_ref, cwg_ref, cwva_ref, cwvb_ref, cbg_ref, cbva_ref, cbvb_ref,
                    o_ref, wb_ref, *, stream):
    stream.cast_chunk([wg_ref, wva_ref, wvb_ref], wb_ref)
    is_ctx = pl.program_id(1) * FFN_TM < N_CTX
    w = stream.weights(wb_ref)

    @pl.when(stream.multiplying())
    def _():
        u = jnp.dot(h_ref[...], w[...], preferred_element_type=F32)
        cwg = cwg_ref[...]
        cbg = cbg_ref[...]
        for blk, (cwv_ref, cbv_ref) in enumerate(((cwva_ref, cbva_ref), (cwvb_ref, cbvb_ref))):
            lanes = slice(blk * FFN_TN, (blk + 1) * FFN_TN)
            g = _conv3_rows(u[:, lanes], cwg[:, lanes], cbg[:, lanes], SEQ, is_ctx)
            v = _conv3_rows(u[:, FFN_PAIR + blk * FFN_TN:FFN_PAIR + (blk + 1) * FFN_TN], cwv_ref[...],
                            cbv_ref[...], SEQ, is_ctx)
            o_ref[:, lanes] = (g * jax.nn.sigmoid(g) * v).astype(o_ref.dtype)


def _ffn_up2(h, w_up, layer, conv_w_t, conv_b):
    nb = D_FF // FFN_TN
    npair = nb // 2
    n_m = N_TOK // FFN_TM
    assert n_m > W_CHUNKS
    stream = _Stream(npair, D_MODEL)
    est = (2 * (FFN_TM * D_MODEL * 2 + FFN_TM * FFN_PAIR * 2 + stream.kc * 2 * FFN_PAIR * 4)
           + 2 * D_MODEL * 2 * FFN_PAIR * 2 + 12 * FFN_TM * FFN_PAIR * 4)
    pair = lambda rows: pl.BlockSpec((rows, FFN_PAIR), lambda jj, i: (0, stream.tile(jj)))
    val = lambda rows, blk: pl.BlockSpec((rows, FFN_TN), lambda jj, i: (0, nb + 2 * stream.tile(jj) + blk))
    return pl.pallas_call(
        functools.partial(_ffn_up2_kernel, stream=stream),
        name="ffn_up2",
        grid=(npair + 1, n_m),
        in_specs=[
            pl.BlockSpec((FFN_TM, D_MODEL), lambda jj, i: (stream.token(jj, i), 0)),
            stream.chunk_spec(layer, FFN_PAIR, lambda t: t),
            stream.chunk_spec(layer, FFN_TN, lambda t: nb + 2 * t),
            stream.chunk_spec(layer, FFN_TN, lambda t: nb + 2 * t + 1),
            pair(3), val(3, 0), val(3, 1), pair(1), val(1, 0), val(1, 1),
        ],
        out_specs=pl.BlockSpec((FFN_TM, FFN_PAIR), lambda jj, i: (stream.token(jj, i), stream.tile(jj))),
        out_shape=jax.ShapeDtypeStruct((N_TOK, npair * FFN_PAIR), BF16),
        scratch_shapes=[pltpu.VMEM((2, D_MODEL, 2 * FFN_PAIR), BF16)],
        compiler_params=pltpu.CompilerParams(
            dimension_semantics=("arbitrary", "arbitrary"), vmem_limit_bytes=_vmem_limit(est)),
    )(h, w_up, w_up, w_up, conv_w_t, conv_w_t, conv_w_t, conv_b, conv_b, conv_b)


def _ffn_up(h, w_up, layer, conv_w_t, conv_b, block0, n_blocks):
    nb = D_FF // FFN_TN
    est = (2 * (FFN_TM * D_MODEL * 2 + 2 * D_MODEL * FFN_TN * 4 + FFN_TM * FFN_TN * 2)
           + 2 * D_MODEL * FFN_TN * 2 + 10 * FFN_TM * FFN_TN * 4)
    return pl.pallas_call(
        _ffn_up_kernel,
        name="ffn_up",
        grid=(n_blocks, N_TOK // FFN_TM),
        in_specs=[
            pl.BlockSpec((FFN_TM, D_MODEL), lambda j, i: (i, 0)),
            pl.BlockSpec((None, D_MODEL, FFN_TN), lambda j, i: (layer, 0, block0 + j)),
            pl.BlockSpec((None, D_MODEL, FFN_TN), lambda j, i: (layer, 0, nb + block0 + j)),
            pl.BlockSpec((3, FFN_TN), lambda j, i: (0, block0 + j)),
            pl.BlockSpec((3, FFN_TN), lambda j, i: (0, nb + block0 + j)),
            pl.BlockSpec((1, FFN_TN), lambda j, i: (0, block0 + j)),
            pl.BlockSpec((1, FFN_TN), lambda j, i: (0, nb + block0 + j)),
        ],
        out_specs=pl.BlockSpec((FFN_TM, FFN_TN), lambda j, i: (i, j)),
        out_shape=jax.ShapeDtypeStruct((N_TOK, n_blocks * FFN_TN), BF16),
        scratch_shapes=[pltpu.VMEM((D_MODEL, 2 * FFN_TN), BF16)],
        compiler_params=pltpu.CompilerParams(
            dimension_semantics=("arbitrary", "arbitrary"), vmem_limit_bytes=_vmem_limit(est)),
    )(h, w_up, w_up, conv_w_t, conv_w_t, conv_b, conv_b)


FILT_PAD = 64


def _split_bf16(a):
    hi = a.astype(BF16)
    return hi, (a - hi.astype(F32)).astype(BF16)


def _dot_split(t_hi, t_lo, b):
    b_hi, b_lo = _split_bf16(b)
    return (jnp.dot(t_hi, b_hi, preferred_element_type=F32)
            + jnp.dot(t_hi, b_lo, preferred_element_type=F32)
            + jnp.dot(t_lo, b_hi, preferred_element_type=F32))


def _dft_tables(L):
    f = jnp.arange(L, dtype=jnp.int32)[:, None]
    s = jnp.arange(L, dtype=jnp.int32)[None, :]
    ang = ((f * s) % (2 * L)).astype(F32) * (math.pi / L)
    cm = jnp.cos(ang)
    nyq = jnp.where(s % 2 == 0, 1.0, -1.0).astype(F32)
    sm = jnp.where(f == 0, nyq, -jnp.sin(ang))
    return tuple(jnp.stack(_split_bf16(t)) for t in (cm, sm, sm.T))


def _filter_features(L):
    t = jnp.linspace(0.0, 1.0, L, dtype=F32)[:, None]
    w = 2.0 * math.pi * jnp.arange(L, dtype=F32)[:, None] / L
    f = jnp.linspace(1e-4, FILTER_BANDS - 1, FILTER_BANDS, dtype=F32)[None, :]
    z = jnp.concatenate([t, jnp.cos(f * w), -jnp.sin(f * w)], axis=-1)
    return jnp.pad(z, ((0, 0), (0, FILT_PAD - FILTER_EMB)))


def _filter_kernel(z_ref, w1_ref, b1_ref, w2_ref, b2_ref, w3f_ref, w3b_ref, freq_ref, delta_ref,
                   cm_ref, sm_ref, ka_ref, kb_ref, *, L):
    fr = freq_ref[...]
    z = z_ref[...]
    h = jnp.sin(fr * (jnp.dot(z, w1_ref[...], precision=HIGHEST, preferred_element_type=F32) + b1_ref[...]))
    h = jnp.sin(fr * (jnp.dot(h, w2_ref[...], precision=HIGHEST, preferred_element_type=F32) + b2_ref[...]))
    decay = jnp.exp(-z[:, 0:1] * jnp.abs(delta_ref[...]))
    h_fwd = jnp.dot(h, w3f_ref[...], precision=HIGHEST, preferred_element_type=F32) * decay
    h_bwd = jnp.dot(h, w3b_ref[...], precision=HIGHEST, preferred_element_type=F32) * decay
    pos = lax.broadcasted_iota(jnp.int32, (L, 1), 0)
    first = pos == 0
    h_bwd = jnp.where(first, 0.0, h_bwd)
    even = h_fwd + h_bwd
    odd = h_fwd - h_bwd
    sign = jnp.where((pos & 1) == 0, 1.0, -1.0)
    nyq = jnp.sum(sign * even, axis=0, keepdims=True)
    ka_ref[...] = _dot_split(cm_ref[0], cm_ref[1], even)
    kb = _dot_split(sm_ref[0], sm_ref[1], odd)
    kb_ref[...] = jnp.where(first, nyq, kb)


FILT_TC = 256


def _filter_spectrum(L, z, w1p, b1, w2, b2, w3, freq, deltas, cm, sm):
    nb = D_HYENA // FILT_TC
    est = 2 * (2 * 2 * L * L * 2) + 2 * 4 * L * FILT_TC * 4 + 10 * L * FILT_TC * 4 + 4 * L * FILT_PAD * 4
    full = lambda shape: pl.BlockSpec(shape, lambda c: (0,) * len(shape))
    return pl.pallas_call(
        functools.partial(_filter_kernel, L=L),
        name=f"hyena_filter_{L}",
        grid=(nb,),
        in_specs=[
            full((L, FILT_PAD)), full((FILT_PAD, FILTER_ORDER)), full((1, FILTER_ORDER)),
            full((FILTER_ORDER, FILTER_ORDER)), full((1, FILTER_ORDER)),
            pl.BlockSpec((FILTER_ORDER, FILT_TC), lambda c: (0, c)),
            pl.BlockSpec((FILTER_ORDER, FILT_TC), lambda c: (0, nb + c)),
            full((1, FILTER_ORDER)),
            pl.BlockSpec((1, FILT_TC), lambda c: (0, c)),
            full((2, L, L)), full((2, L, L)),
        ],
        out_specs=[pl.BlockSpec((L, FILT_TC), lambda c: (0, c)),
                   pl.BlockSpec((L, FILT_TC), lambda c: (0, c))],
        out_shape=[jax.ShapeDtypeStruct((L, D_HYENA), F32)] * 2,
        compiler_params=pltpu.CompilerParams(
            dimension_semantics=("arbitrary",), vmem_limit_bytes=_vmem_limit(est)),
    )(z, w1p, b1, w2, b2, w3, w3, freq, deltas, cm, sm)


def _hyena_kernel(x0_ref, x1_ref, v_ref, cw0_ref, cw1_ref, cw2_ref, cb0_ref, cb1_ref, cb2_ref, hb_ref,
                  ka_ref, kb_ref, cm_ref, sm_ref, smt_ref, gn_ref, o_ref, acc_ref, *, L, tc, nct, grid_ids=None):
    ct = pl.program_id(1) if grid_ids is None else grid_ids[1]
    pos = lax.broadcasted_iota(jnp.int32, (L, 1), 0)
    first = pos == 0
    x0 = _conv3_rows(x0_ref[...], cw0_ref[...], cb0_ref[...])
    x1 = _conv3_rows(x1_ref[...], cw1_ref[...], cb1_ref[...])
    v = _conv3_rows(v_ref[...], cw2_ref[...], cb2_ref[...])
    u = x1 * v
    ub = u.astype(BF16)
    top = jnp.dot(cm_ref[...], ub, preferred_element_type=F32)
    bot = jnp.dot(sm_ref[...], ub, preferred_element_type=F32)
    ka = ka_ref[...]
    kb = kb_ref[...]
    y_top = top * ka - jnp.where(first, 0.0, bot * kb)
    y_bot = jnp.where(first, bot * kb, top * kb + bot * ka)
    wgt = jnp.where(first, 1.0 / (2 * L), 2.0 / (2 * L))
    y = (jnp.dot(cm_ref[...], (y_top * wgt).astype(BF16), preferred_element_type=F32)
         + jnp.dot(smt_ref[...], (y_bot * wgt).astype(BF16), preferred_element_type=F32))
    acc_ref[ct] = x0 * (y + u * hb_ref[...])

    @pl.when(ct == nct - 1)
    def _():
        ss = jnp.zeros((L, 1), F32)
        for j in range(nct):
            a = acc_ref[j]
            ss = ss + jnp.sum(a * a, axis=-1, keepdims=True)
        inv = lax.rsqrt(ss * (1.0 / D_HYENA) + RMS_EPS)
        for j in range(nct):
            o_ref[:, j * tc:(j + 1) * tc] = (acc_ref[j] * inv * gn_ref[:, j * tc:(j + 1) * tc]).astype(o_ref.dtype)


def _with_fill(kernel_fn, n_real, out_pos, n_axes):
    def body(*refs):
        grid_ids = tuple(pl.program_id(a) for a in range(n_axes))

        @pl.when(grid_ids[0] < n_real)
        def _():
            kernel_fn(*refs, grid_ids=grid_ids)

        @pl.when(grid_ids[0] >= n_real)
        def _():
            refs[out_pos][...] = jnp.zeros_like(refs[out_pos])
    return body


def _into_rows(kernel_fn, dest):
    if dest is None:
        return kernel_fn, [], [], {}

    def body(dest_ref, *refs):
        del dest_ref
        kernel_fn(*refs)
    return body, [pl.BlockSpec(memory_space=pl.ANY)], [dest], {0: 0}


def _hyena(proj, conv_w_t, conv_b, hy_bias, ka, kb, cm, sm, smt, gn, *, L, tc, nseq, row_block0, dest=None):
    nct = D_HYENA // tc
    cpb = D_HYENA // tc
    est = (3 * L * L * 2 + 2 * (3 * L * tc * 4 + 2 * L * tc * 4) + nct * L * tc * 4
           + 2 * L * D_HYENA * 2 + 14 * L * tc * 4)
    once = pl.Buffered(1)
    n_fill = N_TOK // L - row_block0 - nseq if dest is None else 0
    grp = lambda g: pl.BlockSpec((L, tc), lambda s, c: (row_block0 + jnp.minimum(s, nseq - 1), g * cpb + c))
    tap = lambda g: pl.BlockSpec((3, tc), lambda s, c: (0, g * cpb + c))
    bias = lambda g: pl.BlockSpec((1, tc), lambda s, c: (0, g * cpb + c))
    col = lambda rows: pl.BlockSpec((rows, tc), lambda s, c: (0, c))
    table = pl.BlockSpec((None, L, L), lambda s, c: (0, 0, 0), pipeline_mode=once)
    kernel_fn = functools.partial(_hyena_kernel, L=L, tc=tc, nct=nct)
    if n_fill:
        kernel_fn = _with_fill(kernel_fn, nseq, -2, 2)
    body, dest_specs, dest_args, aliases = _into_rows(kernel_fn, dest)
    return pl.pallas_call(
        body,
        name=f"hyena_conv_{L}",
        grid=(nseq + n_fill, nct),
        in_specs=dest_specs + [grp(0), grp(1), grp(2), tap(0), tap(1), tap(2), bias(0), bias(1), bias(2),
                               col(1), col(L), col(L), table, table, table,
                               pl.BlockSpec((1, D_HYENA), lambda s, c: (0, 0))],
        out_specs=pl.BlockSpec((L, D_HYENA), lambda s, c: (row_block0 + s, 0)),
        out_shape=jax.ShapeDtypeStruct((N_TOK, D_HYENA), BF16),
        input_output_aliases=aliases,
        scratch_shapes=[pltpu.VMEM((nct, L, tc), F32)],
        compiler_params=pltpu.CompilerParams(
            dimension_semantics=("arbitrary", "arbitrary"), vmem_limit_bytes=_vmem_limit(est)),
    )(*dest_args, proj, proj, proj, conv_w_t, conv_w_t, conv_w_t, conv_b, conv_b, conv_b,
      hy_bias, ka, kb, cm, sm, smt, gn)


def _dot_nt(a, b):
    return lax.dot_general(a, b, (((1,), (1,)), ((), ())), preferred_element_type=F32)


def _head_lanes(kvh, g):
    hd = kvh * KV_GROUP + g
    return slice(hd * HEAD_DIM, (hd + 1) * HEAD_DIM)


def _group_sinks(sink_ref, kvh, rows):
    return jnp.concatenate([jnp.full((rows, 1), sink_ref[kvh * KV_GROUP + g], F32) for g in range(KV_GROUP)],
                           axis=0)


def _head_norm_store(acc_ref, gn_ref, o_ref):
    y = acc_ref[...]
    inv = lax.rsqrt(jnp.mean(y * y, axis=-1, keepdims=True) + RMS_EPS)
    o_ref[...] = (y * inv * gn_ref[...]).astype(o_ref.dtype)


def _attn_ctx_kernel(sink_ref, q_ref, k_ref, v_ref, gn_ref, o_ref, ko_ref, vo_ref, acc_ref, grid_ids=None):
    del grid_ids
    ko_ref[...] = k_ref[...]
    vo_ref[...] = v_ref[...]
    for kvh in range(N_KV_HEADS):
        ksl = slice(kvh * HEAD_DIM, (kvh + 1) * HEAD_DIM)
        kh = k_ref[:, ksl].astype(BF16)
        vh = v_ref[:, ksl].astype(BF16)
        for g in range(KV_GROUP):
            hsl = _head_lanes(kvh, g)
            s = _dot_nt(q_ref[:, hsl].astype(BF16), kh) * ATTN_SCALE
            sk = sink_ref[kvh * KV_GROUP + g]
            m = jnp.maximum(jnp.max(s, axis=-1, keepdims=True), sk)
            e = jnp.exp(s - m)
            den = jnp.sum(e, axis=-1, keepdims=True) + jnp.exp(sk - m)
            acc_ref[:, hsl] = jnp.dot((e / den).astype(BF16), vh, preferred_element_type=F32)
    _head_norm_store(acc_ref, gn_ref, o_ref)


def _attn_ctx(proj, sink, gn):
    est = 2 * (SEQ * D_ATTN * 4 + 4 * SEQ * KV_DIM * 4 + SEQ * D_ATTN * 2) + 3 * SEQ * D_ATTN * 4
    n_fill = N_LAT // SEQ
    real = lambda b: jnp.minimum(b, BATCH - 1)
    return pl.pallas_call(
        _with_fill(_attn_ctx_kernel, BATCH, -4, 1),
        name="attn_ctx",
        grid=(BATCH + n_fill,),
        in_specs=[
            pl.BlockSpec(memory_space=pltpu.SMEM),
            pl.BlockSpec((SEQ, D_ATTN), lambda b: (real(b), O_Q // D_ATTN)),
            pl.BlockSpec((SEQ, KV_DIM), lambda b: (real(b), O_K // KV_DIM)),
            pl.BlockSpec((SEQ, KV_DIM), lambda b: (real(b), O_V // KV_DIM)),
            pl.BlockSpec((1, D_ATTN), lambda b: (0, 0)),
        ],
        out_specs=[pl.BlockSpec((SEQ, D_ATTN), lambda b: (b, 0)),
                   pl.BlockSpec((SEQ, KV_DIM), lambda b: (real(b), 0)),
                   pl.BlockSpec((SEQ, KV_DIM), lambda b: (real(b), 0))],
        out_shape=[jax.ShapeDtypeStruct((N_TOK, D_ATTN), BF16),
                   jax.ShapeDtypeStruct((N_CTX, KV_DIM), F32),
                   jax.ShapeDtypeStruct((N_CTX, KV_DIM), F32)],
        scratch_shapes=[pltpu.VMEM((SEQ, D_ATTN), F32)],
        compiler_params=pltpu.CompilerParams(
            dimension_semantics=("arbitrary",), vmem_limit_bytes=_vmem_limit(est)),
    )(sink, proj, proj, proj, gn)


def _rope_tables():
    half = HEAD_DIM // 2
    quarter = half // 2
    pos = jnp.arange(DEC_SEQ, dtype=jnp.int32)
    row_pos = (pos // GRID_W).astype(F32)
    col_pos = (pos % GRID_W).astype(F32)
    inv = ROPE_BASE ** (-jnp.arange(quarter, dtype=F32) / quarter)
    lane = jnp.arange(HEAD_DIM, dtype=jnp.int32)
    p = jnp.where((lane < half)[None, :], row_pos[:, None], col_pos[:, None])
    ang = p * inv[lane % quarter][None, :]
    low = ((lane % half) < quarter)[None, :]
    cos = jnp.cos(ang)
    sin = jnp.sin(ang)
    return cos, jnp.where(low, -sin, 0.0), jnp.where(low, 0.0, sin)


ROPE_SHIFT = HEAD_DIM // 4
LAT_SPAN = BLOCK + 2 * WINDOW


def _rope(x, cos, sin_lo, sin_hi):
    return (x * cos + pltpu.roll(x, HEAD_DIM - ROPE_SHIFT, 1) * sin_lo
            + pltpu.roll(x, ROPE_SHIFT, 1) * sin_hi)


def _attn_lat_kernel(sink_ref, q_ref, k_ref, v_ref, ck_ref, cv_ref, cos_ref, slo_ref, shi_ref, gn_ref,
                     o_ref, acc_ref):
    i = pl.program_id(1)
    q0 = pl.multiple_of(i * BLOCK, BLOCK)
    k0 = pl.multiple_of(jnp.clip((i - 1) * BLOCK, 0, DEC_SEQ - LAT_SPAN), BLOCK)
    q_tab = (cos_ref[pl.ds(q0, BLOCK), :], slo_ref[pl.ds(q0, BLOCK), :], shi_ref[pl.ds(q0, BLOCK), :])
    k_tab = (cos_ref[pl.ds(k0, LAT_SPAN), :], slo_ref[pl.ds(k0, LAT_SPAN), :], shi_ref[pl.ds(k0, LAT_SPAN), :])
    q_pos = q0 + lax.broadcasted_iota(jnp.int32, (BLOCK, 1), 0)
    k_pos = k0 + lax.broadcasted_iota(jnp.int32, (1, LAT_SPAN), 1)
    valid = jnp.abs(q_pos - k_pos) <= WINDOW
    valid = jnp.concatenate([valid] * KV_GROUP, axis=0)
    for kvh in range(N_KV_HEADS):
        ksl = slice(kvh * HEAD_DIM, (kvh + 1) * HEAD_DIM)
        kw = _rope(k_ref[pl.ds(k0, LAT_SPAN), ksl], *k_tab).astype(BF16)
        vw = v_ref[pl.ds(k0, LAT_SPAN), ksl].astype(BF16)
        kc = ck_ref[0, :, ksl].astype(BF16)
        vc = cv_ref[0, :, ksl].astype(BF16)
        q = jnp.concatenate([_rope(q_ref[:, _head_lanes(kvh, g)], *q_tab) for g in range(KV_GROUP)],
                            axis=0).astype(BF16)
        sk = _group_sinks(sink_ref, kvh, BLOCK)
        s_lat = jnp.where(valid, _dot_nt(q, kw) * ATTN_SCALE, NEG_INF)
        s_ctx = _dot_nt(q, kc) * ATTN_SCALE
        m = jnp.maximum(jnp.maximum(jnp.max(s_lat, axis=-1, keepdims=True),
                                    jnp.max(s_ctx, axis=-1, keepdims=True)), sk)
        e_lat = jnp.exp(s_lat - m)
        e_ctx = jnp.exp(s_ctx - m)
        den = (jnp.sum(e_lat, axis=-1, keepdims=True) + jnp.sum(e_ctx, axis=-1, keepdims=True)
               + jnp.exp(sk - m))
        o = (jnp.dot((e_lat / den).astype(BF16), vw, preferred_element_type=F32)
             + jnp.dot((e_ctx / den).astype(BF16), vc, preferred_element_type=F32))
        for g in range(KV_GROUP):
            acc_ref[:, _head_lanes(kvh, g)] = o[g * BLOCK:(g + 1) * BLOCK]
    _head_norm_store(acc_ref, gn_ref, o_ref)


def _attn_lat(proj, cache_k, cache_v, sink, rope, gn, dest):
    nqb = DEC_SEQ // BLOCK
    est = (2 * (BLOCK * D_ATTN * 4 + 2 * DEC_SEQ * KV_DIM * 4 + 2 * PAST_LEN * KV_DIM * 4
                + 3 * DEC_SEQ * HEAD_DIM * 4 + BLOCK * D_ATTN * 2) + 8 * BLOCK * D_ATTN * 4)
    lat_row_block0 = N_CTX // BLOCK
    lat_seq_block0 = N_CTX // DEC_SEQ
    tab = pl.BlockSpec((DEC_SEQ, HEAD_DIM), lambda b, i: (0, 0))
    body, dest_specs, dest_args, aliases = _into_rows(_attn_lat_kernel, dest)
    return pl.pallas_call(
        body,
        name="attn_lat",
        grid=(DEC_BATCH, nqb),
        input_output_aliases=aliases,
        in_specs=dest_specs + [
            pl.BlockSpec(memory_space=pltpu.SMEM),
            pl.BlockSpec((BLOCK, D_ATTN), lambda b, i: (lat_row_block0 + b * nqb + i, O_Q // D_ATTN)),
            pl.BlockSpec((DEC_SEQ, KV_DIM), lambda b, i: (lat_seq_block0 + b, O_K // KV_DIM)),
            pl.BlockSpec((DEC_SEQ, KV_DIM), lambda b, i: (lat_seq_block0 + b, O_V // KV_DIM)),
            pl.BlockSpec((1, PAST_LEN, KV_DIM), lambda b, i: (b, 0, 0)),
            pl.BlockSpec((1, PAST_LEN, KV_DIM), lambda b, i: (b, 0, 0)),
            tab, tab, tab,
            pl.BlockSpec((1, D_ATTN), lambda b, i: (0, 0)),
        ],
        out_specs=pl.BlockSpec((BLOCK, D_ATTN), lambda b, i: (lat_row_block0 + b * nqb + i, 0)),
        out_shape=jax.ShapeDtypeStruct((N_TOK, D_ATTN), BF16),
        scratch_shapes=[pltpu.VMEM((BLOCK, D_ATTN), F32)],
        compiler_params=pltpu.CompilerParams(
            dimension_semantics=("arbitrary", "arbitrary"), vmem_limit_bytes=_vmem_limit(est)),
    )(*dest_args, sink, proj, proj, proj, cache_k, cache_v, *rope, gn)


def kernel(x_prompt, x_sample, cache_k, cache_v, c, c_ctx, w_ada, b_ada, norm_mix, w_in, hy_conv_w, hy_conv_b, filt_w1, filt_b1, filt_w2, filt_b2, filt_w3, filt_freq, hy_bias, attn_sink, gn_hyena, gn_attn, w_out, norm_ffn, w_up, ffn_conv_w, ffn_conv_b, w_down, final_norm):
    x = jnp.concatenate([x_prompt.reshape(N_CTX, D_MODEL), x_sample.reshape(N_LAT, D_MODEL)], axis=0)
    cond = jnp.concatenate([c_ctx[None, :], c, jnp.zeros((MOD_ROWS - 1 - DEC_BATCH, D_MODEL), F32)], axis=0)
    mod_all = _modulation(cond, w_ada, b_ada)

    rope = _rope_tables()
    tables = {L: _dft_tables(L) for L in (SEQ, DEC_SEQ)}
    feats = {L: _filter_features(L) for L in (SEQ, DEC_SEQ)}
    deltas = jnp.linspace(MIN_DECAY, MAX_DECAY, D_HYENA, dtype=F32)[None, :]

    w_down_bf16 = w_down.astype(BF16)
    new_k = []
    new_v = []
    for i in range(DEPTH):
        mod = mod_all[i]
        h = _prenorm(x, norm_mix[i], mod, 0, 1)
        proj = _matmul(h, w_in, i, 1024, 1024)

        hy_w_t = hy_conv_w[i].T
        hy_b = hy_conv_b[i][None, :]
        w1p = jnp.pad(filt_w1[i], ((0, FILT_PAD - FILTER_EMB), (0, 0)))
        y_a = None
        for L, tc, nseq, rb0 in ((SEQ, 1024, BATCH, 0), (DEC_SEQ, 256, DEC_BATCH, N_CTX // DEC_SEQ)):
            cm, sm, smt = tables[L]
            ka, kb = _filter_spectrum(L, feats[L], w1p, filt_b1[i][None, :], filt_w2[i], filt_b2[i][None, :],
                                      filt_w3[i], filt_freq[i][None, :], deltas, cm, sm)
            y_a = _hyena(proj, hy_w_t, hy_b, hy_bias[i][None, :], ka, kb, cm, sm, smt,
                         gn_hyena[i][None, :], L=L, tc=tc, nseq=nseq, row_block0=rb0, dest=y_a)
        gn_a = gn_attn[i][None, :]
        y_b, k_ctx, v_ctx = _attn_ctx(proj, attn_sink[i], gn_a)
        y_b = _attn_lat(proj, cache_k[:, i].reshape(DEC_BATCH, PAST_LEN, KV_DIM),
                        cache_v[:, i].reshape(DEC_BATCH, PAST_LEN, KV_DIM), attn_sink[i], rope, gn_a, y_b)
        x = _matmul_residual([y_a, y_b], w_out, i, x, mod, 2, 512, 1024, "out_proj")

        h = _prenorm(x, norm_ffn[i], mod, 3, 4)
        ffn_w_t = ffn_conv_w[i].T
        ffn_b = ffn_conv_b[i][None, :]
        n_ffn_blocks = D_FF // FFN_TN
        acts = [_ffn_up2(h, w_up, i, ffn_w_t, ffn_b)]
        if n_ffn_blocks % 2:
            acts.append(_ffn_up(h, w_up, i, ffn_w_t, ffn_b, n_ffn_blocks - 1, 1))
        x = _matmul_residual(acts, w_down_bf16, i, x, mod, 5, 512, 512, "ffn_down")

        new_k.append(k_ctx.reshape(BATCH, SEQ, N_KV_HEADS, HEAD_DIM))
        new_v.append(v_ctx.reshape(BATCH, SEQ, N_KV_HEADS, HEAD_DIM))

    y_prompt = _final_norm(x, final_norm, 0, N_CTX).reshape(BATCH, SEQ, D_MODEL)
    y_sample = _final_norm(x, final_norm, N_CTX // NORM_TR, N_LAT).reshape(DEC_BATCH, DEC_SEQ, D_MODEL)
    return (y_prompt, y_sample, jnp.stack(new_k, axis=1), jnp.stack(new_v, axis=1))
```

```python
import functools
import math

import jax
import jax.numpy as jnp
from jax import lax
from jax.experimental import pallas as pl
from jax.experimental.pallas import tpu as pltpu

F32 = jnp.float32
BF16 = jnp.bfloat16
HIGHEST = lax.Precision.HIGHEST

D_MODEL = 4096
BATCH = 32
SEQ = 256
DEPTH = 4
DEC_BATCH = 2
DEC_SEQ = 1024
PAST_LEN = 512
GRID_W = 64
D_HYENA = D_MODEL // 2
D_ATTN = D_MODEL - D_HYENA
HEAD_DIM = 128
N_HEADS = D_ATTN // HEAD_DIM
N_KV_HEADS = 4
KV_GROUP = N_HEADS // N_KV_HEADS
KV_DIM = N_KV_HEADS * HEAD_DIM
WINDOW = 128
BLOCK = 128
ATTN_SCALE = 1.0 / math.sqrt(HEAD_DIM)
ROPE_BASE = 10000.0
FILTER_EMB = 33
FILTER_BANDS = (FILTER_EMB - 1) // 2
FILTER_ORDER = 64
DECAY_TARGET = 1e-2
MAX_DECAY = math.log(DECAY_TARGET) / 0.3
MIN_DECAY = math.log(DECAY_TARGET) / 1.5
D_FF = 11008
RMS_EPS = 1e-6
NEG_INF = -1e30
O_Q = 3 * D_HYENA
O_K = O_Q + D_ATTN
O_V = O_K + KV_DIM
D_IN = O_V + KV_DIM

N_CTX = BATCH * SEQ
N_LAT = DEC_BATCH * DEC_SEQ
N_TOK = N_CTX + N_LAT
MOD_ROWS = 8
N_MOD = 6 * D_MODEL

V7X_VMEM_BYTES = 64 * 1024 * 1024
V7X_VMEM_USABLE = 58 * 1024 * 1024
MIB = 1024 * 1024


def _vmem_limit(estimate_bytes):
    return int(min(V7X_VMEM_USABLE, estimate_bytes + estimate_bytes // 4 + 2 * MIB))


def _mod_row(row_start):
    return jnp.where(row_start < N_CTX, 0, 1 + (row_start - N_CTX) // DEC_SEQ)


def _seq_len(row_start):
    return jnp.where(row_start < N_CTX, SEQ, DEC_SEQ)


SUBLANES = 8


def _zero_rows_at(x, sub, breaks):
    pick = lax.broadcasted_iota(jnp.int32, (SUBLANES, 1), 0) == sub
    pieces = []
    at = 0
    for start in sorted(breaks):
        if start > at:
            pieces.append(x[at:start])
        cond = pick if breaks[start] is True else jnp.logical_and(pick, breaks[start])
        pieces.append(jnp.where(cond, 0.0, x[start:start + SUBLANES]))
        at = start + SUBLANES
    if at < x.shape[0]:
        pieces.append(x[at:])
    return jnp.concatenate(pieces, axis=0)


def _conv3_rows(u, w, b, seq_rows=None, split=False):
    rows = u.shape[0]
    seq_rows = rows if seq_rows is None else seq_rows
    starts = range(0, rows, seq_rows)
    up = _zero_rows_at(pltpu.roll(u, 1, 0), 0, {s: (True if s == 0 else split) for s in starts})
    un = _zero_rows_at(pltpu.roll(u, rows - 1, 0), SUBLANES - 1,
                       {s + seq_rows - SUBLANES: (True if s + seq_rows == rows else split) for s in starts})
    return up * w[0:1, :] + u * w[1:2, :] + un * w[2:3, :] + b


MOD_TN = 512


def _mod_kernel(c_ref, w_ref, b_ref, o_ref):
    c = c_ref[...]
    s = (c * jax.nn.sigmoid(c)).astype(BF16)
    o_ref[0] = jnp.dot(s, w_ref[0].astype(BF16), preferred_element_type=F32) + b_ref[0]


def _modulation(cond, w_ada, b_ada):
    est = 2 * D_MODEL * MOD_TN * 4 + D_MODEL * MOD_TN * 2 + 4 * MOD_ROWS * D_MODEL * 4
    return pl.pallas_call(
        _mod_kernel,
        name="adaln_mod",
        grid=(DEPTH, N_MOD // MOD_TN),
        in_specs=[
            pl.BlockSpec((MOD_ROWS, D_MODEL), lambda l, n: (0, 0)),
            pl.BlockSpec((1, D_MODEL, MOD_TN), lambda l, n: (l, 0, n)),
            pl.BlockSpec((1, 1, MOD_TN), lambda l, n: (l, 0, n)),
        ],
        out_specs=pl.BlockSpec((1, MOD_ROWS, MOD_TN), lambda l, n: (l, 0, n)),
        out_shape=jax.ShapeDtypeStruct((DEPTH, MOD_ROWS, N_MOD), F32),
        compiler_params=pltpu.CompilerParams(
            dimension_semantics=("arbitrary", "arbitrary"), vmem_limit_bytes=_vmem_limit(est)),
    )(cond, w_ada, b_ada.reshape(DEPTH, 1, N_MOD))


NORM_TR = 256


def _prenorm_kernel(x_ref, g_ref, shift_ref, scale_ref, o_ref):
    r = _mod_row(pl.program_id(0) * NORM_TR)
    x = x_ref[...]
    y = x * lax.rsqrt(jnp.mean(x * x, axis=-1, keepdims=True) + RMS_EPS) * g_ref[...]
    o_ref[...] = (y * (1.0 + scale_ref[pl.ds(r, 1), :]) + shift_ref[pl.ds(r, 1), :]).astype(o_ref.dtype)


def _prenorm(x, g, mod, shift_chunk, scale_chunk):
    est = 2 * NORM_TR * D_MODEL * (4 + 2) + 3 * NORM_TR * D_MODEL * 4
    return pl.pallas_call(
        _prenorm_kernel,
        name="prenorm",
        grid=(N_TOK // NORM_TR,),
        in_specs=[
            pl.BlockSpec((NORM_TR, D_MODEL), lambda i: (i, 0)),
            pl.BlockSpec((1, D_MODEL), lambda i: (0, 0)),
            pl.BlockSpec((MOD_ROWS, D_MODEL), lambda i: (0, shift_chunk)),
            pl.BlockSpec((MOD_ROWS, D_MODEL), lambda i: (0, scale_chunk)),
        ],
        out_specs=pl.BlockSpec((NORM_TR, D_MODEL), lambda i: (i, 0)),
        out_shape=jax.ShapeDtypeStruct((N_TOK, D_MODEL), BF16),
        compiler_params=pltpu.CompilerParams(
            dimension_semantics=("arbitrary",), vmem_limit_bytes=_vmem_limit(est)),
    )(x, g.reshape(1, D_MODEL), mod, mod)


def _final_norm_kernel(x_ref, g_ref, o_ref):
    x = x_ref[...]
    o_ref[...] = x * lax.rsqrt(jnp.mean(x * x, axis=-1, keepdims=True) + RMS_EPS) * g_ref[...]


def _final_norm(x, g, row_block0, rows):
    est = 4 * NORM_TR * D_MODEL * 4 + 2 * NORM_TR * D_MODEL * 4
    return pl.pallas_call(
        _final_norm_kernel,
        name="final_norm",
        grid=(rows // NORM_TR,),
        in_specs=[
            pl.BlockSpec((NORM_TR, D_MODEL), lambda i: (row_block0 + i, 0)),
            pl.BlockSpec((1, D_MODEL), lambda i: (0, 0)),
        ],
        out_specs=pl.BlockSpec((NORM_TR, D_MODEL), lambda i: (i, 0)),
        out_shape=jax.ShapeDtypeStruct((rows, D_MODEL), F32),
        compiler_params=pltpu.CompilerParams(
            dimension_semantics=("arbitrary",), vmem_limit_bytes=_vmem_limit(est)),
    )(x, g.reshape(1, D_MODEL))


W_CHUNKS = 8


class _Stream:
    def __init__(self, n_tiles, k):
        self.n_tiles = n_tiles
        self.kc = k // W_CHUNKS

    def token(self, jj, i):
        return jnp.where(jj == 0, 0, i)

    def tile(self, jj):
        return jnp.maximum(jj - 1, 0)

    def chunk_spec(self, layer, width, col_block):
        last = W_CHUNKS - 1

        def index(jj, i):
            arriving = jnp.minimum(jj, self.n_tiles - 1)
            chunk = jnp.where(jj < self.n_tiles, jnp.minimum(i, last), last)
            return layer, chunk, col_block(arriving)
        return pl.BlockSpec((None, self.kc, width), index)

    def cast_chunk(self, chunk_refs, wb_ref):
        jj = pl.program_id(0)
        i = pl.program_id(1)

        @pl.when(jnp.logical_and(jj < self.n_tiles, i < W_CHUNKS))
        def _():
            rows = pl.ds(pl.multiple_of(i * self.kc, self.kc), self.kc)
            lane = 0
            for ref in chunk_refs:
                width = ref.shape[-1]
                wb_ref[jj % 2, rows, lane:lane + width] = ref[...].astype(BF16)
                lane += width

    def multiplying(self):
        return pl.program_id(0) >= 1

    def weights(self, wb_ref):
        return wb_ref.at[(pl.program_id(0) + 1) % 2]


def _matmul_kernel(x_ref, w_ref, o_ref, wb_ref, *, stream):
    stream.cast_chunk([w_ref], wb_ref)
    w = stream.weights(wb_ref)

    @pl.when(stream.multiplying())
    def _():
        o_ref[...] = jnp.dot(x_ref[...], w[...], preferred_element_type=F32)


def _matmul(x, w_stack, layer, tm, tn):
    m, k = x.shape
    n = w_stack.shape[2]
    assert m // tm > W_CHUNKS
    stream = _Stream(n // tn, k)
    est = 2 * (tm * k * 2 + tm * tn * 4 + stream.kc * tn * 4) + 2 * k * tn * 2 + tm * tn * 4
    return pl.pallas_call(
        functools.partial(_matmul_kernel, stream=stream),
        name="in_proj",
        grid=(n // tn + 1, m // tm),
        in_specs=[
            pl.BlockSpec((tm, k), lambda jj, i: (stream.token(jj, i), 0)),
            stream.chunk_spec(layer, tn, lambda t: t),
        ],
        out_specs=pl.BlockSpec((tm, tn), lambda jj, i: (stream.token(jj, i), stream.tile(jj))),
        out_shape=jax.ShapeDtypeStruct((m, n), F32),
        scratch_shapes=[pltpu.VMEM((2, k, tn), BF16)],
        compiler_params=pltpu.CompilerParams(
            dimension_semantics=("arbitrary", "arbitrary"), vmem_limit_bytes=_vmem_limit(est)),
    )(x, w_stack)


def _residual_update(lhs, w, res_ref, gate_ref, o_ref, mod_row):
    y = None
    k0 = 0
    for x_ref in lhs:
        kk = x_ref.shape[1]
        part = jnp.dot(x_ref[...], w[k0:k0 + kk, :], preferred_element_type=F32)
        y = part if y is None else y + part
        k0 += kk
    o_ref[...] = res_ref[...] + gate_ref[pl.ds(mod_row, 1), :] * y


def _matmul_residual_kernel(*refs, n_lhs, tm):
    w_ref, res_ref, gate_ref, o_ref = refs[n_lhs:]
    _residual_update(refs[:n_lhs], w_ref, res_ref, gate_ref, o_ref, _mod_row(pl.program_id(1) * tm))


def _matmul_residual_stream_kernel(*refs, n_lhs, tm, stream):
    w_ref, res_ref, gate_ref, o_ref, wb_ref = refs[n_lhs:]
    stream.cast_chunk([w_ref], wb_ref)
    mod_row = _mod_row(pl.program_id(1) * tm)
    w = stream.weights(wb_ref)

    @pl.when(stream.multiplying())
    def _():
        _residual_update(refs[:n_lhs], w, res_ref, gate_ref, o_ref, mod_row)


def _matmul_residual(lhs, w_stack, layer, res, mod, gate_chunk, tm, tn, name):
    m = res.shape[0]
    _, k, n = w_stack.shape
    gate_block0 = gate_chunk * (D_MODEL // tn)
    io_bytes = 2 * (tm * k * 2 + 2 * tm * tn * 4 + MOD_ROWS * tn * 4) + 2 * tm * tn * 4
    if w_stack.dtype == BF16:
        token = lambda j, i: i
        tile = lambda j: j
        grid = (n // tn, m // tm)
        w_spec = pl.BlockSpec((None, k, tn), lambda j, i: (layer, 0, j))
        body = functools.partial(_matmul_residual_kernel, n_lhs=len(lhs), tm=tm)
        scratch = []
        est = io_bytes + 2 * k * tn * 2
    else:
        assert m // tm > W_CHUNKS
        stream = _Stream(n // tn, k)
        token, tile = stream.token, stream.tile
        grid = (n // tn + 1, m // tm)
        w_spec = stream.chunk_spec(layer, tn, lambda t: t)
        body = functools.partial(_matmul_residual_stream_kernel, n_lhs=len(lhs), tm=tm, stream=stream)
        scratch = [pltpu.VMEM((2, k, tn), BF16)]
        est = io_bytes + 2 * k * tn * 2 + 2 * stream.kc * tn * 4
    return pl.pallas_call(
        body,
        name=name,
        grid=grid,
        in_specs=[pl.BlockSpec((tm, x.shape[1]), lambda j, i: (token(j, i), 0)) for x in lhs] + [
            w_spec,
            pl.BlockSpec((tm, tn), lambda j, i: (token(j, i), tile(j))),
            pl.BlockSpec((MOD_ROWS, tn), lambda j, i: (0, gate_block0 + tile(j))),
        ],
        out_specs=pl.BlockSpec((tm, tn), lambda j, i: (token(j, i), tile(j))),
        out_shape=jax.ShapeDtypeStruct((m, n), F32),
        scratch_shapes=scratch,
        compiler_params=pltpu.CompilerParams(
            dimension_semantics=("arbitrary", "arbitrary"), vmem_limit_bytes=_vmem_limit(est)),
    )(*lhs, w_stack, res, mod)


FFN_TM = 1024
FFN_TN = 256


def _ffn_up_kernel(h_ref, wg_ref, wv_ref, cwg_ref, cwv_ref, cbg_ref, cbv_ref, o_ref, wb_ref):
    @pl.when(pl.program_id(1) == 0)
    def _():
        wb_ref[:, :FFN_TN] = wg_ref[...].astype(BF16)
        wb_ref[:, FFN_TN:] = wv_ref[...].astype(BF16)

    u = jnp.dot(h_ref[...], wb_ref[...], preferred_element_type=F32)
    is_ctx = pl.program_id(1) * FFN_TM < N_CTX
    g = _conv3_rows(u[:, :FFN_TN], cwg_ref[...], cbg_ref[...], SEQ, is_ctx)
    v = _conv3_rows(u[:, FFN_TN:], cwv_ref[...], cbv_ref[...], SEQ, is_ctx)
    o_ref[...] = (g * jax.nn.sigmoid(g) * v).astype(o_ref.dtype)


FFN_PAIR = 2 * FFN_TN


def _ffn_up2_kernel(h_ref, wg_ref, wva_ref, wvb_ref, cwg_ref, cwva_ref, cwvb_ref, cbg_ref, cbva_ref, cbvb_ref,
                    o_ref, wb_ref, *, stream):
    stream.cast_chunk([wg_ref, wva_ref, wvb_ref], wb_ref)
    is_ctx = pl.program_id(1) * FFN_TM < N_CTX
    w = stream.weights(wb_ref)

    @pl.when(stream.multiplying())
    def _():
        u = jnp.dot(h_ref[...], w[...], preferred_element_type=F32)
        cwg = cwg_ref[...]
        cbg = cbg_ref[...]
        for blk, (cwv_ref, cbv_ref) in enumerate(((cwva_ref, cbva_ref), (cwvb_ref, cbvb_ref))):
            lanes = slice(blk * FFN_TN, (blk + 1) * FFN_TN)
            g = _conv3_rows(u[:, lanes], cwg[:, lanes], cbg[:, lanes], SEQ, is_ctx)
            v = _conv3_rows(u[:, FFN_PAIR + blk * FFN_TN:FFN_PAIR + (blk + 1) * FFN_TN], cwv_ref[...],
                            cbv_ref[...], SEQ, is_ctx)
            o_ref[:, lanes] = (g * jax.nn.sigmoid(g) * v).astype(o_ref.dtype)


def _ffn_up2(h, w_up, layer, conv_w_t, conv_b):
    nb = D_FF // FFN_TN
    npair = nb // 2
    n_m = N_TOK // FFN_TM
    assert n_m > W_CHUNKS
    stream = _Stream(npair, D_MODEL)
    est = (2 * (FFN_TM * D_MODEL * 2 + FFN_TM * FFN_PAIR * 2 + stream.kc * 2 * FFN_PAIR * 4)
           + 2 * D_MODEL * 2 * FFN_PAIR * 2 + 12 * FFN_TM * FFN_PAIR * 4)
    pair = lambda rows: pl.BlockSpec((rows, FFN_PAIR), lambda jj, i: (0, stream.tile(jj)))
    val = lambda rows, blk: pl.BlockSpec((rows, FFN_TN), lambda jj, i: (0, nb + 2 * stream.tile(jj) + blk))
    return pl.pallas_call(
        functools.partial(_ffn_up2_kernel, stream=stream),
        name="ffn_up2",
        grid=(npair + 1, n_m),
        in_specs=[
            pl.BlockSpec((FFN_TM, D_MODEL), lambda jj, i: (stream.token(jj, i), 0)),
            stream.chunk_spec(layer, FFN_PAIR, lambda t: t),
            stream.chunk_spec(layer, FFN_TN, lambda t: nb + 2 * t),
            stream.chunk_spec(layer, FFN_TN, lambda t: nb + 2 * t + 1),
            pair(3), val(3, 0), val(3, 1), pair(1), val(1, 0), val(1, 1),
        ],
        out_specs=pl.BlockSpec((FFN_TM, FFN_PAIR), lambda jj, i: (stream.token(jj, i), stream.tile(jj))),
        out_shape=jax.ShapeDtypeStruct((N_TOK, npair * FFN_PAIR), BF16),
        scratch_shapes=[pltpu.VMEM((2, D_MODEL, 2 * FFN_PAIR), BF16)],
        compiler_params=pltpu.CompilerParams(
            dimension_semantics=("arbitrary", "arbitrary"), vmem_limit_bytes=_vmem_limit(est)),
    )(h, w_up, w_up, w_up, conv_w_t, conv_w_t, conv_w_t, conv_b, conv_b, conv_b)


def _ffn_up(h, w_up, layer, conv_w_t, conv_b, block0, n_blocks):
    nb = D_FF // FFN_TN
    est = (2 * (FFN_TM * D_MODEL * 2 + 2 * D_MODEL * FFN_TN * 4 + FFN_TM * FFN_TN * 2)
           + 2 * D_MODEL * FFN_TN * 2 + 10 * FFN_TM * FFN_TN * 4)
    return pl.pallas_call(
        _ffn_up_kernel,
        name="ffn_up",
        grid=(n_blocks, N_TOK // FFN_TM),
        in_specs=[
            pl.BlockSpec((FFN_TM, D_MODEL), lambda j, i: (i, 0)),
            pl.BlockSpec((None, D_MODEL, FFN_TN), lambda j, i: (layer, 0, block0 + j)),
            pl.BlockSpec((None, D_MODEL, FFN_TN), lambda j, i: (layer, 0, nb + block0 + j)),
            pl.BlockSpec((3, FFN_TN), lambda j, i: (0, block0 + j)),
            pl.BlockSpec((3, FFN_TN), lambda j, i: (0, nb + block0 + j)),
            pl.BlockSpec((1, FFN_TN), lambda j, i: (0, block0 + j)),
            pl.BlockSpec((1, FFN_TN), lambda j, i: (0, nb + block0 + j)),
        ],
        out_specs=pl.BlockSpec((FFN_TM, FFN_TN), lambda j, i: (i, j)),
        out_shape=jax.ShapeDtypeStruct((N_TOK, n_blocks * FFN_TN), BF16),
        scratch_shapes=[pltpu.VMEM((D_MODEL, 2 * FFN_TN), BF16)],
        compiler_params=pltpu.CompilerParams(
            dimension_semantics=("arbitrary", "arbitrary"), vmem_limit_bytes=_vmem_limit(est)),
    )(h, w_up, w_up, conv_w_t, conv_w_t, conv_b, conv_b)


FILT_PAD = 64


def _split_bf16(a):
    hi = a.astype(BF16)
    return hi, (a - hi.astype(F32)).astype(BF16)


def _dot_split(t_hi, t_lo, b):
    b_hi, b_lo = _split_bf16(b)
    return (jnp.dot(t_hi, b_hi, preferred_element_type=F32)
            + jnp.dot(t_hi, b_lo, preferred_element_type=F32)
            + jnp.dot(t_lo, b_hi, preferred_element_type=F32))


def _dft_tables(L):
    f = jnp.arange(L, dtype=jnp.int32)[:, None]
    s = jnp.arange(L, dtype=jnp.int32)[None, :]
    ang = ((f * s) % (2 * L)).astype(F32) * (math.pi / L)
    cm = jnp.cos(ang)
    nyq = jnp.where(s % 2 == 0, 1.0, -1.0).astype(F32)
    sm = jnp.where(f == 0, nyq, -jnp.sin(ang))
    return tuple(jnp.stack(_split_bf16(t)) for t in (cm, sm, sm.T))


def _filter_features(L):
    t = jnp.linspace(0.0, 1.0, L, dtype=F32)[:, None]
    w = 2.0 * math.pi * jnp.arange(L, dtype=F32)[:, None] / L
    f = jnp.linspace(1e-4, FILTER_BANDS - 1, FILTER_BANDS, dtype=F32)[None, :]
    z = jnp.concatenate([t, jnp.cos(f * w), -jnp.sin(f * w)], axis=-1)
    return jnp.pad(z, ((0, 0), (0, FILT_PAD - FILTER_EMB)))


def _filter_kernel(z_ref, w1_ref, b1_ref, w2_ref, b2_ref, w3f_ref, w3b_ref, freq_ref, delta_ref,
                   cm_ref, sm_ref, p_ref, q_ref, r_ref, hid_ref, *, L):
    z = z_ref[...]

    @pl.when(pl.program_id(0) == 0)
    def _():
        fr = freq_ref[...]
        h = jnp.sin(fr * (jnp.dot(z, w1_ref[...], precision=HIGHEST, preferred_element_type=F32) + b1_ref[...]))
        hid_ref[...] = jnp.sin(
            fr * (jnp.dot(h, w2_ref[...], precision=HIGHEST, preferred_element_type=F32) + b2_ref[...]))

    h = hid_ref[...]
    decay = jnp.exp(-z[:, 0:1] * jnp.abs(delta_ref[...]))
    h_fwd = jnp.dot(h, w3f_ref[...], precision=HIGHEST, preferred_element_type=F32) * decay
    h_bwd = jnp.dot(h, w3b_ref[...], precision=HIGHEST, preferred_element_type=F32) * decay
    pos = lax.broadcasted_iota(jnp.int32, (L, 1), 0)
    first = pos == 0
    h_bwd = jnp.where(first, 0.0, h_bwd)
    even = h_fwd + h_bwd
    odd = h_fwd - h_bwd
    sign = jnp.where((pos & 1) == 0, 1.0, -1.0)
    nyq = jnp.sum(sign * even, axis=0, keepdims=True)
    wgt = jnp.where(first, 1.0 / (2 * L), 1.0 / L)
    p = _dot_split(cm_ref[0], cm_ref[1], even) * wgt
    p_ref[...] = p
    q_ref[...] = jnp.where(first, 0.0, _dot_split(sm_ref[0], sm_ref[1], odd) * wgt)
    r_ref[...] = jnp.where(first, nyq * wgt, p)


FILT_TC = 256


def _filter_spectrum(L, z, w1p, b1, w2, b2, w3, freq, deltas, cm, sm):
    nb = D_HYENA // FILT_TC
    est = 2 * (2 * 2 * L * L * 2) + 2 * 4 * L * FILT_TC * 4 + 10 * L * FILT_TC * 4 + 4 * L * FILT_PAD * 4
    full = lambda shape: pl.BlockSpec(shape, lambda c: (0,) * len(shape))
    return pl.pallas_call(
        functools.partial(_filter_kernel, L=L),
        name=f"hyena_filter_{L}",
        grid=(nb,),
        in_specs=[
            full((L, FILT_PAD)), full((FILT_PAD, FILTER_ORDER)), full((1, FILTER_ORDER)),
            full((FILTER_ORDER, FILTER_ORDER)), full((1, FILTER_ORDER)),
            pl.BlockSpec((FILTER_ORDER, FILT_TC), lambda c: (0, c)),
            pl.BlockSpec((FILTER_ORDER, FILT_TC), lambda c: (0, nb + c)),
            full((1, FILTER_ORDER)),
            pl.BlockSpec((1, FILT_TC), lambda c: (0, c)),
            full((2, L, L)), full((2, L, L)),
        ],
        out_specs=[pl.BlockSpec((L, FILT_TC), lambda c: (0, c))] * 3,
        out_shape=[jax.ShapeDtypeStruct((L, D_HYENA), F32)] * 3,
        scratch_shapes=[pltpu.VMEM((L, FILTER_ORDER), F32)],
        compiler_params=pltpu.CompilerParams(
            dimension_semantics=("arbitrary",), vmem_limit_bytes=_vmem_limit(est)),
    )(z, w1p, b1, w2, b2, w3, w3, freq, deltas, cm, sm)


def _hyena_kernel(x0_ref, x1_ref, v_ref, cw0_ref, cw1_ref, cw2_ref, cb0_ref, cb1_ref, cb2_ref, hb_ref,
                  p_ref, q_ref, r_ref, cm_ref, sm_ref, smt_ref, gn_ref, o_ref, acc_ref, *, L, tc, nct,
                  grid_ids=None):
    ct = pl.program_id(1) if grid_ids is None else grid_ids[1]
    x0 = _conv3_rows(x0_ref[...], cw0_ref[...], cb0_ref[...])
    x1 = _conv3_rows(x1_ref[...], cw1_ref[...], cb1_ref[...])
    v = _conv3_rows(v_ref[...], cw2_ref[...], cb2_ref[...])
    u = x1 * v
    ub = u.astype(BF16)
    top = jnp.dot(cm_ref[...], ub, preferred_element_type=F32)
    bot = jnp.dot(sm_ref[...], ub, preferred_element_type=F32)
    q = q_ref[...]
    y_top = top * p_ref[...] - bot * q
    y_bot = top * q + bot * r_ref[...]
    y = (jnp.dot(cm_ref[...], y_top.astype(BF16), preferred_element_type=F32)
         + jnp.dot(smt_ref[...], y_bot.astype(BF16), preferred_element_type=F32))
    acc_ref[ct] = x0 * (y + u * hb_ref[...])

    @pl.when(ct == nct - 1)
    def _():
        ss = jnp.zeros((L, 1), F32)
        for j in range(nct):
            a = acc_ref[j]
            ss = ss + jnp.sum(a * a, axis=-1, keepdims=True)
        inv = lax.rsqrt(ss * (1.0 / D_HYENA) + RMS_EPS)
        for j in range(nct):
            o_ref[:, j * tc:(j + 1) * tc] = (acc_ref[j] * inv * gn_ref[:, j * tc:(j + 1) * tc]).astype(o_ref.dtype)


def _with_fill(kernel_fn, n_real, out_pos, n_axes):
    def body(*refs):
        grid_ids = tuple(pl.program_id(a) for a in range(n_axes))

        @pl.when(grid_ids[0] < n_real)
        def _():
            kernel_fn(*refs, grid_ids=grid_ids)

        @pl.when(grid_ids[0] >= n_real)
        def _():
            refs[out_pos][...] = jnp.zeros_like(refs[out_pos])
    return body


def _into_rows(kernel_fn, dest):
    if dest is None:
        return kernel_fn, [], [], {}

    def body(dest_ref, *refs):
        del dest_ref
        kernel_fn(*refs)
    return body, [pl.BlockSpec(memory_space=pl.ANY)], [dest], {0: 0}


def _hyena(proj, conv_w_t, conv_b, hy_bias, spectrum, cm, sm, smt, gn, *, L, tc, nseq, row_block0, dest=None):
    nct = D_HYENA // tc
    cpb = D_HYENA // tc
    est = (3 * L * L * 2 + 2 * (3 * L * tc * 4 + 3 * L * tc * 4) + nct * L * tc * 4
           + 2 * L * D_HYENA * 2 + 14 * L * tc * 4)
    once = pl.Buffered(1)
    n_fill = N_TOK // L - row_block0 - nseq if dest is None else 0
    grp = lambda g: pl.BlockSpec((L, tc), lambda s, c: (row_block0 + jnp.minimum(s, nseq - 1), g * cpb + c))
    tap = lambda g: pl.BlockSpec((3, tc), lambda s, c: (0, g * cpb + c))
    bias = lambda g: pl.BlockSpec((1, tc), lambda s, c: (0, g * cpb + c))
    col = lambda rows: pl.BlockSpec((rows, tc), lambda s, c: (0, c))
    table = pl.BlockSpec((None, L, L), lambda s, c: (0, 0, 0), pipeline_mode=once)
    kernel_fn = functools.partial(_hyena_kernel, L=L, tc=tc, nct=nct)
    if n_fill:
        kernel_fn = _with_fill(kernel_fn, nseq, -2, 2)
    body, dest_specs, dest_args, aliases = _into_rows(kernel_fn, dest)
    return pl.pallas_call(
        body,
        name=f"hyena_conv_{L}",
        grid=(nseq + n_fill, nct),
        in_specs=dest_specs + [grp(0), grp(1), grp(2), tap(0), tap(1), tap(2), bias(0), bias(1), bias(2),
                               col(1), col(L), col(L), col(L), table, table, table,
                               pl.BlockSpec((1, D_HYENA), lambda s, c: (0, 0))],
        out_specs=pl.BlockSpec((L, D_HYENA), lambda s, c: (row_block0 + s, 0)),
        out_shape=jax.ShapeDtypeStruct((N_TOK, D_HYENA), BF16),
        input_output_aliases=aliases,
        scratch_shapes=[pltpu.VMEM((nct, L, tc), F32)],
        compiler_params=pltpu.CompilerParams(
            dimension_semantics=("arbitrary", "arbitrary"), vmem_limit_bytes=_vmem_limit(est)),
    )(*dest_args, proj, proj, proj, conv_w_t, conv_w_t, conv_w_t, conv_b, conv_b, conv_b,
      hy_bias, *spectrum, cm, sm, smt, gn)


def _dot_nt(a, b):
    return lax.dot_general(a, b, (((1,), (1,)), ((), ())), preferred_element_type=F32)


def _head_lanes(kvh, g):
    hd = kvh * KV_GROUP + g
    return slice(hd * HEAD_DIM, (hd + 1) * HEAD_DIM)


def _group_sinks(sink_ref, kvh, rows):
    return jnp.concatenate([jnp.full((rows, 1), sink_ref[kvh * KV_GROUP + g], F32) for g in range(KV_GROUP)],
                           axis=0)


def _head_norm_store(acc_ref, gn_ref, o_ref):
    y = acc_ref[...]
    inv = lax.rsqrt(jnp.mean(y * y, axis=-1, keepdims=True) + RMS_EPS)
    o_ref[...] = (y * inv * gn_ref[...]).astype(o_ref.dtype)


def _attn_ctx_kernel(sink_ref, q_ref, k_ref, v_ref, gn_ref, o_ref, ko_ref, vo_ref, acc_ref, grid_ids=None):
    del grid_ids
    ko_ref[...] = k_ref[...]
    vo_ref[...] = v_ref[...]
    for kvh in range(N_KV_HEADS):
        ksl = slice(kvh * HEAD_DIM, (kvh + 1) * HEAD_DIM)
        kh = k_ref[:, ksl].astype(BF16)
        vh = v_ref[:, ksl].astype(BF16)
        for g in range(KV_GROUP):
            hsl = _head_lanes(kvh, g)
            s = _dot_nt(q_ref[:, hsl].astype(BF16), kh) * ATTN_SCALE
            sk = sink_ref[kvh * KV_GROUP + g]
            m = jnp.maximum(jnp.max(s, axis=-1, keepdims=True), sk)
            e = jnp.exp(s - m)
            den = jnp.sum(e, axis=-1, keepdims=True) + jnp.exp(sk - m)
            acc_ref[:, hsl] = jnp.dot((e / den).astype(BF16), vh, preferred_element_type=F32)
    _head_norm_store(acc_ref, gn_ref, o_ref)


def _attn_ctx(proj, sink, gn):
    est = 2 * (SEQ * D_ATTN * 4 + 4 * SEQ * KV_DIM * 4 + SEQ * D_ATTN * 2) + 3 * SEQ * D_ATTN * 4
    n_fill = N_LAT // SEQ
    real = lambda b: jnp.minimum(b, BATCH - 1)
    return pl.pallas_call(
        _with_fill(_attn_ctx_kernel, BATCH, -4, 1),
        name="attn_ctx",
        grid=(BATCH + n_fill,),
        in_specs=[
            pl.BlockSpec(memory_space=pltpu.SMEM),
            pl.BlockSpec((SEQ, D_ATTN), lambda b: (real(b), O_Q // D_ATTN)),
            pl.BlockSpec((SEQ, KV_DIM), lambda b: (real(b), O_K // KV_DIM)),
            pl.BlockSpec((SEQ, KV_DIM), lambda b: (real(b), O_V // KV_DIM)),
            pl.BlockSpec((1, D_ATTN), lambda b: (0, 0)),
        ],
        out_specs=[pl.BlockSpec((SEQ, D_ATTN), lambda b: (b, 0)),
                   pl.BlockSpec((SEQ, KV_DIM), lambda b: (real(b), 0)),
                   pl.BlockSpec((SEQ, KV_DIM), lambda b: (real(b), 0))],
        out_shape=[jax.ShapeDtypeStruct((N_TOK, D_ATTN), BF16),
                   jax.ShapeDtypeStruct((N_CTX, KV_DIM), F32),
                   jax.ShapeDtypeStruct((N_CTX, KV_DIM), F32)],
        scratch_shapes=[pltpu.VMEM((SEQ, D_ATTN), F32)],
        compiler_params=pltpu.CompilerParams(
            dimension_semantics=("arbitrary",), vmem_limit_bytes=_vmem_limit(est)),
    )(sink, proj, proj, proj, gn)


def _rope_tables():
    half = HEAD_DIM // 2
    quarter = half // 2
    pos = jnp.arange(DEC_SEQ, dtype=jnp.int32)
    row_pos = (pos // GRID_W).astype(F32)
    col_pos = (pos % GRID_W).astype(F32)
    inv = ROPE_BASE ** (-jnp.arange(quarter, dtype=F32) / quarter)
    lane = jnp.arange(HEAD_DIM, dtype=jnp.int32)
    p = jnp.where((lane < half)[None, :], row_pos[:, None], col_pos[:, None])
    ang = p * inv[lane % quarter][None, :]
    low = ((lane % half) < quarter)[None, :]
    cos = jnp.cos(ang)
    sin = jnp.sin(ang)
    return cos, jnp.where(low, -sin, 0.0), jnp.where(low, 0.0, sin)


ROPE_SHIFT = HEAD_DIM // 4
LAT_SPAN = BLOCK + 2 * WINDOW


def _rope(x, cos, sin_lo, sin_hi):
    return (x * cos + pltpu.roll(x, HEAD_DIM - ROPE_SHIFT, 1) * sin_lo
            + pltpu.roll(x, ROPE_SHIFT, 1) * sin_hi)


def _attn_lat_kernel(sink_ref, q_ref, k_ref, v_ref, ck_ref, cv_ref, cos_ref, slo_ref, shi_ref, gn_ref,
                     o_ref, acc_ref):
    i = pl.program_id(1)
    q0 = pl.multiple_of(i * BLOCK, BLOCK)
    k0 = pl.multiple_of(jnp.clip((i - 1) * BLOCK, 0, DEC_SEQ - LAT_SPAN), BLOCK)
    q_tab = (cos_ref[pl.ds(q0, BLOCK), :], slo_ref[pl.ds(q0, BLOCK), :], shi_ref[pl.ds(q0, BLOCK), :])
    k_tab = (cos_ref[pl.ds(k0, LAT_SPAN), :], slo_ref[pl.ds(k0, LAT_SPAN), :], shi_ref[pl.ds(k0, LAT_SPAN), :])
    q_pos = q0 + lax.broadcasted_iota(jnp.int32, (BLOCK, 1), 0)
    k_pos = k0 + lax.broadcasted_iota(jnp.int32, (1, LAT_SPAN), 1)
    valid = jnp.abs(q_pos - k_pos) <= WINDOW
    valid = jnp.concatenate([valid] * KV_GROUP, axis=0)
    for kvh in range(N_KV_HEADS):
        ksl = slice(kvh * HEAD_DIM, (kvh + 1) * HEAD_DIM)
        kw = _rope(k_ref[pl.ds(k0, LAT_SPAN), ksl], *k_tab).astype(BF16)
        vw = v_ref[pl.ds(k0, LAT_SPAN), ksl].astype(BF16)
        kc = ck_ref[0, :, ksl].astype(BF16)
        vc = cv_ref[0, :, ksl].astype(BF16)
        q = jnp.concatenate([_rope(q_ref[:, _head_lanes(kvh, g)], *q_tab) for g in range(KV_GROUP)],
                            axis=0).astype(BF16)
        sk = _group_sinks(sink_ref, kvh, BLOCK)
        s_lat = jnp.where(valid, _dot_nt(q, kw) * ATTN_SCALE, NEG_INF)
        s_ctx = _dot_nt(q, kc) * ATTN_SCALE
        m = jnp.maximum(jnp.maximum(jnp.max(s_lat, axis=-1, keepdims=True),
                                    jnp.max(s_ctx, axis=-1, keepdims=True)), sk)
        e_lat = jnp.exp(s_lat - m)
        e_ctx = jnp.exp(s_ctx - m)
        den = (jnp.sum(e_lat, axis=-1, keepdims=True) + jnp.sum(e_ctx, axis=-1, keepdims=True)
               + jnp.exp(sk - m))
        o = (jnp.dot((e_lat / den).astype(BF16), vw, preferred_element_type=F32)
             + jnp.dot((e_ctx / den).astype(BF16), vc, preferred_element_type=F32))
        for g in range(KV_GROUP):
            acc_ref[:, _head_lanes(kvh, g)] = o[g * BLOCK:(g + 1) * BLOCK]
    _head_norm_store(acc_ref, gn_ref, o_ref)


def _attn_lat(proj, cache_k, cache_v, sink, rope, gn, dest):
    nqb = DEC_SEQ // BLOCK
    est = (2 * (BLOCK * D_ATTN * 4 + 2 * DEC_SEQ * KV_DIM * 4 + 2 * PAST_LEN * KV_DIM * 4
                + 3 * DEC_SEQ * HEAD_DIM * 4 + BLOCK * D_ATTN * 2) + 8 * BLOCK * D_ATTN * 4)
    lat_row_block0 = N_CTX // BLOCK
    lat_seq_block0 = N_CTX // DEC_SEQ
    tab = pl.BlockSpec((DEC_SEQ, HEAD_DIM), lambda b, i: (0, 0))
    body, dest_specs, dest_args, aliases = _into_rows(_attn_lat_kernel, dest)
    return pl.pallas_call(
        body,
        name="attn_lat",
        grid=(DEC_BATCH, nqb),
        input_output_aliases=aliases,
        in_specs=dest_specs + [
            pl.BlockSpec(memory_space=pltpu.SMEM),
            pl.BlockSpec((BLOCK, D_ATTN), lambda b, i: (lat_row_block0 + b * nqb + i, O_Q // D_ATTN)),
            pl.BlockSpec((DEC_SEQ, KV_DIM), lambda b, i: (lat_seq_block0 + b, O_K // KV_DIM)),
            pl.BlockSpec((DEC_SEQ, KV_DIM), lambda b, i: (lat_seq_block0 + b, O_V // KV_DIM)),
            pl.BlockSpec((1, PAST_LEN, KV_DIM), lambda b, i: (b, 0, 0)),
            pl.BlockSpec((1, PAST_LEN, KV_DIM), lambda b, i: (b, 0, 0)),
            tab, tab, tab,
            pl.BlockSpec((1, D_ATTN), lambda b, i: (0, 0)),
        ],
        out_specs=pl.BlockSpec((BLOCK, D_ATTN), lambda b, i: (lat_row_block0 + b * nqb + i, 0)),
        out_shape=jax.ShapeDtypeStruct((N_TOK, D_ATTN), BF16),
        scratch_shapes=[pltpu.VMEM((BLOCK, D_ATTN), F32)],
        compiler_params=pltpu.CompilerParams(
            dimension_semantics=("arbitrary", "arbitrary"), vmem_limit_bytes=_vmem_limit(est)),
    )(*dest_args, sink, proj, proj, proj, cache_k, cache_v, *rope, gn)


def kernel(x_prompt, x_sample, cache_k, cache_v, c, c_ctx, w_ada, b_ada, norm_mix, w_in, hy_conv_w, hy_conv_b, filt_w1, filt_b1, filt_w2, filt_b2, filt_w3, filt_freq, hy_bias, attn_sink, gn_hyena, gn_attn, w_out, norm_ffn, w_up, ffn_conv_w, ffn_conv_b, w_down, final_norm):
    x = jnp.concatenate([x_prompt.reshape(N_CTX, D_MODEL), x_sample.reshape(N_LAT, D_MODEL)], axis=0)
    cond = jnp.concatenate([c_ctx[None, :], c, jnp.zeros((MOD_ROWS - 1 - DEC_BATCH, D_MODEL), F32)], axis=0)
    mod_all = _modulation(cond, w_ada, b_ada)

    rope = _rope_tables()
    tables = {L: _dft_tables(L) for L in (SEQ, DEC_SEQ)}
    feats = {L: _filter_features(L) for L in (SEQ, DEC_SEQ)}
    deltas = jnp.linspace(MIN_DECAY, MAX_DECAY, D_HYENA, dtype=F32)[None, :]

    w_down_bf16 = w_down.astype(BF16)
    new_k = []
    new_v = []
    for i in range(DEPTH):
        mod = mod_all[i]
        h = _prenorm(x, norm_mix[i], mod, 0, 1)
        proj = _matmul(h, w_in, i, 1024, 1024)

        hy_w_t = hy_conv_w[i].T
        hy_b = hy_conv_b[i][None, :]
        w1p = jnp.pad(filt_w1[i], ((0, FILT_PAD - FILTER_EMB), (0, 0)))
        y_a = None
        for L, tc, nseq, rb0 in ((SEQ, 1024, BATCH, 0), (DEC_SEQ, 256, DEC_BATCH, N_CTX // DEC_SEQ)):
            cm, sm, smt = tables[L]
            spectrum = _filter_spectrum(L, feats[L], w1p, filt_b1[i][None, :], filt_w2[i], filt_b2[i][None, :],
                                        filt_w3[i], filt_freq[i][None, :], deltas, cm, sm)
            y_a = _hyena(proj, hy_w_t, hy_b, hy_bias[i][None, :], spectrum, cm, sm, smt,
                         gn_hyena[i][None, :], L=L, tc=tc, nseq=nseq, row_block0=rb0, dest=y_a)
        gn_a = gn_attn[i][None, :]
        y_b, k_ctx, v_ctx = _attn_ctx(proj, attn_sink[i], gn_a)
        y_b = _attn_lat(proj, cache_k[:, i].reshape(DEC_BATCH, PAST_LEN, KV_DIM),
                        cache_v[:, i].reshape(DEC_BATCH, PAST_LEN, KV_DIM), attn_sink[i], rope, gn_a, y_b)
        x = _matmul_residual([y_a, y_b], w_out, i, x, mod, 2, 512, 1024, "out_proj")

        h = _prenorm(x, norm_ffn[i], mod, 3, 4)
        ffn_w_t = ffn_conv_w[i].T
        ffn_b = ffn_conv_b[i][None, :]
        n_ffn_blocks = D_FF // FFN_TN
        acts = [_ffn_up2(h, w_up, i, ffn_w_t, ffn_b)]
        if n_ffn_blocks % 2:
            acts.append(_ffn_up(h, w_up, i, ffn_w_t, ffn_b, n_ffn_blocks - 1, 1))
        x = _matmul_residual(acts, w_down_bf16, i, x, mod, 5, 512, 512, "ffn_down")

        new_k.append(k_ctx.reshape(BATCH, SEQ, N_KV_HEADS, HEAD_DIM))
        new_v.append(v_ctx.reshape(BATCH, SEQ, N_KV_HEADS, HEAD_DIM))

    y_prompt = _final_norm(x, final_norm, 0, N_CTX).reshape(BATCH, SEQ, D_MODEL)
    y_sample = _final_norm(x, final_norm, N_CTX // NORM_TR, N_LAT).reshape(DEC_BATCH, DEC_SEQ, D_MODEL)
    return (y_prompt, y_sample, jnp.stack(new_k, axis=1), jnp.stack(new_v, axis=1))
```

```python
import functools
import math

import jax
import jax.numpy as jnp
from jax import lax
from jax.experimental import pallas as pl
from jax.experimental.pallas import tpu as pltpu

F32 = jnp.float32
BF16 = jnp.bfloat16
HIGHEST = lax.Precision.HIGHEST

D_MODEL = 4096
BATCH = 32
SEQ = 256
DEPTH = 4
DEC_BATCH = 2
DEC_SEQ = 1024
PAST_LEN = 512
GRID_W = 64
D_HYENA = D_MODEL // 2
D_ATTN = D_MODEL - D_HYENA
HEAD_DIM = 128
N_HEADS = D_ATTN // HEAD_DIM
N_KV_HEADS = 4
KV_GROUP = N_HEADS // N_KV_HEADS
KV_DIM = N_KV_HEADS * HEAD_DIM
WINDOW = 128
BLOCK = 128
ATTN_SCALE = 1.0 / math.sqrt(HEAD_DIM)
ROPE_BASE = 10000.0
FILTER_EMB = 33
FILTER_BANDS = (FILTER_EMB - 1) // 2
FILTER_ORDER = 64
DECAY_TARGET = 1e-2
MAX_DECAY = math.log(DECAY_TARGET) / 0.3
MIN_DECAY = math.log(DECAY_TARGET) / 1.5
D_FF = 11008
RMS_EPS = 1e-6
NEG_INF = -1e30
O_Q = 3 * D_HYENA
O_K = O_Q + D_ATTN
O_V = O_K + KV_DIM
D_IN = O_V + KV_DIM

N_CTX = BATCH * SEQ
N_LAT = DEC_BATCH * DEC_SEQ
N_TOK = N_CTX + N_LAT
MOD_ROWS = 8
N_MOD = 6 * D_MODEL

V7X_VMEM_BYTES = 64 * 1024 * 1024
V7X_VMEM_USABLE = 58 * 1024 * 1024
MIB = 1024 * 1024


def _vmem_limit(estimate_bytes):
    return int(min(V7X_VMEM_USABLE, estimate_bytes + estimate_bytes // 4 + 2 * MIB))


def _mod_row(row_start):
    return jnp.where(row_start < N_CTX, 0, 1 + (row_start - N_CTX) // DEC_SEQ)


def _seq_len(row_start):
    return jnp.where(row_start < N_CTX, SEQ, DEC_SEQ)


SUBLANES = 8


def _zero_rows_at(x, sub, breaks):
    pick = lax.broadcasted_iota(jnp.int32, (SUBLANES, 1), 0) == sub
    pieces = []
    at = 0
    for start in sorted(breaks):
        if start > at:
            pieces.append(x[at:start])
        cond = pick if breaks[start] is True else jnp.logical_and(pick, breaks[start])
        pieces.append(jnp.where(cond, 0.0, x[start:start + SUBLANES]))
        at = start + SUBLANES
    if at < x.shape[0]:
        pieces.append(x[at:])
    return jnp.concatenate(pieces, axis=0)


def _conv3_rows(u, w, b, seq_rows=None, split=False):
    rows = u.shape[0]
    seq_rows = rows if seq_rows is None else seq_rows
    starts = range(0, rows, seq_rows)
    up = _zero_rows_at(pltpu.roll(u, 1, 0), 0, {s: (True if s == 0 else split) for s in starts})
    un = _zero_rows_at(pltpu.roll(u, rows - 1, 0), SUBLANES - 1,
                       {s + seq_rows - SUBLANES: (True if s + seq_rows == rows else split) for s in starts})
    return up * w[0:1, :] + u * w[1:2, :] + un * w[2:3, :] + b


MOD_TN = 512


def _mod_kernel(c_ref, w_ref, b_ref, o_ref):
    c = c_ref[...]
    s = (c * jax.nn.sigmoid(c)).astype(BF16)
    o_ref[0] = jnp.dot(s, w_ref[0].astype(BF16), preferred_element_type=F32) + b_ref[0]


def _modulation(cond, w_ada, b_ada):
    est = 2 * D_MODEL * MOD_TN * 4 + D_MODEL * MOD_TN * 2 + 4 * MOD_ROWS * D_MODEL * 4
    return pl.pallas_call(
        _mod_kernel,
        name="adaln_mod",
        grid=(DEPTH, N_MOD // MOD_TN),
        in_specs=[
            pl.BlockSpec((MOD_ROWS, D_MODEL), lambda l, n: (0, 0)),
            pl.BlockSpec((1, D_MODEL, MOD_TN), lambda l, n: (l, 0, n)),
            pl.BlockSpec((1, 1, MOD_TN), lambda l, n: (l, 0, n)),
        ],
        out_specs=pl.BlockSpec((1, MOD_ROWS, MOD_TN), lambda l, n: (l, 0, n)),
        out_shape=jax.ShapeDtypeStruct((DEPTH, MOD_ROWS, N_MOD), F32),
        compiler_params=pltpu.CompilerParams(
            dimension_semantics=("arbitrary", "arbitrary"), vmem_limit_bytes=_vmem_limit(est)),
    )(cond, w_ada, b_ada.reshape(DEPTH, 1, N_MOD))


NORM_TR = 256


def _prenorm_kernel(x_ref, g_ref, shift_ref, scale_ref, o_ref):
    r = _mod_row(pl.program_id(0) * NORM_TR)
    x = x_ref[...]
    y = x * lax.rsqrt(jnp.mean(x * x, axis=-1, keepdims=True) + RMS_EPS) * g_ref[...]
    o_ref[...] = (y * (1.0 + scale_ref[pl.ds(r, 1), :]) + shift_ref[pl.ds(r, 1), :]).astype(o_ref.dtype)


def _prenorm(x, g, mod, shift_chunk, scale_chunk):
    est = 2 * NORM_TR * D_MODEL * (4 + 2) + 3 * NORM_TR * D_MODEL * 4
    return pl.pallas_call(
        _prenorm_kernel,
        name="prenorm",
        grid=(N_TOK // NORM_TR,),
        in_specs=[
            pl.BlockSpec((NORM_TR, D_MODEL), lambda i: (i, 0)),
            pl.BlockSpec((1, D_MODEL), lambda i: (0, 0)),
            pl.BlockSpec((MOD_ROWS, D_MODEL), lambda i: (0, shift_chunk)),
            pl.BlockSpec((MOD_ROWS, D_MODEL), lambda i: (0, scale_chunk)),
        ],
        out_specs=pl.BlockSpec((NORM_TR, D_MODEL), lambda i: (i, 0)),
        out_shape=jax.ShapeDtypeStruct((N_TOK, D_MODEL), BF16),
        compiler_params=pltpu.CompilerParams(
            dimension_semantics=("arbitrary",), vmem_limit_bytes=_vmem_limit(est)),
    )(x, g.reshape(1, D_MODEL), mod, mod)


def _final_norm_kernel(x_ref, g_ref, o_ref):
    x = x_ref[...]
    o_ref[...] = x * lax.rsqrt(jnp.mean(x * x, axis=-1, keepdims=True) + RMS_EPS) * g_ref[...]


def _final_norm(x, g, row_block0, rows):
    est = 4 * NORM_TR * D_MODEL * 4 + 2 * NORM_TR * D_MODEL * 4
    return pl.pallas_call(
        _final_norm_kernel,
        name="final_norm",
        grid=(rows // NORM_TR,),
        in_specs=[
            pl.BlockSpec((NORM_TR, D_MODEL), lambda i: (row_block0 + i, 0)),
            pl.BlockSpec((1, D_MODEL), lambda i: (0, 0)),
        ],
        out_specs=pl.BlockSpec((NORM_TR, D_MODEL), lambda i: (i, 0)),
        out_shape=jax.ShapeDtypeStruct((rows, D_MODEL), F32),
        compiler_params=pltpu.CompilerParams(
            dimension_semantics=("arbitrary",), vmem_limit_bytes=_vmem_limit(est)),
    )(x, g.reshape(1, D_MODEL))


W_CHUNKS = 8


class _Stream:
    def __init__(self, n_tiles, k, n_token_tiles):
        self.n_tiles = n_tiles
        self.n_chunks = W_CHUNKS * ((n_token_tiles - 1) // W_CHUNKS)
        assert self.n_chunks >= W_CHUNKS and k % self.n_chunks == 0
        self.kc = k // self.n_chunks

    def token(self, jj, i):
        return jnp.where(jj == 0, 0, i)

    def tile(self, jj):
        return jnp.maximum(jj - 1, 0)

    def chunk_spec(self, layer, width, col_block):
        last = self.n_chunks - 1

        def index(jj, i):
            arriving = jnp.minimum(jj, self.n_tiles - 1)
            chunk = jnp.where(jj < self.n_tiles, jnp.minimum(i, last), last)
            return layer, chunk, col_block(arriving)
        return pl.BlockSpec((None, self.kc, width), index)

    def cast_chunk(self, chunk_refs, wb_ref):
        jj = pl.program_id(0)
        i = pl.program_id(1)

        @pl.when(jnp.logical_and(jj < self.n_tiles, i < self.n_chunks))
        def _():
            rows = pl.ds(pl.multiple_of(i * self.kc, self.kc), self.kc)
            lane = 0
            for ref in chunk_refs:
                width = ref.shape[-1]
                wb_ref[jj % 2, rows, lane:lane + width] = ref[...].astype(BF16)
                lane += width

    def multiplying(self):
        return pl.program_id(0) >= 1

    def weights(self, wb_ref):
        return wb_ref.at[(pl.program_id(0) + 1) % 2]


def _matmul_kernel(x_ref, w_ref, o_ref, wb_ref, *, stream):
    stream.cast_chunk([w_ref], wb_ref)
    w = stream.weights(wb_ref)

    @pl.when(stream.multiplying())
    def _():
        o_ref[...] = jnp.dot(x_ref[...], w[...], preferred_element_type=F32)


def _matmul(x, w_stack, layer, tm, tn):
    m, k = x.shape
    n = w_stack.shape[2]
    stream = _Stream(n // tn, k, m // tm)
    est = 2 * (tm * k * 2 + tm * tn * 4 + stream.kc * tn * 4) + 2 * k * tn * 2 + tm * tn * 4
    return pl.pallas_call(
        functools.partial(_matmul_kernel, stream=stream),
        name="in_proj",
        grid=(n // tn + 1, m // tm),
        in_specs=[
            pl.BlockSpec((tm, k), lambda jj, i: (stream.token(jj, i), 0)),
            stream.chunk_spec(layer, tn, lambda t: t),
        ],
        out_specs=pl.BlockSpec((tm, tn), lambda jj, i: (stream.token(jj, i), stream.tile(jj))),
        out_shape=jax.ShapeDtypeStruct((m, n), F32),
        scratch_shapes=[pltpu.VMEM((2, k, tn), BF16)],
        compiler_params=pltpu.CompilerParams(
            dimension_semantics=("arbitrary", "arbitrary"), vmem_limit_bytes=_vmem_limit(est)),
    )(x, w_stack)


def _residual_update(lhs, w, res_ref, gate_ref, o_ref, mod_row):
    y = None
    k0 = 0
    for x_ref in lhs:
        kk = x_ref.shape[1]
        part = jnp.dot(x_ref[...], w[k0:k0 + kk, :], preferred_element_type=F32)
        y = part if y is None else y + part
        k0 += kk
    o_ref[...] = res_ref[...] + gate_ref[pl.ds(mod_row, 1), :] * y


def _matmul_residual_kernel(*refs, n_lhs, tm, stream):
    w_ref, res_ref, gate_ref, o_ref, wb_ref = refs[n_lhs:]
    stream.cast_chunk([w_ref], wb_ref)
    mod_row = _mod_row(pl.program_id(1) * tm)
    w = stream.weights(wb_ref)

    @pl.when(stream.multiplying())
    def _():
        _residual_update(refs[:n_lhs], w, res_ref, gate_ref, o_ref, mod_row)


def _matmul_residual(lhs, w_stack, layer, res, mod, gate_chunk, tm, tn, name):
    m = res.shape[0]
    _, k, n = w_stack.shape
    gate_block0 = gate_chunk * (D_MODEL // tn)
    stream = _Stream(n // tn, k, m // tm)
    token, tile = stream.token, stream.tile
    est = (2 * (tm * k * 2 + 2 * tm * tn * 4 + MOD_ROWS * tn * 4 + stream.kc * tn * 4) + 2 * k * tn * 2
           + tm * tn * 4)
    return pl.pallas_call(
        functools.partial(_matmul_residual_kernel, n_lhs=len(lhs), tm=tm, stream=stream),
        name=name,
        grid=(n // tn + 1, m // tm),
        in_specs=[pl.BlockSpec((tm, x.shape[1]), lambda j, i: (token(j, i), 0)) for x in lhs] + [
            stream.chunk_spec(layer, tn, lambda t: t),
            pl.BlockSpec((tm, tn), lambda j, i: (token(j, i), tile(j))),
            pl.BlockSpec((MOD_ROWS, tn), lambda j, i: (0, gate_block0 + tile(j))),
        ],
        out_specs=pl.BlockSpec((tm, tn), lambda j, i: (token(j, i), tile(j))),
        out_shape=jax.ShapeDtypeStruct((m, n), F32),
        scratch_shapes=[pltpu.VMEM((2, k, tn), BF16)],
        compiler_params=pltpu.CompilerParams(
            dimension_semantics=("arbitrary", "arbitrary"), vmem_limit_bytes=_vmem_limit(est)),
    )(*lhs, w_stack, res, mod)


FFN_TM = 1024
FFN_TN = 256


def _ffn_up_kernel(h_ref, wg_ref, wv_ref, cwg_ref, cwv_ref, cbg_ref, cbv_ref, o_ref, wb_ref):
    @pl.when(pl.program_id(1) == 0)
    def _():
        wb_ref[:, :FFN_TN] = wg_ref[...].astype(BF16)
        wb_ref[:, FFN_TN:] = wv_ref[...].astype(BF16)

    u = jnp.dot(h_ref[...], wb_ref[...], preferred_element_type=F32)
    is_ctx = pl.program_id(1) * FFN_TM < N_CTX
    g = _conv3_rows(u[:, :FFN_TN], cwg_ref[...], cbg_ref[...], SEQ, is_ctx)
    v = _conv3_rows(u[:, FFN_TN:], cwv_ref[...], cbv_ref[...], SEQ, is_ctx)
    o_ref[...] = (g * jax.nn.sigmoid(g) * v).astype(o_ref.dtype)


FFN_PAIR = 2 * FFN_TN


def _ffn_up2_kernel(h_ref, wg_ref, wva_ref, wvb_ref, cwg_ref, cwva_ref, cwvb_ref, cbg_ref, cbva_ref, cbvb_ref,
                    o_ref, wb_ref, *, stream):
    stream.cast_chunk([wg_ref, wva_ref, wvb_ref], wb_ref)
    is_ctx = pl.program_id(1) * FFN_TM < N_CTX
    w = stream.weights(wb_ref)

    @pl.when(stream.multiplying())
    def _():
        u = jnp.dot(h_ref[...], w[...], preferred_element_type=F32)
        cwg = cwg_ref[...]
        cbg = cbg_ref[...]
        for blk, (cwv_ref, cbv_ref) in enumerate(((cwva_ref, cbva_ref), (cwvb_ref, cbvb_ref))):
            lanes = slice(blk * FFN_TN, (blk + 1) * FFN_TN)
            g = _conv3_rows(u[:, lanes], cwg[:, lanes], cbg[:, lanes], SEQ, is_ctx)
            v = _conv3_rows(u[:, FFN_PAIR + blk * FFN_TN:FFN_PAIR + (blk + 1) * FFN_TN], cwv_ref[...],
                            cbv_ref[...], SEQ, is_ctx)
            o_ref[:, lanes] = (g * jax.nn.sigmoid(g) * v).astype(o_ref.dtype)


def _ffn_up2(h, w_up, layer, conv_w_t, conv_b):
    nb = D_FF // FFN_TN
    npair = nb // 2
    n_m = N_TOK // FFN_TM
    stream = _Stream(npair, D_MODEL, n_m)
    est = (2 * (FFN_TM * D_MODEL * 2 + FFN_TM * FFN_PAIR * 2 + stream.kc * 2 * FFN_PAIR * 4)
           + 2 * D_MODEL * 2 * FFN_PAIR * 2 + 12 * FFN_TM * FFN_PAIR * 4)
    pair = lambda rows: pl.BlockSpec((rows, FFN_PAIR), lambda jj, i: (0, stream.tile(jj)))
    val = lambda rows, blk: pl.BlockSpec((rows, FFN_TN), lambda jj, i: (0, nb + 2 * stream.tile(jj) + blk))
    return pl.pallas_call(
        functools.partial(_ffn_up2_kernel, stream=stream),
        name="ffn_up2",
        grid=(npair + 1, n_m),
        in_specs=[
            pl.BlockSpec((FFN_TM, D_MODEL), lambda jj, i: (stream.token(jj, i), 0)),
            stream.chunk_spec(layer, FFN_PAIR, lambda t: t),
            stream.chunk_spec(layer, FFN_TN, lambda t: nb + 2 * t),
            stream.chunk_spec(layer, FFN_TN, lambda t: nb + 2 * t + 1),
            pair(3), val(3, 0), val(3, 1), pair(1), val(1, 0), val(1, 1),
        ],
        out_specs=pl.BlockSpec((FFN_TM, FFN_PAIR), lambda jj, i: (stream.token(jj, i), stream.tile(jj))),
        out_shape=jax.ShapeDtypeStruct((N_TOK, npair * FFN_PAIR), BF16),
        scratch_shapes=[pltpu.VMEM((2, D_MODEL, 2 * FFN_PAIR), BF16)],
        compiler_params=pltpu.CompilerParams(
            dimension_semantics=("arbitrary", "arbitrary"), vmem_limit_bytes=_vmem_limit(est)),
    )(h, w_up, w_up, w_up, conv_w_t, conv_w_t, conv_w_t, conv_b, conv_b, conv_b)


def _ffn_up(h, w_up, layer, conv_w_t, conv_b, block0, n_blocks):
    nb = D_FF // FFN_TN
    est = (2 * (FFN_TM * D_MODEL * 2 + 2 * D_MODEL * FFN_TN * 4 + FFN_TM * FFN_TN * 2)
           + 2 * D_MODEL * FFN_TN * 2 + 10 * FFN_TM * FFN_TN * 4)
    return pl.pallas_call(
        _ffn_up_kernel,
        name="ffn_up",
        grid=(n_blocks, N_TOK // FFN_TM),
        in_specs=[
            pl.BlockSpec((FFN_TM, D_MODEL), lambda j, i: (i, 0)),
            pl.BlockSpec((None, D_MODEL, FFN_TN), lambda j, i: (layer, 0, block0 + j)),
            pl.BlockSpec((None, D_MODEL, FFN_TN), lambda j, i: (layer, 0, nb + block0 + j)),
            pl.BlockSpec((3, FFN_TN), lambda j, i: (0, block0 + j)),
            pl.BlockSpec((3, FFN_TN), lambda j, i: (0, nb + block0 + j)),
            pl.BlockSpec((1, FFN_TN), lambda j, i: (0, block0 + j)),
            pl.BlockSpec((1, FFN_TN), lambda j, i: (0, nb + block0 + j)),
        ],
        out_specs=pl.BlockSpec((FFN_TM, FFN_TN), lambda j, i: (i, j)),
        out_shape=jax.ShapeDtypeStruct((N_TOK, n_blocks * FFN_TN), BF16),
        scratch_shapes=[pltpu.VMEM((D_MODEL, 2 * FFN_TN), BF16)],
        compiler_params=pltpu.CompilerParams(
            dimension_semantics=("arbitrary", "arbitrary"), vmem_limit_bytes=_vmem_limit(est)),
    )(h, w_up, w_up, conv_w_t, conv_w_t, conv_b, conv_b)


FILT_PAD = 64


def _split_bf16(a):
    hi = a.astype(BF16)
    return hi, (a - hi.astype(F32)).astype(BF16)


def _dot_split(t_hi, t_lo, b):
    b_hi, b_lo = _split_bf16(b)
    return (jnp.dot(t_hi, b_hi, preferred_element_type=F32)
            + jnp.dot(t_hi, b_lo, preferred_element_type=F32)
            + jnp.dot(t_lo, b_hi, preferred_element_type=F32))


def _dft_tables(L):
    f = jnp.arange(L, dtype=jnp.int32)[:, None]
    s = jnp.arange(L, dtype=jnp.int32)[None, :]
    ang = ((f * s) % (2 * L)).astype(F32) * (math.pi / L)
    cm = jnp.cos(ang)
    nyq = jnp.where(s % 2 == 0, 1.0, -1.0).astype(F32)
    sm = jnp.where(f == 0, nyq, -jnp.sin(ang))
    return tuple(jnp.stack(_split_bf16(t)) for t in (cm, sm, sm.T))


def _filter_features(L):
    t = jnp.linspace(0.0, 1.0, L, dtype=F32)[:, None]
    w = 2.0 * math.pi * jnp.arange(L, dtype=F32)[:, None] / L
    f = jnp.linspace(1e-4, FILTER_BANDS - 1, FILTER_BANDS, dtype=F32)[None, :]
    z = jnp.concatenate([t, jnp.cos(f * w), -jnp.sin(f * w)], axis=-1)
    return jnp.pad(z, ((0, 0), (0, FILT_PAD - FILTER_EMB)))


def _filter_kernel(z_ref, w1_ref, b1_ref, w2_ref, b2_ref, w3f_ref, w3b_ref, freq_ref, delta_ref,
                   cm_ref, sm_ref, p_ref, q_ref, r_ref, hid_ref, *, L):
    z = z_ref[...]

    @pl.when(pl.program_id(0) == 0)
    def _():
        fr = freq_ref[...]
        h = jnp.sin(fr * (jnp.dot(z, w1_ref[...], precision=HIGHEST, preferred_element_type=F32) + b1_ref[...]))
        hid_ref[...] = jnp.sin(
            fr * (jnp.dot(h, w2_ref[...], precision=HIGHEST, preferred_element_type=F32) + b2_ref[...]))

    h = hid_ref[...]
    decay = jnp.exp(-z[:, 0:1] * jnp.abs(delta_ref[...]))
    h_fwd = jnp.dot(h, w3f_ref[...], precision=HIGHEST, preferred_element_type=F32) * decay
    h_bwd = jnp.dot(h, w3b_ref[...], precision=HIGHEST, preferred_element_type=F32) * decay
    pos = lax.broadcasted_iota(jnp.int32, (L, 1), 0)
    first = pos == 0
    h_bwd = jnp.where(first, 0.0, h_bwd)
    even = h_fwd + h_bwd
    odd = h_fwd - h_bwd
    sign = jnp.where((pos & 1) == 0, 1.0, -1.0)
    nyq = jnp.sum(sign * even, axis=0, keepdims=True)
    wgt = jnp.where(first, 1.0 / (2 * L), 1.0 / L)
    p = _dot_split(cm_ref[0], cm_ref[1], even) * wgt
    p_ref[...] = p
    q_ref[...] = jnp.where(first, 0.0, _dot_split(sm_ref[0], sm_ref[1], odd) * wgt)
    r_ref[...] = jnp.where(first, nyq * wgt, p)


FILT_TC = 256


def _filter_spectrum(L, z, w1p, b1, w2, b2, w3, freq, deltas, cm, sm):
    nb = D_HYENA // FILT_TC
    est = 2 * (2 * 2 * L * L * 2) + 2 * 4 * L * FILT_TC * 4 + 10 * L * FILT_TC * 4 + 4 * L * FILT_PAD * 4
    full = lambda shape: pl.BlockSpec(shape, lambda c: (0,) * len(shape))
    return pl.pallas_call(
        functools.partial(_filter_kernel, L=L),
        name=f"hyena_filter_{L}",
        grid=(nb,),
        in_specs=[
            full((L, FILT_PAD)), full((FILT_PAD, FILTER_ORDER)), full((1, FILTER_ORDER)),
            full((FILTER_ORDER, FILTER_ORDER)), full((1, FILTER_ORDER)),
            pl.BlockSpec((FILTER_ORDER, FILT_TC), lambda c: (0, c)),
            pl.BlockSpec((FILTER_ORDER, FILT_TC), lambda c: (0, nb + c)),
            full((1, FILTER_ORDER)),
            pl.BlockSpec((1, FILT_TC), lambda c: (0, c)),
            full((2, L, L)), full((2, L, L)),
        ],
        out_specs=[pl.BlockSpec((L, FILT_TC), lambda c: (0, c))] * 3,
        out_shape=[jax.ShapeDtypeStruct((L, D_HYENA), F32)] * 3,
        scratch_shapes=[pltpu.VMEM((L, FILTER_ORDER), F32)],
        compiler_params=pltpu.CompilerParams(
            dimension_semantics=("arbitrary",), vmem_limit_bytes=_vmem_limit(est)),
    )(z, w1p, b1, w2, b2, w3, w3, freq, deltas, cm, sm)


def _hyena_kernel(x0_ref, x1_ref, v_ref, cw0_ref, cw1_ref, cw2_ref, cb0_ref, cb1_ref, cb2_ref, hb_ref,
                  p_ref, q_ref, r_ref, cm_ref, sm_ref, smt_ref, gn_ref, o_ref, acc_ref, *, L, tc, nct,
                  grid_ids=None):
    ct = pl.program_id(1) if grid_ids is None else grid_ids[1]
    x0 = _conv3_rows(x0_ref[...], cw0_ref[...], cb0_ref[...])
    x1 = _conv3_rows(x1_ref[...], cw1_ref[...], cb1_ref[...])
    v = _conv3_rows(v_ref[...], cw2_ref[...], cb2_ref[...])
    u = x1 * v
    ub = u.astype(BF16)
    top = jnp.dot(cm_ref[...], ub, preferred_element_type=F32)
    bot = jnp.dot(sm_ref[...], ub, preferred_element_type=F32)
    q = q_ref[...]
    y_top = top * p_ref[...] - bot * q
    y_bot = top * q + bot * r_ref[...]
    y = (jnp.dot(cm_ref[...], y_top.astype(BF16), preferred_element_type=F32)
         + jnp.dot(smt_ref[...], y_bot.astype(BF16), preferred_element_type=F32))
    acc_ref[ct] = x0 * (y + u * hb_ref[...])

    @pl.when(ct == nct - 1)
    def _():
        ss = jnp.zeros((L, 1), F32)
        for j in range(nct):
            a = acc_ref[j]
            ss = ss + jnp.sum(a * a, axis=-1, keepdims=True)
        inv = lax.rsqrt(ss * (1.0 / D_HYENA) + RMS_EPS)
        for j in range(nct):
            o_ref[:, j * tc:(j + 1) * tc] = (acc_ref[j] * inv * gn_ref[:, j * tc:(j + 1) * tc]).astype(o_ref.dtype)


def _with_fill(kernel_fn, n_real, out_pos, n_axes):
    def body(*refs):
        grid_ids = tuple(pl.program_id(a) for a in range(n_axes))

        @pl.when(grid_ids[0] < n_real)
        def _():
            kernel_fn(*refs, grid_ids=grid_ids)

        @pl.when(grid_ids[0] >= n_real)
        def _():
            refs[out_pos][...] = jnp.zeros_like(refs[out_pos])
    return body


def _into_rows(kernel_fn, dest):
    if dest is None:
        return kernel_fn, [], [], {}

    def body(dest_ref, *refs):
        del dest_ref
        kernel_fn(*refs)
    return body, [pl.BlockSpec(memory_space=pl.ANY)], [dest], {0: 0}


def _hyena(proj, conv_w_t, conv_b, hy_bias, spectrum, cm, sm, smt, gn, *, L, tc, nseq, row_block0, dest=None):
    nct = D_HYENA // tc
    cpb = D_HYENA // tc
    est = (3 * L * L * 2 + 2 * (3 * L * tc * 4 + 3 * L * tc * 4) + nct * L * tc * 4
           + 2 * L * D_HYENA * 2 + 14 * L * tc * 4)
    once = pl.Buffered(1)
    n_fill = N_TOK // L - row_block0 - nseq if dest is None else 0
    grp = lambda g: pl.BlockSpec((L, tc), lambda s, c: (row_block0 + jnp.minimum(s, nseq - 1), g * cpb + c))
    tap = lambda g: pl.BlockSpec((3, tc), lambda s, c: (0, g * cpb + c))
    bias = lambda g: pl.BlockSpec((1, tc), lambda s, c: (0, g * cpb + c))
    col = lambda rows: pl.BlockSpec((rows, tc), lambda s, c: (0, c))
    table = pl.BlockSpec((None, L, L), lambda s, c: (0, 0, 0), pipeline_mode=once)
    kernel_fn = functools.partial(_hyena_kernel, L=L, tc=tc, nct=nct)
    if n_fill:
        kernel_fn = _with_fill(kernel_fn, nseq, -2, 2)
    body, dest_specs, dest_args, aliases = _into_rows(kernel_fn, dest)
    return pl.pallas_call(
        body,
        name=f"hyena_conv_{L}",
        grid=(nseq + n_fill, nct),
        in_specs=dest_specs + [grp(0), grp(1), grp(2), tap(0), tap(1), tap(2), bias(0), bias(1), bias(2),
                               col(1), col(L), col(L), col(L), table, table, table,
                               pl.BlockSpec((1, D_HYENA), lambda s, c: (0, 0))],
        out_specs=pl.BlockSpec((L, D_HYENA), lambda s, c: (row_block0 + s, 0)),
        out_shape=jax.ShapeDtypeStruct((N_TOK, D_HYENA), BF16),
        input_output_aliases=aliases,
        scratch_shapes=[pltpu.VMEM((nct, L, tc), F32)],
        compiler_params=pltpu.CompilerParams(
            dimension_semantics=("arbitrary", "arbitrary"), vmem_limit_bytes=_vmem_limit(est)),
    )(*dest_args, proj, proj, proj, conv_w_t, conv_w_t, conv_w_t, conv_b, conv_b, conv_b,
      hy_bias, *spectrum, cm, sm, smt, gn)


def _dot_nt(a, b):
    return lax.dot_general(a, b, (((1,), (1,)), ((), ())), preferred_element_type=F32)


def _head_lanes(kvh, g):
    hd = kvh * KV_GROUP + g
    return slice(hd * HEAD_DIM, (hd + 1) * HEAD_DIM)


def _group_sinks(sink_ref, kvh, rows):
    return jnp.concatenate([jnp.full((rows, 1), sink_ref[kvh * KV_GROUP + g], F32) for g in range(KV_GROUP)],
                           axis=0)


def _head_norm_store(acc_ref, gn_ref, o_ref):
    y = acc_ref[...]
    inv = lax.rsqrt(jnp.mean(y * y, axis=-1, keepdims=True) + RMS_EPS)
    o_ref[...] = (y * inv * gn_ref[...]).astype(o_ref.dtype)


def _attn_ctx_kernel(sink_ref, q_ref, k_ref, v_ref, gn_ref, o_ref, ko_ref, vo_ref, acc_ref, grid_ids=None):
    del grid_ids
    ko_ref[...] = k_ref[...]
    vo_ref[...] = v_ref[...]
    for kvh in range(N_KV_HEADS):
        ksl = slice(kvh * HEAD_DIM, (kvh + 1) * HEAD_DIM)
        kh = k_ref[:, ksl].astype(BF16)
        vh = v_ref[:, ksl].astype(BF16)
        for g in range(KV_GROUP):
            hsl = _head_lanes(kvh, g)
            s = _dot_nt(q_ref[:, hsl].astype(BF16), kh) * ATTN_SCALE
            sk = sink_ref[kvh * KV_GROUP + g]
            m = jnp.maximum(jnp.max(s, axis=-1, keepdims=True), sk)
            e = jnp.exp(s - m)
            den = jnp.sum(e, axis=-1, keepdims=True) + jnp.exp(sk - m)
            acc_ref[:, hsl] = jnp.dot((e / den).astype(BF16), vh, preferred_element_type=F32)
    _head_norm_store(acc_ref, gn_ref, o_ref)


def _attn_ctx(proj, sink, gn):
    est = 2 * (SEQ * D_ATTN * 4 + 4 * SEQ * KV_DIM * 4 + SEQ * D_ATTN * 2) + 3 * SEQ * D_ATTN * 4
    n_fill = N_LAT // SEQ
    real = lambda b: jnp.minimum(b, BATCH - 1)
    return pl.pallas_call(
        _with_fill(_attn_ctx_kernel, BATCH, -4, 1),
        name="attn_ctx",
        grid=(BATCH + n_fill,),
        in_specs=[
            pl.BlockSpec(memory_space=pltpu.SMEM),
            pl.BlockSpec((SEQ, D_ATTN), lambda b: (real(b), O_Q // D_ATTN)),
            pl.BlockSpec((SEQ, KV_DIM), lambda b: (real(b), O_K // KV_DIM)),
            pl.BlockSpec((SEQ, KV_DIM), lambda b: (real(b), O_V // KV_DIM)),
            pl.BlockSpec((1, D_ATTN), lambda b: (0, 0)),
        ],
        out_specs=[pl.BlockSpec((SEQ, D_ATTN), lambda b: (b, 0)),
                   pl.BlockSpec((SEQ, KV_DIM), lambda b: (real(b), 0)),
                   pl.BlockSpec((SEQ, KV_DIM), lambda b: (real(b), 0))],
        out_shape=[jax.ShapeDtypeStruct((N_TOK, D_ATTN), BF16),
                   jax.ShapeDtypeStruct((N_CTX, KV_DIM), F32),
                   jax.ShapeDtypeStruct((N_CTX, KV_DIM), F32)],
        scratch_shapes=[pltpu.VMEM((SEQ, D_ATTN), F32)],
        compiler_params=pltpu.CompilerParams(
            dimension_semantics=("arbitrary",), vmem_limit_bytes=_vmem_limit(est)),
    )(sink, proj, proj, proj, gn)


def _rope_tables():
    half = HEAD_DIM // 2
    quarter = half // 2
    pos = jnp.arange(DEC_SEQ, dtype=jnp.int32)
    row_pos = (pos // GRID_W).astype(F32)
    col_pos = (pos % GRID_W).astype(F32)
    inv = ROPE_BASE ** (-jnp.arange(quarter, dtype=F32) / quarter)
    lane = jnp.arange(HEAD_DIM, dtype=jnp.int32)
    p = jnp.where((lane < half)[None, :], row_pos[:, None], col_pos[:, None])
    ang = p * inv[lane % quarter][None, :]
    low = ((lane % half) < quarter)[None, :]
    cos = jnp.cos(ang)
    sin = jnp.sin(ang)
    return cos, jnp.where(low, -sin, 0.0), jnp.where(low, 0.0, sin)


ROPE_SHIFT = HEAD_DIM // 4
LAT_SPAN = BLOCK + 2 * WINDOW


def _rope(x, cos, sin_lo, sin_hi):
    return (x * cos + pltpu.roll(x, HEAD_DIM - ROPE_SHIFT, 1) * sin_lo
            + pltpu.roll(x, ROPE_SHIFT, 1) * sin_hi)


def _attn_lat_kernel(sink_ref, q_ref, k_ref, v_ref, ck_ref, cv_ref, cos_ref, slo_ref, shi_ref, gn_ref,
                     o_ref, acc_ref):
    i = pl.program_id(1)
    q0 = pl.multiple_of(i * BLOCK, BLOCK)
    k0 = pl.multiple_of(jnp.clip((i - 1) * BLOCK, 0, DEC_SEQ - LAT_SPAN), BLOCK)
    q_tab = (cos_ref[pl.ds(q0, BLOCK), :], slo_ref[pl.ds(q0, BLOCK), :], shi_ref[pl.ds(q0, BLOCK), :])
    k_tab = (cos_ref[pl.ds(k0, LAT_SPAN), :], slo_ref[pl.ds(k0, LAT_SPAN), :], shi_ref[pl.ds(k0, LAT_SPAN), :])
    q_pos = q0 + lax.broadcasted_iota(jnp.int32, (BLOCK, 1), 0)
    k_pos = k0 + lax.broadcasted_iota(jnp.int32, (1, LAT_SPAN), 1)
    valid = jnp.abs(q_pos - k_pos) <= WINDOW
    valid = jnp.concatenate([valid] * KV_GROUP, axis=0)
    for kvh in range(N_KV_HEADS):
        ksl = slice(kvh * HEAD_DIM, (kvh + 1) * HEAD_DIM)
        kw = _rope(k_ref[pl.ds(k0, LAT_SPAN), ksl], *k_tab).astype(BF16)
        vw = v_ref[pl.ds(k0, LAT_SPAN), ksl].astype(BF16)
        kc = ck_ref[0, :, ksl].astype(BF16)
        vc = cv_ref[0, :, ksl].astype(BF16)
        q = jnp.concatenate([_rope(q_ref[:, _head_lanes(kvh, g)], *q_tab) for g in range(KV_GROUP)],
                            axis=0).astype(BF16)
        sk = _group_sinks(sink_ref, kvh, BLOCK)
        s_lat = jnp.where(valid, _dot_nt(q, kw) * ATTN_SCALE, NEG_INF)
        s_ctx = _dot_nt(q, kc) * ATTN_SCALE
        m = jnp.maximum(jnp.maximum(jnp.max(s_lat, axis=-1, keepdims=True),
                                    jnp.max(s_ctx, axis=-1, keepdims=True)), sk)
        e_lat = jnp.exp(s_lat - m)
        e_ctx = jnp.exp(s_ctx - m)
        den = (jnp.sum(e_lat, axis=-1, keepdims=True) + jnp.sum(e_ctx, axis=-1, keepdims=True)
               + jnp.exp(sk - m))
        o = (jnp.dot((e_lat / den).astype(BF16), vw, preferred_element_type=F32)
             + jnp.dot((e_ctx / den).astype(BF16), vc, preferred_element_type=F32))
        for g in range(KV_GROUP):
            acc_ref[:, _head_lanes(kvh, g)] = o[g * BLOCK:(g + 1) * BLOCK]
    _head_norm_store(acc_ref, gn_ref, o_ref)


def _attn_lat(proj, cache_k, cache_v, sink, rope, gn, dest):
    nqb = DEC_SEQ // BLOCK
    est = (2 * (BLOCK * D_ATTN * 4 + 2 * DEC_SEQ * KV_DIM * 4 + 2 * PAST_LEN * KV_DIM * 4
                + 3 * DEC_SEQ * HEAD_DIM * 4 + BLOCK * D_ATTN * 2) + 8 * BLOCK * D_ATTN * 4)
    lat_row_block0 = N_CTX // BLOCK
    lat_seq_block0 = N_CTX // DEC_SEQ
    tab = pl.BlockSpec((DEC_SEQ, HEAD_DIM), lambda b, i: (0, 0))
    body, dest_specs, dest_args, aliases = _into_rows(_attn_lat_kernel, dest)
    return pl.pallas_call(
        body,
        name="attn_lat",
        grid=(DEC_BATCH, nqb),
        input_output_aliases=aliases,
        in_specs=dest_specs + [
            pl.BlockSpec(memory_space=pltpu.SMEM),
            pl.BlockSpec((BLOCK, D_ATTN), lambda b, i: (lat_row_block0 + b * nqb + i, O_Q // D_ATTN)),
            pl.BlockSpec((DEC_SEQ, KV_DIM), lambda b, i: (lat_seq_block0 + b, O_K // KV_DIM)),
            pl.BlockSpec((DEC_SEQ, KV_DIM), lambda b, i: (lat_seq_block0 + b, O_V // KV_DIM)),
            pl.BlockSpec((1, PAST_LEN, KV_DIM), lambda b, i: (b, 0, 0)),
            pl.BlockSpec((1, PAST_LEN, KV_DIM), lambda b, i: (b, 0, 0)),
            tab, tab, tab,
            pl.BlockSpec((1, D_ATTN), lambda b, i: (0, 0)),
        ],
        out_specs=pl.BlockSpec((BLOCK, D_ATTN), lambda b, i: (lat_row_block0 + b * nqb + i, 0)),
        out_shape=jax.ShapeDtypeStruct((N_TOK, D_ATTN), BF16),
        scratch_shapes=[pltpu.VMEM((BLOCK, D_ATTN), F32)],
        compiler_params=pltpu.CompilerParams(
            dimension_semantics=("arbitrary", "arbitrary"), vmem_limit_bytes=_vmem_limit(est)),
    )(*dest_args, sink, proj, proj, proj, cache_k, cache_v, *rope, gn)


def kernel(x_prompt, x_sample, cache_k, cache_v, c, c_ctx, w_ada, b_ada, norm_mix, w_in, hy_conv_w, hy_conv_b, filt_w1, filt_b1, filt_w2, filt_b2, filt_w3, filt_freq, hy_bias, attn_sink, gn_hyena, gn_attn, w_out, norm_ffn, w_up, ffn_conv_w, ffn_conv_b, w_down, final_norm):
    x = jnp.concatenate([x_prompt.reshape(N_CTX, D_MODEL), x_sample.reshape(N_LAT, D_MODEL)], axis=0)
    cond = jnp.concatenate([c_ctx[None, :], c, jnp.zeros((MOD_ROWS - 1 - DEC_BATCH, D_MODEL), F32)], axis=0)
    mod_all = _modulation(cond, w_ada, b_ada)

    rope = _rope_tables()
    tables = {L: _dft_tables(L) for L in (SEQ, DEC_SEQ)}
    feats = {L: _filter_features(L) for L in (SEQ, DEC_SEQ)}
    deltas = jnp.linspace(MIN_DECAY, MAX_DECAY, D_HYENA, dtype=F32)[None, :]

    new_k = []
    new_v = []
    for i in range(DEPTH):
        mod = mod_all[i]
        h = _prenorm(x, norm_mix[i], mod, 0, 1)
        proj = _matmul(h, w_in, i, 1024, 1024)

        hy_w_t = hy_conv_w[i].T
        hy_b = hy_conv_b[i][None, :]
        w1p = jnp.pad(filt_w1[i], ((0, FILT_PAD - FILTER_EMB), (0, 0)))
        y_a = None
        for L, tc, nseq, rb0 in ((SEQ, 1024, BATCH, 0), (DEC_SEQ, 256, DEC_BATCH, N_CTX // DEC_SEQ)):
            cm, sm, smt = tables[L]
            spectrum = _filter_spectrum(L, feats[L], w1p, filt_b1[i][None, :], filt_w2[i], filt_b2[i][None, :],
                                        filt_w3[i], filt_freq[i][None, :], deltas, cm, sm)
            y_a = _hyena(proj, hy_w_t, hy_b, hy_bias[i][None, :], spectrum, cm, sm, smt,
                         gn_hyena[i][None, :], L=L, tc=tc, nseq=nseq, row_block0=rb0, dest=y_a)
        gn_a = gn_attn[i][None, :]
        y_b, k_ctx, v_ctx = _attn_ctx(proj, attn_sink[i], gn_a)
        y_b = _attn_lat(proj, cache_k[:, i].reshape(DEC_BATCH, PAST_LEN, KV_DIM),
                        cache_v[:, i].reshape(DEC_BATCH, PAST_LEN, KV_DIM), attn_sink[i], rope, gn_a, y_b)
        x = _matmul_residual([y_a, y_b], w_out, i, x, mod, 2, 512, 1024, "out_proj")

        h = _prenorm(x, norm_ffn[i], mod, 3, 4)
        ffn_w_t = ffn_conv_w[i].T
        ffn_b = ffn_conv_b[i][None, :]
        n_ffn_blocks = D_FF // FFN_TN
        acts = [_ffn_up2(h, w_up, i, ffn_w_t, ffn_b)]
        if n_ffn_blocks % 2:
            acts.append(_ffn_up(h, w_up, i, ffn_w_t, ffn_b, n_ffn_blocks - 1, 1))
        x = _matmul_residual(acts, w_down, i, x, mod, 5, 512, 512, "ffn_down")

        new_k.append(k_ctx.reshape(BATCH, SEQ, N_KV_HEADS, HEAD_DIM))
        new_v.append(v_ctx.reshape(BATCH, SEQ, N_KV_HEADS, HEAD_DIM))

    y_prompt = _final_norm(x, final_norm, 0, N_CTX).reshape(BATCH, SEQ, D_MODEL)
    y_sample = _final_norm(x, final_norm, N_CTX // NORM_TR, N_LAT).reshape(DEC_BATCH, DEC_SEQ, D_MODEL)
    return (y_prompt, y_sample, jnp.stack(new_k, axis=1), jnp.stack(new_v, axis=1))
```

```python
import functools
import math

import jax
import jax.numpy as jnp
from jax import lax
from jax.experimental import pallas as pl
from jax.experimental.pallas import tpu as pltpu

F32 = jnp.float32
BF16 = jnp.bfloat16
HIGHEST = lax.Precision.HIGHEST

D_MODEL = 4096
BATCH = 32
SEQ = 256
DEPTH = 4
DEC_BATCH = 2
DEC_SEQ = 1024
PAST_LEN = 512
GRID_W = 64
D_HYENA = D_MODEL // 2
D_ATTN = D_MODEL - D_HYENA
HEAD_DIM = 128
N_HEADS = D_ATTN // HEAD_DIM
N_KV_HEADS = 4
KV_GROUP = N_HEADS // N_KV_HEADS
KV_DIM = N_KV_HEADS * HEAD_DIM
WINDOW = 128
BLOCK = 128
ATTN_SCALE = 1.0 / math.sqrt(HEAD_DIM)
ROPE_BASE = 10000.0
FILTER_EMB = 33
FILTER_BANDS = (FILTER_EMB - 1) // 2
FILTER_ORDER = 64
DECAY_TARGET = 1e-2
MAX_DECAY = math.log(DECAY_TARGET) / 0.3
MIN_DECAY = math.log(DECAY_TARGET) / 1.5
D_FF = 11008
RMS_EPS = 1e-6
NEG_INF = -1e30
O_Q = 3 * D_HYENA
O_K = O_Q + D_ATTN
O_V = O_K + KV_DIM
D_IN = O_V + KV_DIM

N_CTX = BATCH * SEQ
N_LAT = DEC_BATCH * DEC_SEQ
N_TOK = N_CTX + N_LAT
MOD_ROWS = 8
N_MOD = 6 * D_MODEL

V7X_VMEM_BYTES = 64 * 1024 * 1024
V7X_VMEM_USABLE = 58 * 1024 * 1024
MIB = 1024 * 1024


def _vmem_limit(estimate_bytes):
    return int(min(V7X_VMEM_USABLE, estimate_bytes + estimate_bytes // 4 + 2 * MIB))


def _mod_row(row_start):
    return jnp.where(row_start < N_CTX, 0, 1 + (row_start - N_CTX) // DEC_SEQ)


def _seq_len(row_start):
    return jnp.where(row_start < N_CTX, SEQ, DEC_SEQ)


SUBLANES = 8


def _zero_rows_at(x, sub, breaks):
    pick = lax.broadcasted_iota(jnp.int32, (SUBLANES, 1), 0) == sub
    pieces = []
    at = 0
    for start in sorted(breaks):
        if start > at:
            pieces.append(x[at:start])
        cond = pick if breaks[start] is True else jnp.logical_and(pick, breaks[start])
        pieces.append(jnp.where(cond, 0.0, x[start:start + SUBLANES]))
        at = start + SUBLANES
    if at < x.shape[0]:
        pieces.append(x[at:])
    return jnp.concatenate(pieces, axis=0)


def _conv3_rows(u, w, b, seq_rows=None, split=False):
    rows = u.shape[0]
    seq_rows = rows if seq_rows is None else seq_rows
    starts = range(0, rows, seq_rows)
    up = _zero_rows_at(pltpu.roll(u, 1, 0), 0, {s: (True if s == 0 else split) for s in starts})
    un = _zero_rows_at(pltpu.roll(u, rows - 1, 0), SUBLANES - 1,
                       {s + seq_rows - SUBLANES: (True if s + seq_rows == rows else split) for s in starts})
    return up * w[0:1, :] + u * w[1:2, :] + un * w[2:3, :] + b


MOD_TN = 512


def _mod_kernel(c_ref, w_ref, b_ref, o_ref):
    c = c_ref[...]
    s = (c * jax.nn.sigmoid(c)).astype(BF16)
    o_ref[0] = jnp.dot(s, w_ref[0].astype(BF16), preferred_element_type=F32) + b_ref[0]


def _modulation(cond, w_ada, b_ada):
    est = 2 * D_MODEL * MOD_TN * 4 + D_MODEL * MOD_TN * 2 + 4 * MOD_ROWS * D_MODEL * 4
    return pl.pallas_call(
        _mod_kernel,
        name="adaln_mod",
        grid=(DEPTH, N_MOD // MOD_TN),
        in_specs=[
            pl.BlockSpec((MOD_ROWS, D_MODEL), lambda l, n: (0, 0)),
            pl.BlockSpec((1, D_MODEL, MOD_TN), lambda l, n: (l, 0, n)),
            pl.BlockSpec((1, 1, MOD_TN), lambda l, n: (l, 0, n)),
        ],
        out_specs=pl.BlockSpec((1, MOD_ROWS, MOD_TN), lambda l, n: (l, 0, n)),
        out_shape=jax.ShapeDtypeStruct((DEPTH, MOD_ROWS, N_MOD), F32),
        compiler_params=pltpu.CompilerParams(
            dimension_semantics=("arbitrary", "arbitrary"), vmem_limit_bytes=_vmem_limit(est)),
    )(cond, w_ada, b_ada.reshape(DEPTH, 1, N_MOD))


NORM_TR = 512


def _prenorm_kernel(x_ref, g_ref, shift_ref, scale_ref, o_ref):
    r = _mod_row(pl.program_id(0) * NORM_TR)
    x = x_ref[...]
    y = x * lax.rsqrt(jnp.mean(x * x, axis=-1, keepdims=True) + RMS_EPS) * g_ref[...]
    o_ref[...] = (y * (1.0 + scale_ref[pl.ds(r, 1), :]) + shift_ref[pl.ds(r, 1), :]).astype(o_ref.dtype)


def _prenorm(x, g, mod, shift_chunk, scale_chunk):
    est = 2 * NORM_TR * D_MODEL * (4 + 2) + 3 * NORM_TR * D_MODEL * 4
    return pl.pallas_call(
        _prenorm_kernel,
        name="prenorm",
        grid=(N_TOK // NORM_TR,),
        in_specs=[
            pl.BlockSpec((NORM_TR, D_MODEL), lambda i: (i, 0)),
            pl.BlockSpec((1, D_MODEL), lambda i: (0, 0)),
            pl.BlockSpec((MOD_ROWS, D_MODEL), lambda i: (0, shift_chunk)),
            pl.BlockSpec((MOD_ROWS, D_MODEL), lambda i: (0, scale_chunk)),
        ],
        out_specs=pl.BlockSpec((NORM_TR, D_MODEL), lambda i: (i, 0)),
        out_shape=jax.ShapeDtypeStruct((N_TOK, D_MODEL), BF16),
        compiler_params=pltpu.CompilerParams(
            dimension_semantics=("arbitrary",), vmem_limit_bytes=_vmem_limit(est)),
    )(x, g.reshape(1, D_MODEL), mod, mod)


def _final_norm_kernel(x_ref, g_ref, o_ref):
    x = x_ref[...]
    o_ref[...] = x * lax.rsqrt(jnp.mean(x * x, axis=-1, keepdims=True) + RMS_EPS) * g_ref[...]


def _final_norm(x, g, row_block0, rows):
    est = 4 * NORM_TR * D_MODEL * 4 + 2 * NORM_TR * D_MODEL * 4
    return pl.pallas_call(
        _final_norm_kernel,
        name="final_norm",
        grid=(rows // NORM_TR,),
        in_specs=[
            pl.BlockSpec((NORM_TR, D_MODEL), lambda i: (row_block0 + i, 0)),
            pl.BlockSpec((1, D_MODEL), lambda i: (0, 0)),
        ],
        out_specs=pl.BlockSpec((NORM_TR, D_MODEL), lambda i: (i, 0)),
        out_shape=jax.ShapeDtypeStruct((rows, D_MODEL), F32),
        compiler_params=pltpu.CompilerParams(
            dimension_semantics=("arbitrary",), vmem_limit_bytes=_vmem_limit(est)),
    )(x, g.reshape(1, D_MODEL))


W_CHUNKS = 8


class _Stream:
    def __init__(self, n_tiles, k, n_token_tiles):
        self.n_tiles = n_tiles
        self.n_chunks = W_CHUNKS * ((n_token_tiles - 1) // W_CHUNKS)
        assert self.n_chunks >= W_CHUNKS and k % self.n_chunks == 0
        self.kc = k // self.n_chunks

    def token(self, jj, i):
        return jnp.where(jj == 0, 0, i)

    def tile(self, jj):
        return jnp.maximum(jj - 1, 0)

    def chunk_spec(self, layer, width, col_block):
        last = self.n_chunks - 1

        def index(jj, i):
            arriving = jnp.minimum(jj, self.n_tiles - 1)
            chunk = jnp.where(jj < self.n_tiles, jnp.minimum(i, last), last)
            return layer, chunk, col_block(arriving)
        return pl.BlockSpec((None, self.kc, width), index)

    def cast_chunk(self, chunk_refs, wb_ref):
        jj = pl.program_id(0)
        i = pl.program_id(1)

        @pl.when(jnp.logical_and(jj < self.n_tiles, i < self.n_chunks))
        def _():
            rows = pl.ds(pl.multiple_of(i * self.kc, self.kc), self.kc)
            lane = 0
            for ref in chunk_refs:
                width = ref.shape[-1]
                wb_ref[jj % 2, rows, lane:lane + width] = ref[...].astype(BF16)
                lane += width

    def multiplying(self):
        return pl.program_id(0) >= 1

    def weights(self, wb_ref):
        return wb_ref.at[(pl.program_id(0) + 1) % 2]


def _matmul_kernel(x_ref, w_ref, o_ref, wb_ref, *, stream):
    stream.cast_chunk([w_ref], wb_ref)
    w = stream.weights(wb_ref)

    @pl.when(stream.multiplying())
    def _():
        o_ref[...] = jnp.dot(x_ref[...], w[...], preferred_element_type=F32)


def _matmul(x, w_stack, layer, tm, tn):
    m, k = x.shape
    n = w_stack.shape[2]
    stream = _Stream(n // tn, k, m // tm)
    est = 2 * (tm * k * 2 + tm * tn * 4 + stream.kc * tn * 4) + 2 * k * tn * 2 + tm * tn * 4
    return pl.pallas_call(
        functools.partial(_matmul_kernel, stream=stream),
        name="in_proj",
        grid=(n // tn + 1, m // tm),
        in_specs=[
            pl.BlockSpec((tm, k), lambda jj, i: (stream.token(jj, i), 0)),
            stream.chunk_spec(layer, tn, lambda t: t),
        ],
        out_specs=pl.BlockSpec((tm, tn), lambda jj, i: (stream.token(jj, i), stream.tile(jj))),
        out_shape=jax.ShapeDtypeStruct((m, n), F32),
        scratch_shapes=[pltpu.VMEM((2, k, tn), BF16)],
        compiler_params=pltpu.CompilerParams(
            dimension_semantics=("arbitrary", "arbitrary"), vmem_limit_bytes=_vmem_limit(est)),
    )(x, w_stack)


def _residual_update(lhs, w, res_ref, gate_ref, o_ref, mod_row):
    y = None
    k0 = 0
    for x_ref in lhs:
        kk = x_ref.shape[1]
        part = jnp.dot(x_ref[...], w[k0:k0 + kk, :], preferred_element_type=F32)
        y = part if y is None else y + part
        k0 += kk
    o_ref[...] = res_ref[...] + gate_ref[pl.ds(mod_row, 1), :] * y


def _matmul_residual_kernel(*refs, split, tm, stream):
    n_refs = len(split) + sum(split)
    w_ref, res_ref, gate_ref, o_ref, wb_ref = refs[n_refs:]
    stream.cast_chunk([w_ref], wb_ref)
    row0 = pl.program_id(1) * tm
    mod_row = _mod_row(row0)
    w = stream.weights(wb_ref)
    lhs_refs = iter(refs[:n_refs])
    operands = [(next(lhs_refs), next(lhs_refs)) if pair else (next(lhs_refs),) * 2 for pair in split]
    for side, here in enumerate((row0 < N_CTX, row0 >= N_CTX) if any(split) else (True,)):
        @pl.when(jnp.logical_and(stream.multiplying(), here))
        def _():
            _residual_update([op[side] for op in operands], w, res_ref, gate_ref, o_ref, mod_row)


def _matmul_residual(lhs, w_stack, layer, res, mod, gate_chunk, tm, tn, name):
    m = res.shape[0]
    _, k, n = w_stack.shape
    gate_block0 = gate_chunk * (D_MODEL // tn)
    stream = _Stream(n // tn, k, m // tm)
    token, tile = stream.token, stream.tile
    est = (2 * (tm * k * 2 + 2 * tm * tn * 4 + MOD_ROWS * tn * 4 + stream.kc * tn * 4) + 2 * k * tn * 2
           + tm * tn * 4)
    n_ctx_tiles = N_CTX // tm
    lhs_specs = []
    lhs_args = []
    for x in lhs:
        if isinstance(x, tuple):
            ctx, lat = x
            lhs_specs += [
                pl.BlockSpec((tm, ctx.shape[1]), lambda j, i: (jnp.minimum(token(j, i), n_ctx_tiles - 1), 0)),
                pl.BlockSpec((tm, lat.shape[1]), lambda j, i: (jnp.maximum(token(j, i) - n_ctx_tiles, 0), 0))]
            lhs_args += [ctx, lat]
        else:
            lhs_specs.append(pl.BlockSpec((tm, x.shape[1]), lambda j, i: (token(j, i), 0)))
            lhs_args.append(x)
    split = tuple(isinstance(x, tuple) for x in lhs)
    return pl.pallas_call(
        functools.partial(_matmul_residual_kernel, split=split, tm=tm, stream=stream),
        name=name,
        grid=(n // tn + 1, m // tm),
        in_specs=lhs_specs + [
            stream.chunk_spec(layer, tn, lambda t: t),
            pl.BlockSpec((tm, tn), lambda j, i: (token(j, i), tile(j))),
            pl.BlockSpec((MOD_ROWS, tn), lambda j, i: (0, gate_block0 + tile(j))),
        ],
        out_specs=pl.BlockSpec((tm, tn), lambda j, i: (token(j, i), tile(j))),
        out_shape=jax.ShapeDtypeStruct((m, n), F32),
        scratch_shapes=[pltpu.VMEM((2, k, tn), BF16)],
        compiler_params=pltpu.CompilerParams(
            dimension_semantics=("arbitrary", "arbitrary"), vmem_limit_bytes=_vmem_limit(est)),
    )(*lhs_args, w_stack, res, mod)


FFN_TM = 1024
FFN_TN = 256


def _ffn_up_kernel(h_ref, wg_ref, wv_ref, cwg_ref, cwv_ref, cbg_ref, cbv_ref, o_ref, wb_ref):
    @pl.when(pl.program_id(1) == 0)
    def _():
        wb_ref[:, :FFN_TN] = wg_ref[...].astype(BF16)
        wb_ref[:, FFN_TN:] = wv_ref[...].astype(BF16)

    u = jnp.dot(h_ref[...], wb_ref[...], preferred_element_type=F32)
    is_ctx = pl.program_id(1) * FFN_TM < N_CTX
    g = _conv3_rows(u[:, :FFN_TN], cwg_ref[...], cbg_ref[...], SEQ, is_ctx)
    v = _conv3_rows(u[:, FFN_TN:], cwv_ref[...], cbv_ref[...], SEQ, is_ctx)
    o_ref[...] = (g * jax.nn.sigmoid(g) * v).astype(o_ref.dtype)


FFN_PAIR = 2 * FFN_TN


def _ffn_up2_kernel(h_ref, wg_ref, wva_ref, wvb_ref, cwg_ref, cwva_ref, cwvb_ref, cbg_ref, cbva_ref, cbvb_ref,
                    o_ref, wb_ref, *, stream):
    stream.cast_chunk([wg_ref, wva_ref, wvb_ref], wb_ref)
    is_ctx = pl.program_id(1) * FFN_TM < N_CTX
    w = stream.weights(wb_ref)

    @pl.when(stream.multiplying())
    def _():
        u = jnp.dot(h_ref[...], w[...], preferred_element_type=F32)
        cwg = cwg_ref[...]
        cbg = cbg_ref[...]
        for blk, (cwv_ref, cbv_ref) in enumerate(((cwva_ref, cbva_ref), (cwvb_ref, cbvb_ref))):
            lanes = slice(blk * FFN_TN, (blk + 1) * FFN_TN)
            g = _conv3_rows(u[:, lanes], cwg[:, lanes], cbg[:, lanes], SEQ, is_ctx)
            v = _conv3_rows(u[:, FFN_PAIR + blk * FFN_TN:FFN_PAIR + (blk + 1) * FFN_TN], cwv_ref[...],
                            cbv_ref[...], SEQ, is_ctx)
            o_ref[:, lanes] = (g * jax.nn.sigmoid(g) * v).astype(o_ref.dtype)


def _ffn_up2(h, w_up, layer, conv_w_t, conv_b):
    nb = D_FF // FFN_TN
    npair = nb // 2
    n_m = N_TOK // FFN_TM
    stream = _Stream(npair, D_MODEL, n_m)
    est = (2 * (FFN_TM * D_MODEL * 2 + FFN_TM * FFN_PAIR * 2 + stream.kc * 2 * FFN_PAIR * 4)
           + 2 * D_MODEL * 2 * FFN_PAIR * 2 + 12 * FFN_TM * FFN_PAIR * 4)
    pair = lambda rows: pl.BlockSpec((rows, FFN_PAIR), lambda jj, i: (0, stream.tile(jj)))
    val = lambda rows, blk: pl.BlockSpec((rows, FFN_TN), lambda jj, i: (0, nb + 2 * stream.tile(jj) + blk))
    return pl.pallas_call(
        functools.partial(_ffn_up2_kernel, stream=stream),
        name="ffn_up2",
        grid=(npair + 1, n_m),
        in_specs=[
            pl.BlockSpec((FFN_TM, D_MODEL), lambda jj, i: (stream.token(jj, i), 0)),
            stream.chunk_spec(layer, FFN_PAIR, lambda t: t),
            stream.chunk_spec(layer, FFN_TN, lambda t: nb + 2 * t),
            stream.chunk_spec(layer, FFN_TN, lambda t: nb + 2 * t + 1),
            pair(3), val(3, 0), val(3, 1), pair(1), val(1, 0), val(1, 1),
        ],
        out_specs=pl.BlockSpec((FFN_TM, FFN_PAIR), lambda jj, i: (stream.token(jj, i), stream.tile(jj))),
        out_shape=jax.ShapeDtypeStruct((N_TOK, npair * FFN_PAIR), BF16),
        scratch_shapes=[pltpu.VMEM((2, D_MODEL, 2 * FFN_PAIR), BF16)],
        compiler_params=pltpu.CompilerParams(
            dimension_semantics=("arbitrary", "arbitrary"), vmem_limit_bytes=_vmem_limit(est)),
    )(h, w_up, w_up, w_up, conv_w_t, conv_w_t, conv_w_t, conv_b, conv_b, conv_b)


def _ffn_up(h, w_up, layer, conv_w_t, conv_b, block0, n_blocks):
    nb = D_FF // FFN_TN
    est = (2 * (FFN_TM * D_MODEL * 2 + 2 * D_MODEL * FFN_TN * 4 + FFN_TM * FFN_TN * 2)
           + 2 * D_MODEL * FFN_TN * 2 + 10 * FFN_TM * FFN_TN * 4)
    return pl.pallas_call(
        _ffn_up_kernel,
        name="ffn_up",
        grid=(n_blocks, N_TOK // FFN_TM),
        in_specs=[
            pl.BlockSpec((FFN_TM, D_MODEL), lambda j, i: (i, 0)),
            pl.BlockSpec((None, D_MODEL, FFN_TN), lambda j, i: (layer, 0, block0 + j)),
            pl.BlockSpec((None, D_MODEL, FFN_TN), lambda j, i: (layer, 0, nb + block0 + j)),
            pl.BlockSpec((3, FFN_TN), lambda j, i: (0, block0 + j)),
            pl.BlockSpec((3, FFN_TN), lambda j, i: (0, nb + block0 + j)),
            pl.BlockSpec((1, FFN_TN), lambda j, i: (0, block0 + j)),
            pl.BlockSpec((1, FFN_TN), lambda j, i: (0, nb + block0 + j)),
        ],
        out_specs=pl.BlockSpec((FFN_TM, FFN_TN), lambda j, i: (i, j)),
        out_shape=jax.ShapeDtypeStruct((N_TOK, n_blocks * FFN_TN), BF16),
        scratch_shapes=[pltpu.VMEM((D_MODEL, 2 * FFN_TN), BF16)],
        compiler_params=pltpu.CompilerParams(
            dimension_semantics=("arbitrary", "arbitrary"), vmem_limit_bytes=_vmem_limit(est)),
    )(h, w_up, w_up, conv_w_t, conv_w_t, conv_b, conv_b)


FILT_PAD = 64


def _split_bf16(a):
    hi = a.astype(BF16)
    return hi, (a - hi.astype(F32)).astype(BF16)


def _dot_split(t_hi, t_lo, b):
    b_hi, b_lo = _split_bf16(b)
    return (jnp.dot(t_hi, b_hi, preferred_element_type=F32)
            + jnp.dot(t_hi, b_lo, preferred_element_type=F32)
            + jnp.dot(t_lo, b_hi, preferred_element_type=F32))


def _dft_tables(L):
    f = jnp.arange(L, dtype=jnp.int32)[:, None]
    s = jnp.arange(L, dtype=jnp.int32)[None, :]
    ang = ((f * s) % (2 * L)).astype(F32) * (math.pi / L)
    cm = jnp.cos(ang)
    nyq = jnp.where(s % 2 == 0, 1.0, -1.0).astype(F32)
    sm = jnp.where(f == 0, nyq, -jnp.sin(ang))
    return tuple(jnp.stack(_split_bf16(t)) for t in (cm, sm, sm.T))


def _filter_features(L):
    t = jnp.linspace(0.0, 1.0, L, dtype=F32)[:, None]
    w = 2.0 * math.pi * jnp.arange(L, dtype=F32)[:, None] / L
    f = jnp.linspace(1e-4, FILTER_BANDS - 1, FILTER_BANDS, dtype=F32)[None, :]
    z = jnp.concatenate([t, jnp.cos(f * w), -jnp.sin(f * w)], axis=-1)
    return jnp.pad(z, ((0, 0), (0, FILT_PAD - FILTER_EMB)))


def _filter_kernel(z_ref, w1_ref, b1_ref, w2_ref, b2_ref, w3f_ref, w3b_ref, freq_ref, delta_ref,
                   cm_ref, sm_ref, p_ref, q_ref, r_ref, hid_ref, *, L):
    z = z_ref[...]

    @pl.when(pl.program_id(0) == 0)
    def _():
        fr = freq_ref[...]
        h = jnp.sin(fr * (jnp.dot(z, w1_ref[...], precision=HIGHEST, preferred_element_type=F32) + b1_ref[...]))
        hid_ref[...] = jnp.sin(
            fr * (jnp.dot(h, w2_ref[...], precision=HIGHEST, preferred_element_type=F32) + b2_ref[...]))

    h = hid_ref[...]
    decay = jnp.exp(-z[:, 0:1] * jnp.abs(delta_ref[...]))
    h_fwd = jnp.dot(h, w3f_ref[...], precision=HIGHEST, preferred_element_type=F32) * decay
    h_bwd = jnp.dot(h, w3b_ref[...], precision=HIGHEST, preferred_element_type=F32) * decay
    pos = lax.broadcasted_iota(jnp.int32, (L, 1), 0)
    first = pos == 0
    h_bwd = jnp.where(first, 0.0, h_bwd)
    even = h_fwd + h_bwd
    odd = h_fwd - h_bwd
    sign = jnp.where((pos & 1) == 0, 1.0, -1.0)
    nyq = jnp.sum(sign * even, axis=0, keepdims=True)
    wgt = jnp.where(first, 1.0 / (2 * L), 1.0 / L)
    p = _dot_split(cm_ref[0], cm_ref[1], even) * wgt
    p_ref[...] = p
    q_ref[...] = jnp.where(first, 0.0, _dot_split(sm_ref[0], sm_ref[1], odd) * wgt)
    r_ref[...] = jnp.where(first, nyq * wgt, p)


FILT_TC = 256


def _filter_spectrum(L, z, w1p, b1, w2, b2, w3, freq, deltas, cm, sm):
    nb = D_HYENA // FILT_TC
    est = 2 * (2 * 2 * L * L * 2) + 2 * 4 * L * FILT_TC * 4 + 10 * L * FILT_TC * 4 + 4 * L * FILT_PAD * 4
    full = lambda shape: pl.BlockSpec(shape, lambda c: (0,) * len(shape))
    return pl.pallas_call(
        functools.partial(_filter_kernel, L=L),
        name=f"hyena_filter_{L}",
        grid=(nb,),
        in_specs=[
            full((L, FILT_PAD)), full((FILT_PAD, FILTER_ORDER)), full((1, FILTER_ORDER)),
            full((FILTER_ORDER, FILTER_ORDER)), full((1, FILTER_ORDER)),
            pl.BlockSpec((FILTER_ORDER, FILT_TC), lambda c: (0, c)),
            pl.BlockSpec((FILTER_ORDER, FILT_TC), lambda c: (0, nb + c)),
            full((1, FILTER_ORDER)),
            pl.BlockSpec((1, FILT_TC), lambda c: (0, c)),
            full((2, L, L)), full((2, L, L)),
        ],
        out_specs=[pl.BlockSpec((L, FILT_TC), lambda c: (0, c))] * 3,
        out_shape=[jax.ShapeDtypeStruct((L, D_HYENA), F32)] * 3,
        scratch_shapes=[pltpu.VMEM((L, FILTER_ORDER), F32)],
        compiler_params=pltpu.CompilerParams(
            dimension_semantics=("arbitrary",), vmem_limit_bytes=_vmem_limit(est)),
    )(z, w1p, b1, w2, b2, w3, w3, freq, deltas, cm, sm)


def _hyena_kernel(x0_ref, x1_ref, v_ref, cw0_ref, cw1_ref, cw2_ref, cb0_ref, cb1_ref, cb2_ref, hb_ref,
                  p_ref, q_ref, r_ref, cm_ref, sm_ref, smt_ref, gn_ref, o_ref, acc_ref, *, L, tc, nct):
    ct = pl.program_id(1)
    x0 = _conv3_rows(x0_ref[...], cw0_ref[...], cb0_ref[...])
    x1 = _conv3_rows(x1_ref[...], cw1_ref[...], cb1_ref[...])
    v = _conv3_rows(v_ref[...], cw2_ref[...], cb2_ref[...])
    u = x1 * v
    ub = u.astype(BF16)
    top = jnp.dot(cm_ref[...], ub, preferred_element_type=F32)
    bot = jnp.dot(sm_ref[...], ub, preferred_element_type=F32)
    q = q_ref[...]
    y_top = top * p_ref[...] - bot * q
    y_bot = top * q + bot * r_ref[...]
    y = (jnp.dot(cm_ref[...], y_top.astype(BF16), preferred_element_type=F32)
         + jnp.dot(smt_ref[...], y_bot.astype(BF16), preferred_element_type=F32))
    acc_ref[ct] = x0 * (y + u * hb_ref[...])

    @pl.when(ct == nct - 1)
    def _():
        ss = jnp.zeros((L, 1), F32)
        for j in range(nct):
            a = acc_ref[j]
            ss = ss + jnp.sum(a * a, axis=-1, keepdims=True)
        inv = lax.rsqrt(ss * (1.0 / D_HYENA) + RMS_EPS)
        for j in range(nct):
            o_ref[:, j * tc:(j + 1) * tc] = (acc_ref[j] * inv * gn_ref[:, j * tc:(j + 1) * tc]).astype(o_ref.dtype)


def _hyena(proj, conv_w_t, conv_b, hy_bias, spectrum, cm, sm, smt, gn, *, L, tc, nseq, row_block0):
    nct = D_HYENA // tc
    cpb = D_HYENA // tc
    est = (3 * L * L * 2 + 2 * (3 * L * tc * 4 + 3 * L * tc * 4) + nct * L * tc * 4
           + 2 * L * D_HYENA * 2 + 14 * L * tc * 4)
    once = pl.Buffered(1)
    grp = lambda g: pl.BlockSpec((L, tc), lambda s, c: (row_block0 + s, g * cpb + c))
    tap = lambda g: pl.BlockSpec((3, tc), lambda s, c: (0, g * cpb + c))
    bias = lambda g: pl.BlockSpec((1, tc), lambda s, c: (0, g * cpb + c))
    col = lambda rows: pl.BlockSpec((rows, tc), lambda s, c: (0, c))
    table = pl.BlockSpec((None, L, L), lambda s, c: (0, 0, 0), pipeline_mode=once)
    return pl.pallas_call(
        functools.partial(_hyena_kernel, L=L, tc=tc, nct=nct),
        name=f"hyena_conv_{L}",
        grid=(nseq, nct),
        in_specs=[grp(0), grp(1), grp(2), tap(0), tap(1), tap(2), bias(0), bias(1), bias(2),
                  col(1), col(L), col(L), col(L), table, table, table,
                  pl.BlockSpec((1, D_HYENA), lambda s, c: (0, 0))],
        out_specs=pl.BlockSpec((L, D_HYENA), lambda s, c: (s, 0)),
        out_shape=jax.ShapeDtypeStruct((nseq * L, D_HYENA), BF16),
        scratch_shapes=[pltpu.VMEM((nct, L, tc), F32)],
        compiler_params=pltpu.CompilerParams(
            dimension_semantics=("arbitrary", "arbitrary"), vmem_limit_bytes=_vmem_limit(est)),
    )(proj, proj, proj, conv_w_t, conv_w_t, conv_w_t, conv_b, conv_b, conv_b,
      hy_bias, *spectrum, cm, sm, smt, gn)


def _dot_nt(a, b):
    return lax.dot_general(a, b, (((1,), (1,)), ((), ())), preferred_element_type=F32)


def _head_lanes(kvh, g):
    hd = kvh * KV_GROUP + g
    return slice(hd * HEAD_DIM, (hd + 1) * HEAD_DIM)


def _group_sinks(sink_ref, kvh, rows):
    return jnp.concatenate([jnp.full((rows, 1), sink_ref[kvh * KV_GROUP + g], F32) for g in range(KV_GROUP)],
                           axis=0)


def _head_norm_store(acc_ref, gn_ref, o_ref):
    y = acc_ref[...]
    inv = lax.rsqrt(jnp.mean(y * y, axis=-1, keepdims=True) + RMS_EPS)
    o_ref[...] = (y * inv * gn_ref[...]).astype(o_ref.dtype)


def _attn_ctx_kernel(sink_ref, q_ref, k_ref, v_ref, gn_ref, o_ref, ko_ref, vo_ref, acc_ref):
    ko_ref[...] = k_ref[...]
    vo_ref[...] = v_ref[...]
    for kvh in range(N_KV_HEADS):
        ksl = slice(kvh * HEAD_DIM, (kvh + 1) * HEAD_DIM)
        kh = k_ref[:, ksl].astype(BF16)
        vh = v_ref[:, ksl].astype(BF16)
        for g in range(KV_GROUP):
            hsl = _head_lanes(kvh, g)
            s = _dot_nt(q_ref[:, hsl].astype(BF16), kh) * ATTN_SCALE
            sk = sink_ref[kvh * KV_GROUP + g]
            m = jnp.maximum(jnp.max(s, axis=-1, keepdims=True), sk)
            e = jnp.exp(s - m)
            den = jnp.sum(e, axis=-1, keepdims=True) + jnp.exp(sk - m)
            acc_ref[:, hsl] = jnp.dot((e / den).astype(BF16), vh, preferred_element_type=F32)
    _head_norm_store(acc_ref, gn_ref, o_ref)


def _attn_ctx(proj, sink, gn):
    est = 2 * (SEQ * D_ATTN * 4 + 4 * SEQ * KV_DIM * 4 + SEQ * D_ATTN * 2) + 3 * SEQ * D_ATTN * 4
    return pl.pallas_call(
        _attn_ctx_kernel,
        name="attn_ctx",
        grid=(BATCH,),
        in_specs=[
            pl.BlockSpec(memory_space=pltpu.SMEM),
            pl.BlockSpec((SEQ, D_ATTN), lambda b: (b, O_Q // D_ATTN)),
            pl.BlockSpec((SEQ, KV_DIM), lambda b: (b, O_K // KV_DIM)),
            pl.BlockSpec((SEQ, KV_DIM), lambda b: (b, O_V // KV_DIM)),
            pl.BlockSpec((1, D_ATTN), lambda b: (0, 0)),
        ],
        out_specs=[pl.BlockSpec((SEQ, D_ATTN), lambda b: (b, 0)),
                   pl.BlockSpec((SEQ, KV_DIM), lambda b: (b, 0)),
                   pl.BlockSpec((SEQ, KV_DIM), lambda b: (b, 0))],
        out_shape=[jax.ShapeDtypeStruct((N_CTX, D_ATTN), BF16),
                   jax.ShapeDtypeStruct((N_CTX, KV_DIM), F32),
                   jax.ShapeDtypeStruct((N_CTX, KV_DIM), F32)],
        scratch_shapes=[pltpu.VMEM((SEQ, D_ATTN), F32)],
        compiler_params=pltpu.CompilerParams(
            dimension_semantics=("arbitrary",), vmem_limit_bytes=_vmem_limit(est)),
    )(sink, proj, proj, proj, gn)


def _rope_tables():
    half = HEAD_DIM // 2
    quarter = half // 2
    pos = jnp.arange(DEC_SEQ, dtype=jnp.int32)
    row_pos = (pos // GRID_W).astype(F32)
    col_pos = (pos % GRID_W).astype(F32)
    inv = ROPE_BASE ** (-jnp.arange(quarter, dtype=F32) / quarter)
    lane = jnp.arange(HEAD_DIM, dtype=jnp.int32)
    p = jnp.where((lane < half)[None, :], row_pos[:, None], col_pos[:, None])
    ang = p * inv[lane % quarter][None, :]
    low = ((lane % half) < quarter)[None, :]
    cos = jnp.cos(ang)
    sin = jnp.sin(ang)
    return cos, jnp.where(low, -sin, 0.0), jnp.where(low, 0.0, sin)


ROPE_SHIFT = HEAD_DIM // 4
LAT_SPAN = BLOCK + 2 * WINDOW


def _rope(x, cos, sin_lo, sin_hi):
    return (x * cos + pltpu.roll(x, HEAD_DIM - ROPE_SHIFT, 1) * sin_lo
            + pltpu.roll(x, ROPE_SHIFT, 1) * sin_hi)


def _attn_lat_kernel(sink_ref, q_ref, k_ref, v_ref, ck_ref, cv_ref, cos_ref, slo_ref, shi_ref, gn_ref,
                     o_ref, acc_ref):
    i = pl.program_id(1)
    q0 = pl.multiple_of(i * BLOCK, BLOCK)
    k0 = pl.multiple_of(jnp.clip((i - 1) * BLOCK, 0, DEC_SEQ - LAT_SPAN), BLOCK)
    q_tab = (cos_ref[pl.ds(q0, BLOCK), :], slo_ref[pl.ds(q0, BLOCK), :], shi_ref[pl.ds(q0, BLOCK), :])
    k_tab = (cos_ref[pl.ds(k0, LAT_SPAN), :], slo_ref[pl.ds(k0, LAT_SPAN), :], shi_ref[pl.ds(k0, LAT_SPAN), :])
    q_pos = q0 + lax.broadcasted_iota(jnp.int32, (BLOCK, 1), 0)
    k_pos = k0 + lax.broadcasted_iota(jnp.int32, (1, LAT_SPAN), 1)
    valid = jnp.abs(q_pos - k_pos) <= WINDOW
    valid = jnp.concatenate([valid] * KV_GROUP, axis=0)
    for kvh in range(N_KV_HEADS):
        ksl = slice(kvh * HEAD_DIM, (kvh + 1) * HEAD_DIM)
        kw = _rope(k_ref[pl.ds(k0, LAT_SPAN), ksl], *k_tab).astype(BF16)
        vw = v_ref[pl.ds(k0, LAT_SPAN), ksl].astype(BF16)
        kc = ck_ref[0, :, ksl].astype(BF16)
        vc = cv_ref[0, :, ksl].astype(BF16)
        q = jnp.concatenate([_rope(q_ref[:, _head_lanes(kvh, g)], *q_tab) for g in range(KV_GROUP)],
                            axis=0).astype(BF16)
        sk = _group_sinks(sink_ref, kvh, BLOCK)
        s_lat = jnp.where(valid, _dot_nt(q, kw) * ATTN_SCALE, NEG_INF)
        s_ctx = _dot_nt(q, kc) * ATTN_SCALE
        m = jnp.maximum(jnp.maximum(jnp.max(s_lat, axis=-1, keepdims=True),
                                    jnp.max(s_ctx, axis=-1, keepdims=True)), sk)
        e_lat = jnp.exp(s_lat - m)
        e_ctx = jnp.exp(s_ctx - m)
        den = (jnp.sum(e_lat, axis=-1, keepdims=True) + jnp.sum(e_ctx, axis=-1, keepdims=True)
               + jnp.exp(sk - m))
        o = (jnp.dot((e_lat / den).astype(BF16), vw, preferred_element_type=F32)
             + jnp.dot((e_ctx / den).astype(BF16), vc, preferred_element_type=F32))
        for g in range(KV_GROUP):
            acc_ref[:, _head_lanes(kvh, g)] = o[g * BLOCK:(g + 1) * BLOCK]
    _head_norm_store(acc_ref, gn_ref, o_ref)


def _attn_lat(proj, cache_k, cache_v, sink, rope, gn):
    nqb = DEC_SEQ // BLOCK
    est = (2 * (BLOCK * D_ATTN * 4 + 2 * DEC_SEQ * KV_DIM * 4 + 2 * PAST_LEN * KV_DIM * 4
                + 3 * DEC_SEQ * HEAD_DIM * 4 + BLOCK * D_ATTN * 2) + 8 * BLOCK * D_ATTN * 4)
    lat_row_block0 = N_CTX // BLOCK
    lat_seq_block0 = N_CTX // DEC_SEQ
    tab = pl.BlockSpec((DEC_SEQ, HEAD_DIM), lambda b, i: (0, 0))
    return pl.pallas_call(
        _attn_lat_kernel,
        name="attn_lat",
        grid=(DEC_BATCH, nqb),
        in_specs=[
            pl.BlockSpec(memory_space=pltpu.SMEM),
            pl.BlockSpec((BLOCK, D_ATTN), lambda b, i: (lat_row_block0 + b * nqb + i, O_Q // D_ATTN)),
            pl.BlockSpec((DEC_SEQ, KV_DIM), lambda b, i: (lat_seq_block0 + b, O_K // KV_DIM)),
            pl.BlockSpec((DEC_SEQ, KV_DIM), lambda b, i: (lat_seq_block0 + b, O_V // KV_DIM)),
            pl.BlockSpec((1, PAST_LEN, KV_DIM), lambda b, i: (b, 0, 0)),
            pl.BlockSpec((1, PAST_LEN, KV_DIM), lambda b, i: (b, 0, 0)),
            tab, tab, tab,
            pl.BlockSpec((1, D_ATTN), lambda b, i: (0, 0)),
        ],
        out_specs=pl.BlockSpec((BLOCK, D_ATTN), lambda b, i: (b * nqb + i, 0)),
        out_shape=jax.ShapeDtypeStruct((N_LAT, D_ATTN), BF16),
        scratch_shapes=[pltpu.VMEM((BLOCK, D_ATTN), F32)],
        compiler_params=pltpu.CompilerParams(
            dimension_semantics=("arbitrary", "arbitrary"), vmem_limit_bytes=_vmem_limit(est)),
    )(sink, proj, proj, proj, cache_k, cache_v, *rope, gn)


def kernel(x_prompt, x_sample, cache_k, cache_v, c, c_ctx, w_ada, b_ada, norm_mix, w_in, hy_conv_w, hy_conv_b, filt_w1, filt_b1, filt_w2, filt_b2, filt_w3, filt_freq, hy_bias, attn_sink, gn_hyena, gn_attn, w_out, norm_ffn, w_up, ffn_conv_w, ffn_conv_b, w_down, final_norm):
    x = jnp.concatenate([x_prompt.reshape(N_CTX, D_MODEL), x_sample.reshape(N_LAT, D_MODEL)], axis=0)
    cond = jnp.concatenate([c_ctx[None, :], c, jnp.zeros((MOD_ROWS - 1 - DEC_BATCH, D_MODEL), F32)], axis=0)
    mod_all = _modulation(cond, w_ada, b_ada)

    rope = _rope_tables()
    tables = {L: _dft_tables(L) for L in (SEQ, DEC_SEQ)}
    feats = {L: _filter_features(L) for L in (SEQ, DEC_SEQ)}
    deltas = jnp.linspace(MIN_DECAY, MAX_DECAY, D_HYENA, dtype=F32)[None, :]

    new_k = []
    new_v = []
    for i in range(DEPTH):
        mod = mod_all[i]
        h = _prenorm(x, norm_mix[i], mod, 0, 1)
        proj = _matmul(h, w_in, i, 1024, 1024)

        hy_w_t = hy_conv_w[i].T
        hy_b = hy_conv_b[i][None, :]
        w1p = jnp.pad(filt_w1[i], ((0, FILT_PAD - FILTER_EMB), (0, 0)))
        y_a = []
        for L, tc, nseq, rb0 in ((SEQ, 1024, BATCH, 0), (DEC_SEQ, 256, DEC_BATCH, N_CTX // DEC_SEQ)):
            cm, sm, smt = tables[L]
            spectrum = _filter_spectrum(L, feats[L], w1p, filt_b1[i][None, :], filt_w2[i], filt_b2[i][None, :],
                                        filt_w3[i], filt_freq[i][None, :], deltas, cm, sm)
            y_a.append(_hyena(proj, hy_w_t, hy_b, hy_bias[i][None, :], spectrum, cm, sm, smt,
                              gn_hyena[i][None, :], L=L, tc=tc, nseq=nseq, row_block0=rb0))
        gn_a = gn_attn[i][None, :]
        yb_ctx, k_ctx, v_ctx = _attn_ctx(proj, attn_sink[i], gn_a)
        yb_lat = _attn_lat(proj, cache_k[:, i].reshape(DEC_BATCH, PAST_LEN, KV_DIM),
                           cache_v[:, i].reshape(DEC_BATCH, PAST_LEN, KV_DIM), attn_sink[i], rope, gn_a)
        x = _matmul_residual([tuple(y_a), (yb_ctx, yb_lat)], w_out, i, x, mod, 2, 512, 1024, "out_proj")

        h = _prenorm(x, norm_ffn[i], mod, 3, 4)
        ffn_w_t = ffn_conv_w[i].T
        ffn_b = ffn_conv_b[i][None, :]
        n_ffn_blocks = D_FF // FFN_TN
        acts = [_ffn_up2(h, w_up, i, ffn_w_t, ffn_b)]
        if n_ffn_blocks % 2:
            acts.append(_ffn_up(h, w_up, i, ffn_w_t, ffn_b, n_ffn_blocks - 1, 1))
        x = _matmul_residual(acts, w_down, i, x, mod, 5, 512, 512, "ffn_down")

        new_k.append(k_ctx.reshape(BATCH, SEQ, N_KV_HEADS, HEAD_DIM))
        new_v.append(v_ctx.reshape(BATCH, SEQ, N_KV_HEADS, HEAD_DIM))

    y_prompt = _final_norm(x, final_norm, 0, N_CTX).reshape(BATCH, SEQ, D_MODEL)
    y_sample = _final_norm(x, final_norm, N_CTX // NORM_TR, N_LAT).reshape(DEC_BATCH, DEC_SEQ, D_MODEL)
    return (y_prompt, y_sample, jnp.stack(new_k, axis=1), jnp.stack(new_v, axis=1))
```

```python
import functools
import math

import jax
import jax.numpy as jnp
from jax import lax
from jax.experimental import pallas as pl
from jax.experimental.pallas import tpu as pltpu

F32 = jnp.float32
BF16 = jnp.bfloat16
HIGHEST = lax.Precision.HIGHEST

D_MODEL = 4096
BATCH = 32
SEQ = 256
DEPTH = 4
DEC_BATCH = 2
DEC_SEQ = 1024
PAST_LEN = 512
GRID_W = 64
D_HYENA = D_MODEL // 2
D_ATTN = D_MODEL - D_HYENA
HEAD_DIM = 128
N_HEADS = D_ATTN // HEAD_DIM
N_KV_HEADS = 4
KV_GROUP = N_HEADS // N_KV_HEADS
KV_DIM = N_KV_HEADS * HEAD_DIM
WINDOW = 128
BLOCK = 128
ATTN_SCALE = 1.0 / math.sqrt(HEAD_DIM)
ROPE_BASE = 10000.0
FILTER_EMB = 33
FILTER_BANDS = (FILTER_EMB - 1) // 2
FILTER_ORDER = 64
DECAY_TARGET = 1e-2
MAX_DECAY = math.log(DECAY_TARGET) / 0.3
MIN_DECAY = math.log(DECAY_TARGET) / 1.5
D_FF = 11008
RMS_EPS = 1e-6
NEG_INF = -1e30
O_Q = 3 * D_HYENA
O_K = O_Q + D_ATTN
O_V = O_K + KV_DIM
D_IN = O_V + KV_DIM

N_CTX = BATCH * SEQ
N_LAT = DEC_BATCH * DEC_SEQ
N_TOK = N_CTX + N_LAT
MOD_ROWS = 8
N_MOD = 6 * D_MODEL

V7X_VMEM_BYTES = 64 * 1024 * 1024
V7X_VMEM_USABLE = 58 * 1024 * 1024
MIB = 1024 * 1024


def _vmem_limit(estimate_bytes):
    return int(min(V7X_VMEM_USABLE, estimate_bytes + estimate_bytes // 4 + 2 * MIB))


def _mod_row(row_start):
    return jnp.where(row_start < N_CTX, 0, 1 + (row_start - N_CTX) // DEC_SEQ)


def _seq_len(row_start):
    return jnp.where(row_start < N_CTX, SEQ, DEC_SEQ)


SUBLANES = 8


def _zero_rows_at(x, sub, breaks):
    pick = lax.broadcasted_iota(jnp.int32, (SUBLANES, 1), 0) == sub
    pieces = []
    at = 0
    for start in sorted(breaks):
        if start > at:
            pieces.append(x[at:start])
        cond = pick if breaks[start] is True else jnp.logical_and(pick, breaks[start])
        pieces.append(jnp.where(cond, 0.0, x[start:start + SUBLANES]))
        at = start + SUBLANES
    if at < x.shape[0]:
        pieces.append(x[at:])
    return jnp.concatenate(pieces, axis=0)


def _conv3_rows(u, w, b, seq_rows=None, split=False):
    rows = u.shape[0]
    seq_rows = rows if seq_rows is None else seq_rows
    starts = range(0, rows, seq_rows)
    up = _zero_rows_at(pltpu.roll(u, 1, 0), 0, {s: (True if s == 0 else split) for s in starts})
    un = _zero_rows_at(pltpu.roll(u, rows - 1, 0), SUBLANES - 1,
                       {s + seq_rows - SUBLANES: (True if s + seq_rows == rows else split) for s in starts})
    return up * w[0:1, :] + u * w[1:2, :] + un * w[2:3, :] + b


MOD_TN = 512


def _mod_kernel(c_ref, w_ref, b_ref, o_ref):
    c = c_ref[...]
    s = (c * jax.nn.sigmoid(c)).astype(BF16)
    o_ref[0] = jnp.dot(s, w_ref[0].astype(BF16), preferred_element_type=F32) + b_ref[0]


def _modulation(cond, w_ada, b_ada):
    est = 2 * D_MODEL * MOD_TN * 4 + D_MODEL * MOD_TN * 2 + 4 * MOD_ROWS * D_MODEL * 4
    return pl.pallas_call(
        _mod_kernel,
        name="adaln_mod",
        grid=(DEPTH, N_MOD // MOD_TN),
        in_specs=[
            pl.BlockSpec((MOD_ROWS, D_MODEL), lambda l, n: (0, 0)),
            pl.BlockSpec((1, D_MODEL, MOD_TN), lambda l, n: (l, 0, n)),
            pl.BlockSpec((1, 1, MOD_TN), lambda l, n: (l, 0, n)),
        ],
        out_specs=pl.BlockSpec((1, MOD_ROWS, MOD_TN), lambda l, n: (l, 0, n)),
        out_shape=jax.ShapeDtypeStruct((DEPTH, MOD_ROWS, N_MOD), F32),
        compiler_params=pltpu.CompilerParams(
            dimension_semantics=("arbitrary", "arbitrary"), vmem_limit_bytes=_vmem_limit(est)),
    )(cond, w_ada, b_ada.reshape(DEPTH, 1, N_MOD))


NORM_TR = 512


def _rows_specs(x, tm, token):
    if not isinstance(x, tuple):
        return [pl.BlockSpec((tm, x.shape[1]), lambda *ids: (token(*ids), 0))], [x]
    ctx, lat = x
    n_ctx_tiles = N_CTX // tm
    return [pl.BlockSpec((tm, ctx.shape[1]), lambda *ids: (jnp.minimum(token(*ids), n_ctx_tiles - 1), 0)),
            pl.BlockSpec((tm, lat.shape[1]), lambda *ids: (jnp.maximum(token(*ids) - n_ctx_tiles, 0), 0))], [ctx, lat]


def _sides(row0, any_pair):
    return ((0, row0 < N_CTX), (1, row0 >= N_CTX)) if any_pair else ((0, True),)


def _prenorm_kernel(*refs, pair):
    x_refs = refs[:1 + pair]
    g_ref, shift_ref, scale_ref, o_ref = refs[1 + pair:]
    row0 = pl.program_id(0) * NORM_TR
    r = _mod_row(row0)
    for side, here in _sides(row0, pair):
        @pl.when(here)
        def _():
            x = x_refs[side][...]
            y = x * lax.rsqrt(jnp.mean(x * x, axis=-1, keepdims=True) + RMS_EPS) * g_ref[...]
            o_ref[...] = (y * (1.0 + scale_ref[pl.ds(r, 1), :]) + shift_ref[pl.ds(r, 1), :]).astype(o_ref.dtype)


def _prenorm(x, g, mod, shift_chunk, scale_chunk):
    est = 2 * NORM_TR * D_MODEL * (4 + 2) + 3 * NORM_TR * D_MODEL * 4
    x_specs, x_args = _rows_specs(x, NORM_TR, lambda i: i)
    return pl.pallas_call(
        functools.partial(_prenorm_kernel, pair=isinstance(x, tuple)),
        name="prenorm",
        grid=(N_TOK // NORM_TR,),
        in_specs=x_specs + [
            pl.BlockSpec((1, D_MODEL), lambda i: (0, 0)),
            pl.BlockSpec((MOD_ROWS, D_MODEL), lambda i: (0, shift_chunk)),
            pl.BlockSpec((MOD_ROWS, D_MODEL), lambda i: (0, scale_chunk)),
        ],
        out_specs=pl.BlockSpec((NORM_TR, D_MODEL), lambda i: (i, 0)),
        out_shape=jax.ShapeDtypeStruct((N_TOK, D_MODEL), BF16),
        compiler_params=pltpu.CompilerParams(
            dimension_semantics=("arbitrary",), vmem_limit_bytes=_vmem_limit(est)),
    )(*x_args, g.reshape(1, D_MODEL), mod, mod)


def _final_norm_kernel(x_ref, g_ref, o_ref):
    x = x_ref[...]
    o_ref[...] = x * lax.rsqrt(jnp.mean(x * x, axis=-1, keepdims=True) + RMS_EPS) * g_ref[...]


def _final_norm(x, g, row_block0, rows):
    est = 4 * NORM_TR * D_MODEL * 4 + 2 * NORM_TR * D_MODEL * 4
    return pl.pallas_call(
        _final_norm_kernel,
        name="final_norm",
        grid=(rows // NORM_TR,),
        in_specs=[
            pl.BlockSpec((NORM_TR, D_MODEL), lambda i: (row_block0 + i, 0)),
            pl.BlockSpec((1, D_MODEL), lambda i: (0, 0)),
        ],
        out_specs=pl.BlockSpec((NORM_TR, D_MODEL), lambda i: (i, 0)),
        out_shape=jax.ShapeDtypeStruct((rows, D_MODEL), F32),
        compiler_params=pltpu.CompilerParams(
            dimension_semantics=("arbitrary",), vmem_limit_bytes=_vmem_limit(est)),
    )(x, g.reshape(1, D_MODEL))


W_CHUNKS = 8


class _Stream:
    def __init__(self, n_tiles, k, n_token_tiles):
        self.n_tiles = n_tiles
        self.n_chunks = W_CHUNKS * ((n_token_tiles - 1) // W_CHUNKS)
        assert self.n_chunks >= W_CHUNKS and k % self.n_chunks == 0
        self.kc = k // self.n_chunks

    def token(self, jj, i):
        return jnp.where(jj == 0, 0, i)

    def tile(self, jj):
        return jnp.maximum(jj - 1, 0)

    def chunk_spec(self, layer, width, col_block):
        last = self.n_chunks - 1

        def index(jj, i):
            arriving = jnp.minimum(jj, self.n_tiles - 1)
            chunk = jnp.where(jj < self.n_tiles, jnp.minimum(i, last), last)
            return layer, chunk, col_block(arriving)
        return pl.BlockSpec((None, self.kc, width), index)

    def cast_chunk(self, chunk_refs, wb_ref):
        jj = pl.program_id(0)
        i = pl.program_id(1)

        @pl.when(jnp.logical_and(jj < self.n_tiles, i < self.n_chunks))
        def _():
            rows = pl.ds(pl.multiple_of(i * self.kc, self.kc), self.kc)
            lane = 0
            for ref in chunk_refs:
                width = ref.shape[-1]
                wb_ref[jj % 2, rows, lane:lane + width] = ref[...].astype(BF16)
                lane += width

    def multiplying(self):
        return pl.program_id(0) >= 1

    def weights(self, wb_ref):
        return wb_ref.at[(pl.program_id(0) + 1) % 2]


def _matmul_kernel(x_ref, w_ref, o_ref, wb_ref, *, stream):
    stream.cast_chunk([w_ref], wb_ref)
    w = stream.weights(wb_ref)

    @pl.when(stream.multiplying())
    def _():
        o_ref[...] = jnp.dot(x_ref[...], w[...], preferred_element_type=F32)


def _matmul(x, w_stack, layer, tm, tn):
    m, k = x.shape
    n = w_stack.shape[2]
    stream = _Stream(n // tn, k, m // tm)
    est = 2 * (tm * k * 2 + tm * tn * 4 + stream.kc * tn * 4) + 2 * k * tn * 2 + tm * tn * 4
    return pl.pallas_call(
        functools.partial(_matmul_kernel, stream=stream),
        name="in_proj",
        grid=(n // tn + 1, m // tm),
        in_specs=[
            pl.BlockSpec((tm, k), lambda jj, i: (stream.token(jj, i), 0)),
            stream.chunk_spec(layer, tn, lambda t: t),
        ],
        out_specs=pl.BlockSpec((tm, tn), lambda jj, i: (stream.token(jj, i), stream.tile(jj))),
        out_shape=jax.ShapeDtypeStruct((m, n), F32),
        scratch_shapes=[pltpu.VMEM((2, k, tn), BF16)],
        compiler_params=pltpu.CompilerParams(
            dimension_semantics=("arbitrary", "arbitrary"), vmem_limit_bytes=_vmem_limit(est)),
    )(x, w_stack)


def _residual_update(lhs, w, res_ref, gate_ref, o_ref, mod_row):
    y = None
    k0 = 0
    for x_ref in lhs:
        kk = x_ref.shape[1]
        part = jnp.dot(x_ref[...], w[k0:k0 + kk, :], preferred_element_type=F32)
        y = part if y is None else y + part
        k0 += kk
    o_ref[...] = res_ref[...] + gate_ref[pl.ds(mod_row, 1), :] * y


def _matmul_residual_kernel(*refs, split, tm, stream):
    rest = iter(refs)
    operands = [(next(rest), next(rest)) if pair else (next(rest),) * 2 for pair in split[:-1]]
    w_ref = next(rest)
    res = (next(rest), next(rest)) if split[-1] else (next(rest),) * 2
    gate_ref, o_ref, wb_ref = rest
    stream.cast_chunk([w_ref], wb_ref)
    row0 = pl.program_id(1) * tm
    mod_row = _mod_row(row0)
    w = stream.weights(wb_ref)
    for side, here in _sides(row0, any(split)):
        @pl.when(jnp.logical_and(stream.multiplying(), here))
        def _():
            _residual_update([op[side] for op in operands], w, res[side], gate_ref, o_ref, mod_row)


def _matmul_residual(lhs, w_stack, layer, res, mod, gate_chunk, tm, tn, name):
    m = N_TOK
    _, k, n = w_stack.shape
    gate_block0 = gate_chunk * (D_MODEL // tn)
    stream = _Stream(n // tn, k, m // tm)
    token, tile = stream.token, stream.tile
    lhs_specs = []
    lhs_args = []
    for x in lhs:
        specs, args = _rows_specs(x, tm, token)
        lhs_specs += specs
        lhs_args += args
    res_args = list(res) if isinstance(res, tuple) else [res]
    n_ctx_tiles = N_CTX // tm
    res_rows = ([lambda j, i: jnp.minimum(token(j, i), n_ctx_tiles - 1),
                 lambda j, i: jnp.maximum(token(j, i) - n_ctx_tiles, 0)] if isinstance(res, tuple) else [token])
    res_specs = [pl.BlockSpec((tm, tn), lambda j, i, row=row: (row(j, i), tile(j))) for row in res_rows]
    split = tuple(isinstance(x, tuple) for x in lhs) + (isinstance(res, tuple),)
    lhs_bytes = sum(tm * a.shape[1] * 2 for a in lhs_args)
    est = (2 * (lhs_bytes + (len(res_args) + 1) * tm * tn * 4 + MOD_ROWS * tn * 4 + stream.kc * tn * 4)
           + 2 * k * tn * 2 + tm * tn * 4)
    return pl.pallas_call(
        functools.partial(_matmul_residual_kernel, split=split, tm=tm, stream=stream),
        name=name,
        grid=(n // tn + 1, m // tm),
        in_specs=lhs_specs + [
            stream.chunk_spec(layer, tn, lambda t: t),
        ] + res_specs + [
            pl.BlockSpec((MOD_ROWS, tn), lambda j, i: (0, gate_block0 + tile(j))),
        ],
        out_specs=pl.BlockSpec((tm, tn), lambda j, i: (token(j, i), tile(j))),
        out_shape=jax.ShapeDtypeStruct((m, n), F32),
        scratch_shapes=[pltpu.VMEM((2, k, tn), BF16)],
        compiler_params=pltpu.CompilerParams(
            dimension_semantics=("arbitrary", "arbitrary"), vmem_limit_bytes=_vmem_limit(est)),
    )(*lhs_args, w_stack, *res_args, mod)


FFN_TM = 1024
FFN_TN = 256


def _ffn_up_kernel(h_ref, wg_ref, wv_ref, cwg_ref, cwv_ref, cbg_ref, cbv_ref, o_ref, wb_ref):
    @pl.when(pl.program_id(1) == 0)
    def _():
        wb_ref[:, :FFN_TN] = wg_ref[...].astype(BF16)
        wb_ref[:, FFN_TN:] = wv_ref[...].astype(BF16)

    u = jnp.dot(h_ref[...], wb_ref[...], preferred_element_type=F32)
    is_ctx = pl.program_id(1) * FFN_TM < N_CTX
    g = _conv3_rows(u[:, :FFN_TN], cwg_ref[...], cbg_ref[...], SEQ, is_ctx)
    v = _conv3_rows(u[:, FFN_TN:], cwv_ref[...], cbv_ref[...], SEQ, is_ctx)
    o_ref[...] = (g * jax.nn.sigmoid(g) * v).astype(o_ref.dtype)


FFN_PAIR = 2 * FFN_TN


def _ffn_up2_kernel(h_ref, wg_ref, wva_ref, wvb_ref, cwg_ref, cwva_ref, cwvb_ref, cbg_ref, cbva_ref, cbvb_ref,
                    o_ref, wb_ref, *, stream):
    stream.cast_chunk([wg_ref, wva_ref, wvb_ref], wb_ref)
    is_ctx = pl.program_id(1) * FFN_TM < N_CTX
    w = stream.weights(wb_ref)

    @pl.when(stream.multiplying())
    def _():
        u = jnp.dot(h_ref[...], w[...], preferred_element_type=F32)
        cwg = cwg_ref[...]
        cbg = cbg_ref[...]
        for blk, (cwv_ref, cbv_ref) in enumerate(((cwva_ref, cbva_ref), (cwvb_ref, cbvb_ref))):
            lanes = slice(blk * FFN_TN, (blk + 1) * FFN_TN)
            g = _conv3_rows(u[:, lanes], cwg[:, lanes], cbg[:, lanes], SEQ, is_ctx)
            v = _conv3_rows(u[:, FFN_PAIR + blk * FFN_TN:FFN_PAIR + (blk + 1) * FFN_TN], cwv_ref[...],
                            cbv_ref[...], SEQ, is_ctx)
            o_ref[:, lanes] = (g * jax.nn.sigmoid(g) * v).astype(o_ref.dtype)


def _ffn_up2(h, w_up, layer, conv_w_t, conv_b):
    nb = D_FF // FFN_TN
    npair = nb // 2
    n_m = N_TOK // FFN_TM
    stream = _Stream(npair, D_MODEL, n_m)
    est = (2 * (FFN_TM * D_MODEL * 2 + FFN_TM * FFN_PAIR * 2 + stream.kc * 2 * FFN_PAIR * 4)
           + 2 * D_MODEL * 2 * FFN_PAIR * 2 + 12 * FFN_TM * FFN_PAIR * 4)
    pair = lambda rows: pl.BlockSpec((rows, FFN_PAIR), lambda jj, i: (0, stream.tile(jj)))
    val = lambda rows, blk: pl.BlockSpec((rows, FFN_TN), lambda jj, i: (0, nb + 2 * stream.tile(jj) + blk))
    return pl.pallas_call(
        functools.partial(_ffn_up2_kernel, stream=stream),
        name="ffn_up2",
        grid=(npair + 1, n_m),
        in_specs=[
            pl.BlockSpec((FFN_TM, D_MODEL), lambda jj, i: (stream.token(jj, i), 0)),
            stream.chunk_spec(layer, FFN_PAIR, lambda t: t),
            stream.chunk_spec(layer, FFN_TN, lambda t: nb + 2 * t),
            stream.chunk_spec(layer, FFN_TN, lambda t: nb + 2 * t + 1),
            pair(3), val(3, 0), val(3, 1), pair(1), val(1, 0), val(1, 1),
        ],
        out_specs=pl.BlockSpec((FFN_TM, FFN_PAIR), lambda jj, i: (stream.token(jj, i), stream.tile(jj))),
        out_shape=jax.ShapeDtypeStruct((N_TOK, npair * FFN_PAIR), BF16),
        scratch_shapes=[pltpu.VMEM((2, D_MODEL, 2 * FFN_PAIR), BF16)],
        compiler_params=pltpu.CompilerParams(
            dimension_semantics=("arbitrary", "arbitrary"), vmem_limit_bytes=_vmem_limit(est)),
    )(h, w_up, w_up, w_up, conv_w_t, conv_w_t, conv_w_t, conv_b, conv_b, conv_b)


def _ffn_up(h, w_up, layer, conv_w_t, conv_b, block0, n_blocks):
    nb = D_FF // FFN_TN
    est = (2 * (FFN_TM * D_MODEL * 2 + 2 * D_MODEL * FFN_TN * 4 + FFN_TM * FFN_TN * 2)
           + 2 * D_MODEL * FFN_TN * 2 + 10 * FFN_TM * FFN_TN * 4)
    return pl.pallas_call(
        _ffn_up_kernel,
        name="ffn_up",
        grid=(n_blocks, N_TOK // FFN_TM),
        in_specs=[
            pl.BlockSpec((FFN_TM, D_MODEL), lambda j, i: (i, 0)),
            pl.BlockSpec((None, D_MODEL, FFN_TN), lambda j, i: (layer, 0, block0 + j)),
            pl.BlockSpec((None, D_MODEL, FFN_TN), lambda j, i: (layer, 0, nb + block0 + j)),
            pl.BlockSpec((3, FFN_TN), lambda j, i: (0, block0 + j)),
            pl.BlockSpec((3, FFN_TN), lambda j, i: (0, nb + block0 + j)),
            pl.BlockSpec((1, FFN_TN), lambda j, i: (0, block0 + j)),
            pl.BlockSpec((1, FFN_TN), lambda j, i: (0, nb + block0 + j)),
        ],
        out_specs=pl.BlockSpec((FFN_TM, FFN_TN), lambda j, i: (i, j)),
        out_shape=jax.ShapeDtypeStruct((N_TOK, n_blocks * FFN_TN), BF16),
        scratch_shapes=[pltpu.VMEM((D_MODEL, 2 * FFN_TN), BF16)],
        compiler_params=pltpu.CompilerParams(
            dimension_semantics=("arbitrary", "arbitrary"), vmem_limit_bytes=_vmem_limit(est)),
    )(h, w_up, w_up, conv_w_t, conv_w_t, conv_b, conv_b)


FILT_PAD = 64


def _split_bf16(a):
    hi = a.astype(BF16)
    return hi, (a - hi.astype(F32)).astype(BF16)


def _dot_split(t_hi, t_lo, b):
    b_hi, b_lo = _split_bf16(b)
    return (jnp.dot(t_hi, b_hi, preferred_element_type=F32)
            + jnp.dot(t_hi, b_lo, preferred_element_type=F32)
            + jnp.dot(t_lo, b_hi, preferred_element_type=F32))


def _dft_tables(L):
    f = jnp.arange(L, dtype=jnp.int32)[:, None]
    s = jnp.arange(L, dtype=jnp.int32)[None, :]
    ang = ((f * s) % (2 * L)).astype(F32) * (math.pi / L)
    cm = jnp.cos(ang)
    nyq = jnp.where(s % 2 == 0, 1.0, -1.0).astype(F32)
    sm = jnp.where(f == 0, nyq, -jnp.sin(ang))
    return tuple(jnp.stack(_split_bf16(t)) for t in (cm, sm, sm.T))


def _filter_features(L):
    t = jnp.linspace(0.0, 1.0, L, dtype=F32)[:, None]
    w = 2.0 * math.pi * jnp.arange(L, dtype=F32)[:, None] / L
    f = jnp.linspace(1e-4, FILTER_BANDS - 1, FILTER_BANDS, dtype=F32)[None, :]
    z = jnp.concatenate([t, jnp.cos(f * w), -jnp.sin(f * w)], axis=-1)
    return jnp.pad(z, ((0, 0), (0, FILT_PAD - FILTER_EMB)))


def _filter_kernel(z_ref, w1_ref, b1_ref, w2_ref, b2_ref, w3f_ref, w3b_ref, freq_ref, delta_ref,
                   cm_ref, sm_ref, p_ref, q_ref, r_ref, hid_ref, *, L):
    z = z_ref[...]

    @pl.when(pl.program_id(0) == 0)
    def _():
        fr = freq_ref[...]
        h = jnp.sin(fr * (jnp.dot(z, w1_ref[...], precision=HIGHEST, preferred_element_type=F32) + b1_ref[...]))
        hid_ref[...] = jnp.sin(
            fr * (jnp.dot(h, w2_ref[...], precision=HIGHEST, preferred_element_type=F32) + b2_ref[...]))

    h = hid_ref[...]
    decay = jnp.exp(-z[:, 0:1] * jnp.abs(delta_ref[...]))
    h_fwd = jnp.dot(h, w3f_ref[...], precision=HIGHEST, preferred_element_type=F32) * decay
    h_bwd = jnp.dot(h, w3b_ref[...], precision=HIGHEST, preferred_element_type=F32) * decay
    pos = lax.broadcasted_iota(jnp.int32, (L, 1), 0)
    first = pos == 0
    h_bwd = jnp.where(first, 0.0, h_bwd)
    even = h_fwd + h_bwd
    odd = h_fwd - h_bwd
    sign = jnp.where((pos & 1) == 0, 1.0, -1.0)
    nyq = jnp.sum(sign * even, axis=0, keepdims=True)
    wgt = jnp.where(first, 1.0 / (2 * L), 1.0 / L)
    p = _dot_split(cm_ref[0], cm_ref[1], even) * wgt
    p_ref[...] = p
    q_ref[...] = jnp.where(first, 0.0, _dot_split(sm_ref[0], sm_ref[1], odd) * wgt)
    r_ref[...] = jnp.where(first, nyq * wgt, p)


FILT_TC = 256


def _filter_spectrum(L, z, w1p, b1, w2, b2, w3, freq, deltas, cm, sm):
    nb = D_HYENA // FILT_TC
    est = 2 * (2 * 2 * L * L * 2) + 2 * 4 * L * FILT_TC * 4 + 10 * L * FILT_TC * 4 + 4 * L * FILT_PAD * 4
    full = lambda shape: pl.BlockSpec(shape, lambda c: (0,) * len(shape))
    return pl.pallas_call(
        functools.partial(_filter_kernel, L=L),
        name=f"hyena_filter_{L}",
        grid=(nb,),
        in_specs=[
            full((L, FILT_PAD)), full((FILT_PAD, FILTER_ORDER)), full((1, FILTER_ORDER)),
            full((FILTER_ORDER, FILTER_ORDER)), full((1, FILTER_ORDER)),
            pl.BlockSpec((FILTER_ORDER, FILT_TC), lambda c: (0, c)),
            pl.BlockSpec((FILTER_ORDER, FILT_TC), lambda c: (0, nb + c)),
            full((1, FILTER_ORDER)),
            pl.BlockSpec((1, FILT_TC), lambda c: (0, c)),
            full((2, L, L)), full((2, L, L)),
        ],
        out_specs=[pl.BlockSpec((L, FILT_TC), lambda c: (0, c))] * 3,
        out_shape=[jax.ShapeDtypeStruct((L, D_HYENA), F32)] * 3,
        scratch_shapes=[pltpu.VMEM((L, FILTER_ORDER), F32)],
        compiler_params=pltpu.CompilerParams(
            dimension_semantics=("arbitrary",), vmem_limit_bytes=_vmem_limit(est)),
    )(z, w1p, b1, w2, b2, w3, w3, freq, deltas, cm, sm)


def _hyena_kernel(x0_ref, x1_ref, v_ref, cw0_ref, cw1_ref, cw2_ref, cb0_ref, cb1_ref, cb2_ref, hb_ref,
                  p_ref, q_ref, r_ref, cm_ref, sm_ref, smt_ref, gn_ref, o_ref, acc_ref, *, L, tc, nct):
    ct = pl.program_id(1)
    x0 = _conv3_rows(x0_ref[...], cw0_ref[...], cb0_ref[...])
    x1 = _conv3_rows(x1_ref[...], cw1_ref[...], cb1_ref[...])
    v = _conv3_rows(v_ref[...], cw2_ref[...], cb2_ref[...])
    u = x1 * v
    ub = u.astype(BF16)
    top = jnp.dot(cm_ref[...], ub, preferred_element_type=F32)
    bot = jnp.dot(sm_ref[...], ub, preferred_element_type=F32)
    q = q_ref[...]
    y_top = top * p_ref[...] - bot * q
    y_bot = top * q + bot * r_ref[...]
    y = (jnp.dot(cm_ref[...], y_top.astype(BF16), preferred_element_type=F32)
         + jnp.dot(smt_ref[...], y_bot.astype(BF16), preferred_element_type=F32))
    acc_ref[ct] = x0 * (y + u * hb_ref[...])

    @pl.when(ct == nct - 1)
    def _():
        ss = jnp.zeros((L, 1), F32)
        for j in range(nct):
            a = acc_ref[j]
            ss = ss + jnp.sum(a * a, axis=-1, keepdims=True)
        inv = lax.rsqrt(ss * (1.0 / D_HYENA) + RMS_EPS)
        for j in range(nct):
            o_ref[:, j * tc:(j + 1) * tc] = (acc_ref[j] * inv * gn_ref[:, j * tc:(j + 1) * tc]).astype(o_ref.dtype)


def _hyena(proj, conv_w_t, conv_b, hy_bias, spectrum, cm, sm, smt, gn, *, L, tc, nseq, row_block0):
    nct = D_HYENA // tc
    cpb = D_HYENA // tc
    est = (3 * L * L * 2 + 2 * (3 * L * tc * 4 + 3 * L * tc * 4) + nct * L * tc * 4
           + 2 * L * D_HYENA * 2 + 14 * L * tc * 4)
    once = pl.Buffered(1)
    grp = lambda g: pl.BlockSpec((L, tc), lambda s, c: (row_block0 + s, g * cpb + c))
    tap = lambda g: pl.BlockSpec((3, tc), lambda s, c: (0, g * cpb + c))
    bias = lambda g: pl.BlockSpec((1, tc), lambda s, c: (0, g * cpb + c))
    col = lambda rows: pl.BlockSpec((rows, tc), lambda s, c: (0, c))
    table = pl.BlockSpec((None, L, L), lambda s, c: (0, 0, 0), pipeline_mode=once)
    return pl.pallas_call(
        functools.partial(_hyena_kernel, L=L, tc=tc, nct=nct),
        name=f"hyena_conv_{L}",
        grid=(nseq, nct),
        in_specs=[grp(0), grp(1), grp(2), tap(0), tap(1), tap(2), bias(0), bias(1), bias(2),
                  col(1), col(L), col(L), col(L), table, table, table,
                  pl.BlockSpec((1, D_HYENA), lambda s, c: (0, 0))],
        out_specs=pl.BlockSpec((L, D_HYENA), lambda s, c: (s, 0)),
        out_shape=jax.ShapeDtypeStruct((nseq * L, D_HYENA), BF16),
        scratch_shapes=[pltpu.VMEM((nct, L, tc), F32)],
        compiler_params=pltpu.CompilerParams(
            dimension_semantics=("arbitrary", "arbitrary"), vmem_limit_bytes=_vmem_limit(est)),
    )(proj, proj, proj, conv_w_t, conv_w_t, conv_w_t, conv_b, conv_b, conv_b,
      hy_bias, *spectrum, cm, sm, smt, gn)


def _dot_nt(a, b):
    return lax.dot_general(a, b, (((1,), (1,)), ((), ())), preferred_element_type=F32)


def _head_lanes(kvh, g):
    hd = kvh * KV_GROUP + g
    return slice(hd * HEAD_DIM, (hd + 1) * HEAD_DIM)


def _group_sinks(sink_ref, kvh, rows):
    return jnp.concatenate([jnp.full((rows, 1), sink_ref[kvh * KV_GROUP + g], F32) for g in range(KV_GROUP)],
                           axis=0)


def _head_norm_store(acc_ref, gn_ref, o_ref):
    y = acc_ref[...]
    inv = lax.rsqrt(jnp.mean(y * y, axis=-1, keepdims=True) + RMS_EPS)
    o_ref[...] = (y * inv * gn_ref[...]).astype(o_ref.dtype)


def _attn_ctx_kernel(sink_ref, q_ref, k_ref, v_ref, gn_ref, o_ref, ko_ref, vo_ref, acc_ref):
    ko_ref[...] = k_ref[...]
    vo_ref[...] = v_ref[...]
    for kvh in range(N_KV_HEADS):
        ksl = slice(kvh * HEAD_DIM, (kvh + 1) * HEAD_DIM)
        kh = k_ref[:, ksl].astype(BF16)
        vh = v_ref[:, ksl].astype(BF16)
        for g in range(KV_GROUP):
            hsl = _head_lanes(kvh, g)
            s = _dot_nt(q_ref[:, hsl].astype(BF16), kh) * ATTN_SCALE
            sk = sink_ref[kvh * KV_GROUP + g]
            m = jnp.maximum(jnp.max(s, axis=-1, keepdims=True), sk)
            e = jnp.exp(s - m)
            den = jnp.sum(e, axis=-1, keepdims=True) + jnp.exp(sk - m)
            acc_ref[:, hsl] = jnp.dot((e / den).astype(BF16), vh, preferred_element_type=F32)
    _head_norm_store(acc_ref, gn_ref, o_ref)


def _attn_ctx(proj, sink, gn):
    est = 2 * (SEQ * D_ATTN * 4 + 4 * SEQ * KV_DIM * 4 + SEQ * D_ATTN * 2) + 3 * SEQ * D_ATTN * 4
    return pl.pallas_call(
        _attn_ctx_kernel,
        name="attn_ctx",
        grid=(BATCH,),
        in_specs=[
            pl.BlockSpec(memory_space=pltpu.SMEM),
            pl.BlockSpec((SEQ, D_ATTN), lambda b: (b, O_Q // D_ATTN)),
            pl.BlockSpec((SEQ, KV_DIM), lambda b: (b, O_K // KV_DIM)),
            pl.BlockSpec((SEQ, KV_DIM), lambda b: (b, O_V // KV_DIM)),
            pl.BlockSpec((1, D_ATTN), lambda b: (0, 0)),
        ],
        out_specs=[pl.BlockSpec((SEQ, D_ATTN), lambda b: (b, 0)),
                   pl.BlockSpec((SEQ, KV_DIM), lambda b: (b, 0)),
                   pl.BlockSpec((SEQ, KV_DIM), lambda b: (b, 0))],
        out_shape=[jax.ShapeDtypeStruct((N_CTX, D_ATTN), BF16),
                   jax.ShapeDtypeStruct((N_CTX, KV_DIM), F32),
                   jax.ShapeDtypeStruct((N_CTX, KV_DIM), F32)],
        scratch_shapes=[pltpu.VMEM((SEQ, D_ATTN), F32)],
        compiler_params=pltpu.CompilerParams(
            dimension_semantics=("arbitrary",), vmem_limit_bytes=_vmem_limit(est)),
    )(sink, proj, proj, proj, gn)


def _rope_tables():
    half = HEAD_DIM // 2
    quarter = half // 2
    pos = jnp.arange(DEC_SEQ, dtype=jnp.int32)
    row_pos = (pos // GRID_W).astype(F32)
    col_pos = (pos % GRID_W).astype(F32)
    inv = ROPE_BASE ** (-jnp.arange(quarter, dtype=F32) / quarter)
    lane = jnp.arange(HEAD_DIM, dtype=jnp.int32)
    p = jnp.where((lane < half)[None, :], row_pos[:, None], col_pos[:, None])
    ang = p * inv[lane % quarter][None, :]
    low = ((lane % half) < quarter)[None, :]
    cos = jnp.cos(ang)
    sin = jnp.sin(ang)
    return cos, jnp.where(low, -sin, 0.0), jnp.where(low, 0.0, sin)


ROPE_SHIFT = HEAD_DIM // 4
LAT_SPAN = BLOCK + 2 * WINDOW


def _rope(x, cos, sin_lo, sin_hi):
    return (x * cos + pltpu.roll(x, HEAD_DIM - ROPE_SHIFT, 1) * sin_lo
            + pltpu.roll(x, ROPE_SHIFT, 1) * sin_hi)


def _attn_lat_kernel(sink_ref, q_ref, k_ref, v_ref, ck_ref, cv_ref, cos_ref, slo_ref, shi_ref, gn_ref,
                     o_ref, acc_ref):
    i = pl.program_id(1)
    q0 = pl.multiple_of(i * BLOCK, BLOCK)
    k0 = pl.multiple_of(jnp.clip((i - 1) * BLOCK, 0, DEC_SEQ - LAT_SPAN), BLOCK)
    q_tab = (cos_ref[pl.ds(q0, BLOCK), :], slo_ref[pl.ds(q0, BLOCK), :], shi_ref[pl.ds(q0, BLOCK), :])
    k_tab = (cos_ref[pl.ds(k0, LAT_SPAN), :], slo_ref[pl.ds(k0, LAT_SPAN), :], shi_ref[pl.ds(k0, LAT_SPAN), :])
    q_pos = q0 + lax.broadcasted_iota(jnp.int32, (BLOCK, 1), 0)
    k_pos = k0 + lax.broadcasted_iota(jnp.int32, (1, LAT_SPAN), 1)
    valid = jnp.abs(q_pos - k_pos) <= WINDOW
    valid = jnp.concatenate([valid] * KV_GROUP, axis=0)
    for kvh in range(N_KV_HEADS):
        ksl = slice(kvh * HEAD_DIM, (kvh + 1) * HEAD_DIM)
        kw = _rope(k_ref[pl.ds(k0, LAT_SPAN), ksl], *k_tab).astype(BF16)
        vw = v_ref[pl.ds(k0, LAT_SPAN), ksl].astype(BF16)
        kc = ck_ref[0, :, ksl].astype(BF16)
        vc = cv_ref[0, :, ksl].astype(BF16)
        q = jnp.concatenate([_rope(q_ref[:, _head_lanes(kvh, g)], *q_tab) for g in range(KV_GROUP)],
                            axis=0).astype(BF16)
        sk = _group_sinks(sink_ref, kvh, BLOCK)
        s_lat = jnp.where(valid, _dot_nt(q, kw) * ATTN_SCALE, NEG_INF)
        s_ctx = _dot_nt(q, kc) * ATTN_SCALE
        m = jnp.maximum(jnp.maximum(jnp.max(s_lat, axis=-1, keepdims=True),
                                    jnp.max(s_ctx, axis=-1, keepdims=True)), sk)
        e_lat = jnp.exp(s_lat - m)
        e_ctx = jnp.exp(s_ctx - m)
        den = (jnp.sum(e_lat, axis=-1, keepdims=True) + jnp.sum(e_ctx, axis=-1, keepdims=True)
               + jnp.exp(sk - m))
        o = (jnp.dot((e_lat / den).astype(BF16), vw, preferred_element_type=F32)
             + jnp.dot((e_ctx / den).astype(BF16), vc, preferred_element_type=F32))
        for g in range(KV_GROUP):
            acc_ref[:, _head_lanes(kvh, g)] = o[g * BLOCK:(g + 1) * BLOCK]
    _head_norm_store(acc_ref, gn_ref, o_ref)


def _attn_lat(proj, cache_k, cache_v, sink, rope, gn):
    nqb = DEC_SEQ // BLOCK
    est = (2 * (BLOCK * D_ATTN * 4 + 2 * DEC_SEQ * KV_DIM * 4 + 2 * PAST_LEN * KV_DIM * 4
                + 3 * DEC_SEQ * HEAD_DIM * 4 + BLOCK * D_ATTN * 2) + 8 * BLOCK * D_ATTN * 4)
    lat_row_block0 = N_CTX // BLOCK
    lat_seq_block0 = N_CTX // DEC_SEQ
    tab = pl.BlockSpec((DEC_SEQ, HEAD_DIM), lambda b, i: (0, 0))
    return pl.pallas_call(
        _attn_lat_kernel,
        name="attn_lat",
        grid=(DEC_BATCH, nqb),
        in_specs=[
            pl.BlockSpec(memory_space=pltpu.SMEM),
            pl.BlockSpec((BLOCK, D_ATTN), lambda b, i: (lat_row_block0 + b * nqb + i, O_Q // D_ATTN)),
            pl.BlockSpec((DEC_SEQ, KV_DIM), lambda b, i: (lat_seq_block0 + b, O_K // KV_DIM)),
            pl.BlockSpec((DEC_SEQ, KV_DIM), lambda b, i: (lat_seq_block0 + b, O_V // KV_DIM)),
            pl.BlockSpec((1, PAST_LEN, KV_DIM), lambda b, i: (b, 0, 0)),
            pl.BlockSpec((1, PAST_LEN, KV_DIM), lambda b, i: (b, 0, 0)),
            tab, tab, tab,
            pl.BlockSpec((1, D_ATTN), lambda b, i: (0, 0)),
        ],
        out_specs=pl.BlockSpec((BLOCK, D_ATTN), lambda b, i: (b * nqb + i, 0)),
        out_shape=jax.ShapeDtypeStruct((N_LAT, D_ATTN), BF16),
        scratch_shapes=[pltpu.VMEM((BLOCK, D_ATTN), F32)],
        compiler_params=pltpu.CompilerParams(
            dimension_semantics=("arbitrary", "arbitrary"), vmem_limit_bytes=_vmem_limit(est)),
    )(sink, proj, proj, proj, cache_k, cache_v, *rope, gn)


def kernel(x_prompt, x_sample, cache_k, cache_v, c, c_ctx, w_ada, b_ada, norm_mix, w_in, hy_conv_w, hy_conv_b, filt_w1, filt_b1, filt_w2, filt_b2, filt_w3, filt_freq, hy_bias, attn_sink, gn_hyena, gn_attn, w_out, norm_ffn, w_up, ffn_conv_w, ffn_conv_b, w_down, final_norm):
    x = (x_prompt.reshape(N_CTX, D_MODEL), x_sample.reshape(N_LAT, D_MODEL))
    cond = jnp.concatenate([c_ctx[None, :], c, jnp.zeros((MOD_ROWS - 1 - DEC_BATCH, D_MODEL), F32)], axis=0)
    mod_all = _modulation(cond, w_ada, b_ada)

    rope = _rope_tables()
    tables = {L: _dft_tables(L) for L in (SEQ, DEC_SEQ)}
    feats = {L: _filter_features(L) for L in (SEQ, DEC_SEQ)}
    deltas = jnp.linspace(MIN_DECAY, MAX_DECAY, D_HYENA, dtype=F32)[None, :]

    new_k = []
    new_v = []
    for i in range(DEPTH):
        mod = mod_all[i]
        h = _prenorm(x, norm_mix[i], mod, 0, 1)
        proj = _matmul(h, w_in, i, 1024, 1024)

        hy_w_t = hy_conv_w[i].T
        hy_b = hy_conv_b[i][None, :]
        w1p = jnp.pad(filt_w1[i], ((0, FILT_PAD - FILTER_EMB), (0, 0)))
        y_a = []
        for L, tc, nseq, rb0 in ((SEQ, 1024, BATCH, 0), (DEC_SEQ, 256, DEC_BATCH, N_CTX // DEC_SEQ)):
            cm, sm, smt = tables[L]
            spectrum = _filter_spectrum(L, feats[L], w1p, filt_b1[i][None, :], filt_w2[i], filt_b2[i][None, :],
                                        filt_w3[i], filt_freq[i][None, :], deltas, cm, sm)
            y_a.append(_hyena(proj, hy_w_t, hy_b, hy_bias[i][None, :], spectrum, cm, sm, smt,
                              gn_hyena[i][None, :], L=L, tc=tc, nseq=nseq, row_block0=rb0))
        gn_a = gn_attn[i][None, :]
        yb_ctx, k_ctx, v_ctx = _attn_ctx(proj, attn_sink[i], gn_a)
        yb_lat = _attn_lat(proj, cache_k[:, i].reshape(DEC_BATCH, PAST_LEN, KV_DIM),
                           cache_v[:, i].reshape(DEC_BATCH, PAST_LEN, KV_DIM), attn_sink[i], rope, gn_a)
        x = _matmul_residual([tuple(y_a), (yb_ctx, yb_lat)], w_out, i, x, mod, 2, 512, 1024, "out_proj")

        h = _prenorm(x, norm_ffn[i], mod, 3, 4)
        ffn_w_t = ffn_conv_w[i].T
        ffn_b = ffn_conv_b[i][None, :]
        n_ffn_blocks = D_FF // FFN_TN
        acts = [_ffn_up2(h, w_up, i, ffn_w_t, ffn_b)]
        if n_ffn_blocks % 2:
            acts.append(_ffn_up(h, w_up, i, ffn_w_t, ffn_b, n_ffn_blocks - 1, 1))
        x = _matmul_residual(acts, w_down, i, x, mod, 5, 512, 512, "ffn_down")

        new_k.append(k_ctx.reshape(BATCH, SEQ, N_KV_HEADS, HEAD_DIM))
        new_v.append(v_ctx.reshape(BATCH, SEQ, N_KV_HEADS, HEAD_DIM))

    y_prompt = _final_norm(x, final_norm, 0, N_CTX).reshape(BATCH, SEQ, D_MODEL)
    y_sample = _final_norm(x, final_norm, N_CTX // NORM_TR, N_LAT).reshape(DEC_BATCH, DEC_SEQ, D_MODEL)
    return (y_prompt, y_sample, jnp.stack(new_k, axis=1), jnp.stack(new_v, axis=1))
```
